```python
import jax, jax.numpy as jnp
from jax import lax
import numpy as np

D_MODEL = 1024
BATCH = 16
SEQ = 2048
DEPTH = 1

GRID_W = 64
CTX_LEN = 256

RET_HEADS = 4
RET_DK = 256
RET_DV = 512
RET_CHUNK = 128
RET_QK_W = RET_HEADS * RET_DK
RET_V_W = RET_HEADS * RET_DV

ATT_HEADS = 8
ATT_KV_HEADS = 2
ATT_HEAD_DIM = 128
ATT_Q_W = ATT_HEADS * ATT_HEAD_DIM
ATT_KV_W = ATT_KV_HEADS * ATT_HEAD_DIM
ROPE_THETA = 10000.0
Q_BLOCK = 128

NORM_EPS = 1e-6

IN_SPLITS = (RET_QK_W, RET_V_W, ATT_KV_W, ATT_KV_W,
             RET_QK_W, RET_V_W, ATT_Q_W, ATT_Q_W, D_MODEL, D_MODEL)
KV_COLS = RET_QK_W + RET_V_W + 2 * ATT_KV_W
IN_COLS = KV_COLS + RET_QK_W + RET_V_W + 2 * ATT_Q_W + 2 * D_MODEL

kernel_name = "hybrid_retention_gqa_prefix_dit_block"


def rms_norm(x, w=None):
    xf = x.astype(jnp.float32)
    y = xf * lax.rsqrt(jnp.mean(xf * xf, axis=-1, keepdims=True) + NORM_EPS)
    if w is not None:
        y = y * w.astype(jnp.float32)
    return y.astype(x.dtype)


def split_cols(p, widths):
    out, start = [], 0
    for w in widths:
        out.append(p[..., start:start + w])
        start += w
    return out


def adaln_params(cvec, w_ada, b_ada):
    mod = jax.nn.silu(cvec) @ w_ada + b_ada
    return split_cols(mod, (D_MODEL, D_MODEL, D_MODEL))


def axial_rope(L):
    rows = L // GRID_W
    row = jnp.repeat(jnp.arange(rows, dtype=jnp.float32), GRID_W)
    col = jnp.tile(jnp.arange(GRID_W, dtype=jnp.float32), rows)
    half = ATT_HEAD_DIM // 2
    freqs = ROPE_THETA ** (-jnp.arange(0, half, 2, dtype=jnp.float32) / half)
    ang = jnp.concatenate([row[:, None] * freqs, col[:, None] * freqs], axis=-1)
    return jnp.cos(ang), jnp.sin(ang)


def apply_rope(x, cos, sin):
    B, L, H, D = x.shape
    xp = x.astype(jnp.float32).reshape(B, L, H, D // 2, 2)
    x0, x1 = xp[..., 0], xp[..., 1]
    c, s = cos[None, :, None, :], sin[None, :, None, :]
    out = jnp.stack([x0 * c - x1 * s, x0 * s + x1 * c], axis=-1)
    return out.reshape(B, L, H, D).astype(x.dtype)


def ret_kv(k_flat, v_flat):
    B, L, _ = k_flat.shape
    k = k_flat.reshape(B, L, RET_HEADS, RET_DK) * (RET_DK ** -0.5)
    v = v_flat.reshape(B, L, RET_HEADS, RET_DV)
    return k, v


def retention_scan(q, k, v, log_gamma, s0):
    B, L, H, _ = q.shape
    n = L // RET_CHUNK

    def chunks(t):
        return t.astype(jnp.float32).reshape(B, n, RET_CHUNK, H, t.shape[-1]).transpose(1, 0, 3, 2, 4)

    qc, kc, vc = chunks(q), chunks(k), chunks(v)
    idx = jnp.arange(RET_CHUNK, dtype=jnp.float32)
    lg = log_gamma.astype(jnp.float32)[:, None]
    rel = idx[:, None] - idx[None, :]
    decay_mask = jnp.where(rel >= 0, jnp.exp(lg[:, :, None] * jnp.maximum(rel, 0.0)), 0.0)
    q_decay = jnp.exp(lg * (idx + 1.0))
    k_decay = jnp.exp(lg * (RET_CHUNK - 1.0 - idx))
    chunk_decay = jnp.exp(lg[:, 0] * RET_CHUNK)

    def step(s, inp):
        qi, ki, vi = inp
        scores = jnp.einsum('bhid,bhjd->bhij', qi, ki) * decay_mask
        intra = jnp.einsum('bhij,bhje->bhie', scores, vi)
        cross = jnp.einsum('bhid,bhde->bhie', qi * q_decay[..., None], s)
        s_new = s * chunk_decay[:, None, None] + jnp.einsum('bhjd,bhje->bhde', ki * k_decay[..., None], vi)
        return s_new, intra + cross

    _, out = lax.scan(step, s0.astype(jnp.float32), (qc, kc, vc))
    return out.transpose(1, 0, 3, 2, 4).reshape(B, L, H, v.shape[-1])


def ctx_final_states(k, v, log_gamma):
    L = k.shape[1]
    pos = jnp.arange(L, dtype=jnp.float32)
    w_f = jnp.exp(log_gamma[0][:, None] * (L - 1.0 - pos))
    w_b = jnp.exp(log_gamma[1][:, None] * pos)
    kf, vf = k.astype(jnp.float32), v.astype(jnp.float32)
    s_f = jnp.einsum('blhd,hl,blhe->bhde', kf, w_f, vf)
    s_b = jnp.einsum('blhd,hl,blhe->bhde', kf, w_b, vf)
    return s_f, s_b


def retention_branch(q_flat, k, v, gate_flat, log_gamma, states):
    B, L, _ = q_flat.shape
    q = q_flat.reshape(B, L, RET_HEADS, RET_DK)
    o_f = retention_scan(q, k, v, log_gamma[0], states[0])
    o_b = retention_scan(q[:, ::-1], k[:, ::-1], v[:, ::-1], log_gamma[1], states[1])[:, ::-1]
    o = rms_norm(o_f + o_b)
    return o.reshape(B, L, RET_V_W).astype(gate_flat.dtype) * jax.nn.silu(gate_flat)


def att_heads(t_flat, n_heads, norm_w, rope):
    B, L, _ = t_flat.shape
    t = rms_norm(t_flat.reshape(B, L, n_heads, ATT_HEAD_DIM), norm_w)
    if rope is not None:
        t = apply_rope(t, rope[0], rope[1])
    return t


def gqa_block_attention(q, k_all, v_all):
    B, S, Hq, D = q.shape
    nb = S // Q_BLOCK
    rep = Hq // ATT_KV_HEADS
    qb = q.reshape(B, nb, Q_BLOCK, ATT_KV_HEADS, rep, D).transpose(1, 0, 2, 3, 4, 5)
    scale = D ** -0.5

    def one_block(qi):
        s = jnp.einsum('bqgrd,bkgd->bgrqk', qi, k_all).astype(jnp.float32) * scale
        p = jax.nn.softmax(s, axis=-1)
        return jnp.einsum('bgrqk,bkgd->bqgrd', p.astype(v_all.dtype), v_all)

    o = lax.map(one_block, qb)
    return o.transpose(1, 0, 2, 3, 4, 5).reshape(B, S, Hq * D)


def merge_branches(y_ret, y_att, mg_ret, mg_att, w_o_ret, w_o_att, w_out):
    y = jax.nn.sigmoid(mg_ret) * (y_ret @ w_o_ret) + jax.nn.sigmoid(mg_att) * (y_att @ w_o_att)
    return y @ w_out


def trunk_layer(x, ctx, c, c_ctx, norm_w, w_ada, b_ada, w_in, ret_log2_decay,
                q_norm_w, k_norm_w, w_o_ret, w_o_att, w_out, update_ctx):
    B, S, _ = x.shape
    shift, scale, gate = adaln_params(c, w_ada, b_ada)
    shift_c, scale_c, gate_c = adaln_params(c_ctx, w_ada, b_ada)
    hx = rms_norm(x, norm_w) * (1 + scale[:, None]) + shift[:, None]
    hc = rms_norm(ctx, norm_w) * (1 + scale_c) + shift_c
    log_gamma = jnp.log1p(-jnp.exp2(ret_log2_decay.astype(jnp.float32)))

    px = hx @ w_in
    ret_k, ret_v, att_k, att_v, ret_q, ret_g, att_q, att_g, mg_ret, mg_att = split_cols(px, IN_SPLITS)
    pc = hc @ (w_in if update_ctx else w_in[:, :KV_COLS])
    c_ret_k, c_ret_v, c_att_k, c_att_v = split_cols(pc, IN_SPLITS[:4])

    kc, vc = ret_kv(c_ret_k, c_ret_v)
    ctx_states = ctx_final_states(kc, vc, log_gamma)
    kca = att_heads(c_att_k, ATT_KV_HEADS, k_norm_w, None)
    vca = c_att_v.reshape(B, CTX_LEN, ATT_KV_HEADS, ATT_HEAD_DIM)

    kx, vx = ret_kv(ret_k, ret_v)
    y_ret = retention_branch(ret_q, kx, vx, ret_g, log_gamma, ctx_states)

    rope = axial_rope(S)
    qxa = att_heads(att_q, ATT_HEADS, q_norm_w, rope)
    kxa = att_heads(att_k, ATT_KV_HEADS, k_norm_w, rope)
    vxa = att_v.reshape(B, S, ATT_KV_HEADS, ATT_HEAD_DIM)
    k_all = jnp.concatenate([kxa, kca], axis=1)
    v_all = jnp.concatenate([vxa, vca], axis=1)
    y_att = gqa_block_attention(qxa, k_all, v_all) * jax.nn.silu(att_g)

    out = merge_branches(y_ret, y_att, mg_ret, mg_att, w_o_ret, w_o_att, w_out)
    x_new = x + gate[:, None] * out

    if update_ctx:
        _, _, _, _, c_ret_q, c_ret_g, c_att_q, c_att_g, c_mg_ret, c_mg_att = split_cols(pc, IN_SPLITS)
        zero = jnp.zeros((B, RET_HEADS, RET_DK, RET_DV), jnp.float32)
        yc_ret = retention_branch(c_ret_q, kc, vc, c_ret_g, log_gamma, (zero, zero))
        qca = att_heads(c_att_q, ATT_HEADS, q_norm_w, None)
        yc_att = gqa_block_attention(qca, kca, vca) * jax.nn.silu(c_att_g)
        out_c = merge_branches(yc_ret, yc_att, c_mg_ret, c_mg_att, w_o_ret, w_o_att, w_out)
        ctx = ctx + gate_c * out_c
    return x_new, ctx


def setup_inputs(seed: int = 0) -> dict:
    key = jax.random.key(seed)
    ks = jax.random.split(key, 15)
    f32 = jnp.float32
    nrm = lambda k, shape: jax.random.normal(k, shape, f32)
    x = nrm(ks[0], (BATCH, SEQ, D_MODEL))
    c = nrm(ks[1], (BATCH, D_MODEL))
    ctx = nrm(ks[2], (BATCH, CTX_LEN, D_MODEL))
    c_ctx = nrm(ks[3], (D_MODEL,))
    norm_w = 1.0 + 0.02 * nrm(ks[4], (DEPTH, D_MODEL))
    w_ada = nrm(ks[5], (DEPTH, D_MODEL, 3 * D_MODEL)) * (0.5 * D_MODEL ** -0.5)
    b_ada = 0.02 * nrm(ks[6], (DEPTH, 3 * D_MODEL))
    w_in = nrm(ks[7], (DEPTH, D_MODEL, IN_COLS)) * (D_MODEL ** -0.5)
    base = -5.0 - jnp.arange(RET_HEADS, dtype=f32)
    ret_log2_decay = base[None, None, :] + 0.1 * nrm(ks[8], (DEPTH, 2, RET_HEADS))
    q_norm_w = 1.0 + 0.02 * nrm(ks[9], (DEPTH, ATT_HEAD_DIM))
    k_norm_w = 1.0 + 0.02 * nrm(ks[10], (DEPTH, ATT_HEAD_DIM))
    w_o_ret = nrm(ks[11], (DEPTH, RET_V_W, D_MODEL)) * (RET_V_W ** -0.5)
    w_o_att = nrm(ks[12], (DEPTH, ATT_Q_W, D_MODEL)) * (ATT_Q_W ** -0.5)
    w_out = nrm(ks[13], (DEPTH, D_MODEL, D_MODEL)) * (D_MODEL ** -0.5)
    return {"x": x, "c": c, "ctx": ctx, "c_ctx": c_ctx, "norm_w": norm_w,
            "w_ada": w_ada, "b_ada": b_ada, "w_in": w_in, "ret_log2_decay": ret_log2_decay,
            "q_norm_w": q_norm_w, "k_norm_w": k_norm_w, "w_o_ret": w_o_ret,
            "w_o_att": w_o_att, "w_out": w_out}


def reference(x, c, ctx, c_ctx, norm_w, w_ada, b_ada, w_in, ret_log2_decay,
              q_norm_w, k_norm_w, w_o_ret, w_o_att, w_out):
    for layer in range(DEPTH):
        x, ctx = trunk_layer(x, ctx, c, c_ctx, norm_w[layer], w_ada[layer], b_ada[layer],
                             w_in[layer], ret_log2_decay[layer], q_norm_w[layer], k_norm_w[layer],
                             w_o_ret[layer], w_o_att[layer], w_out[layer],
                             update_ctx=(layer < DEPTH - 1))
    return x
```

```python
import functools

import jax
import jax.numpy as jnp
from jax import lax
from jax.experimental import pallas as pl
from jax.experimental.pallas import tpu as pltpu

D_MODEL = 1024
GRID_W = 64
RET_HEADS = 4
RET_DK = 256
RET_DV = 512
RET_QK_W = RET_HEADS * RET_DK
RET_V_W = RET_HEADS * RET_DV
ATT_HEADS = 8
ATT_KV_HEADS = 2
ATT_HEAD_DIM = 128
ATT_Q_W = ATT_HEADS * ATT_HEAD_DIM
ATT_KV_W = ATT_KV_HEADS * ATT_HEAD_DIM
ATT_REP = ATT_HEADS // ATT_KV_HEADS
ROPE_THETA = 10000.0
NORM_EPS = 1e-6
KV_COLS = RET_QK_W + RET_V_W + 2 * ATT_KV_W

OFF_RET_K = 0
OFF_RET_V = OFF_RET_K + RET_QK_W
OFF_RET_Q = OFF_RET_V + RET_V_W
OFF_RET_G = OFF_RET_Q + RET_QK_W
OFF_ATT_Q = OFF_RET_G + RET_V_W
OFF_ATT_G = OFF_ATT_Q + ATT_Q_W
OFF_MG = OFF_ATT_G + ATT_Q_W
OFF_ATT_K = OFF_MG + 2 * D_MODEL
OFF_ATT_V = OFF_ATT_K + ATT_KV_W
IN_COLS = OFF_ATT_V + ATT_KV_W
CTX_OFF_ATT_K = RET_QK_W + RET_V_W
CTX_OFF_ATT_V = CTX_OFF_ATT_K + ATT_KV_W

RET_CHUNK = 256
VMEM_LIMIT = 56 * 1024 * 1024

BF16 = jnp.bfloat16
F32 = jnp.float32


def _dot(a, b):
    return jnp.dot(a, b, preferred_element_type=F32)


def _dot_nt(a, b):
    return lax.dot_general(a, b, (((1,), (1,)), ((), ())), preferred_element_type=F32)


def _dot_tn(a, b):
    return lax.dot_general(a, b, (((0,), (0,)), ((), ())), preferred_element_type=F32)


def _silu(x):
    return x * jax.nn.sigmoid(x)


def _adaln_kernel(c_ref, w_ref, b_ref, o_ref):
    o_ref[...] = _dot(_silu(c_ref[...]), w_ref[...]) + b_ref[...]


def _adaln(cc, w_ada, b_ada):
    rows = cc.shape[0]
    tn = D_MODEL
    return pl.pallas_call(
        _adaln_kernel,
        grid=(3 * D_MODEL // tn,),
        in_specs=[pl.BlockSpec((rows, D_MODEL), lambda j: (0, 0)),
                  pl.BlockSpec((D_MODEL, tn), lambda j: (0, j)),
                  pl.BlockSpec((1, tn), lambda j: (0, j))],
        out_specs=pl.BlockSpec((rows, tn), lambda j: (0, j)),
        out_shape=jax.ShapeDtypeStruct((rows, 3 * D_MODEL), F32),
        compiler_params=pltpu.CompilerParams(vmem_limit_bytes=VMEM_LIMIT),
        name="adaln",
    )(cc, w_ada, b_ada.reshape(1, -1))


def _inproj_kernel(x_ref, nw_ref, sc_ref, sh_ref, w_ref, o_ref, h_ref):
    @pl.when(pl.program_id(1) == 0)
    def _():
        x = x_ref[...]
        ms = jnp.mean(x * x, axis=-1, keepdims=True)
        y = x * lax.rsqrt(ms + NORM_EPS) * nw_ref[...]
        h_ref[...] = (y * (1.0 + sc_ref[0]) + sh_ref[0]).astype(BF16)

    o_ref[...] = _dot(h_ref[...], w_ref[...]).astype(BF16)


def _inproj(x2d, norm_w, scale, shift, w_bf, n_cols, rows_per_mod, tm, tn):
    rows = x2d.shape[0]
    mod_map = lambda i, j: ((i * tm) // rows_per_mod, 0, 0)
    return pl.pallas_call(
        _inproj_kernel,
        grid=(rows // tm, n_cols // tn),
        in_specs=[pl.BlockSpec((tm, D_MODEL), lambda i, j: (i, 0)),
                  pl.BlockSpec((1, D_MODEL), lambda i, j: (0, 0)),
                  pl.BlockSpec((1, 1, D_MODEL), mod_map),
                  pl.BlockSpec((1, 1, D_MODEL), mod_map),
                  pl.BlockSpec((D_MODEL, tn), lambda i, j: (0, j))],
        out_specs=pl.BlockSpec((tm, tn), lambda i, j: (i, j)),
        out_shape=jax.ShapeDtypeStruct((rows, n_cols), BF16),
        scratch_shapes=[pltpu.VMEM((tm, D_MODEL), BF16)],
        compiler_params=pltpu.CompilerParams(
            dimension_semantics=("parallel", "arbitrary"), vmem_limit_bytes=VMEM_LIMIT),
        name="inproj",
    )(x2d, norm_w.reshape(1, -1), scale, shift, w_bf)


def _head_rms(t, w):
    ms = jnp.mean(t * t, axis=-1, keepdims=True)
    return t * lax.rsqrt(ms + NORM_EPS) * w


def _rope(t, cos, sin_signed):
    width = t.shape[-1]
    lane = lax.broadcasted_iota(jnp.int32, t.shape, 1)
    nxt = pltpu.roll(t, width - 1, 1)
    prv = pltpu.roll(t, 1, 1)
    partner = jnp.where(lane % 2 == 0, nxt, prv)
    return t * cos + partner * sin_signed


def _kv_prep_kernel(kl_ref, vl_ref, kc_ref, vc_ref, kw_ref, cos_ref, sin_ref, ko_ref, vo_ref):
    s_len = kl_ref.shape[1]
    kw = kw_ref[...]
    cos = cos_ref[...]
    sin = sin_ref[...]
    for g in range(ATT_KV_HEADS):
        cols = slice(g * ATT_HEAD_DIM, (g + 1) * ATT_HEAD_DIM)
        kl = _head_rms(kl_ref[0, :, cols].astype(F32), kw)
        ko_ref[0, :s_len, cols] = _rope(kl, cos, sin).astype(BF16)
        kc = _head_rms(kc_ref[0, :, cols].astype(F32), kw)
        ko_ref[0, s_len:, cols] = kc.astype(BF16)
    vo_ref[0, :s_len, :] = vl_ref[0]
    vo_ref[0, s_len:, :] = vc_ref[0]


def _kv_prep(px3, pc3, k_norm_w, cos, sin_signed):
    b, s_len, _ = px3.shape
    c_len = pc3.shape[1]
    w = ATT_KV_W
    out = jax.ShapeDtypeStruct((b, s_len + c_len, w), BF16)
    return pl.pallas_call(
        _kv_prep_kernel,
        grid=(b,),
        in_specs=[pl.BlockSpec((1, s_len, w), lambda i: (i, 0, OFF_ATT_K // w)),
                  pl.BlockSpec((1, s_len, w), lambda i: (i, 0, OFF_ATT_V // w)),
                  pl.BlockSpec((1, c_len, w), lambda i: (i, 0, CTX_OFF_ATT_K // w)),
                  pl.BlockSpec((1, c_len, w), lambda i: (i, 0, CTX_OFF_ATT_V // w)),
                  pl.BlockSpec((1, ATT_HEAD_DIM), lambda i: (0, 0)),
                  pl.BlockSpec((s_len, ATT_HEAD_DIM), lambda i: (0, 0)),
                  pl.BlockSpec((s_len, ATT_HEAD_DIM), lambda i: (0, 0))],
        out_specs=[pl.BlockSpec((1, s_len + c_len, w), lambda i: (i, 0, 0)),
                   pl.BlockSpec((1, s_len + c_len, w), lambda i: (i, 0, 0))],
        out_shape=[out, out],
        compiler_params=pltpu.CompilerParams(
            dimension_semantics=("parallel",), vmem_limit_bytes=VMEM_LIMIT),
        name="kv_prep",
    )(px3, px3, pc3, pc3, k_norm_w.reshape(1, -1), cos, sin_signed)


def _attention_kernel(q_ref, k_ref, v_ref, g_ref, qw_ref, cos_ref, sin_ref, o_ref):
    qw = qw_ref[...] * (ATT_HEAD_DIM ** -0.5)
    cos = cos_ref[...]
    sin = sin_ref[...]
    k = k_ref[0]
    v = v_ref[0]
    for r in range(ATT_REP):
        cols = slice(r * ATT_HEAD_DIM, (r + 1) * ATT_HEAD_DIM)
        q = _rope(_head_rms(q_ref[0, :, cols].astype(F32), qw), cos, sin).astype(BF16)
        s = _dot_nt(q, k)
        m = jnp.max(s, axis=-1, keepdims=True)
        p = jnp.exp(s - m)
        l = jnp.sum(p, axis=-1, keepdims=True)
        o = _dot(p.astype(BF16), v) / l
        o_ref[0, :, cols] = (o * _silu(g_ref[0, :, cols].astype(F32))).astype(BF16)


def _attention(px3, k_all, v_all, q_norm_w, cos, sin_signed, tq):
    b, s_len, _ = px3.shape
    l_k = k_all.shape[1]
    gw = ATT_REP * ATT_HEAD_DIM
    return pl.pallas_call(
        _attention_kernel,
        grid=(b, ATT_KV_HEADS, s_len // tq),
        in_specs=[pl.BlockSpec((1, tq, gw), lambda i, g, t: (i, t, OFF_ATT_Q // gw + g)),
                  pl.BlockSpec((1, l_k, ATT_HEAD_DIM), lambda i, g, t: (i, 0, g)),
                  pl.BlockSpec((1, l_k, ATT_HEAD_DIM), lambda i, g, t: (i, 0, g)),
                  pl.BlockSpec((1, tq, gw), lambda i, g, t: (i, t, OFF_ATT_G // gw + g)),
                  pl.BlockSpec((1, ATT_HEAD_DIM), lambda i, g, t: (0, 0)),
                  pl.BlockSpec((tq, ATT_HEAD_DIM), lambda i, g, t: (t, 0)),
                  pl.BlockSpec((tq, ATT_HEAD_DIM), lambda i, g, t: (t, 0))],
        out_specs=pl.BlockSpec((1, tq, gw), lambda i, g, t: (i, t, g)),
        out_shape=jax.ShapeDtypeStruct((b, s_len, ATT_Q_W), BF16),
        compiler_params=pltpu.CompilerParams(
            dimension_semantics=("parallel", "parallel", "arbitrary"),
            vmem_limit_bytes=VMEM_LIMIT),
        name="attention",
    )(px3, k_all, v_all, px3, q_norm_w.reshape(1, -1), cos, sin_signed)


def _retention_kernel(d_ref, q_ref, k_ref, v_ref, g_ref, kc_ref, vc_ref, o_ref,
                      sf_ref, sb_ref, acc_ref):
    s_len = q_ref.shape[1]
    c_len = kc_ref.shape[1]
    ch = RET_CHUNK
    n_chunks = s_len // ch

    d = d_ref[0]
    lg = jnp.log1p(-jnp.exp2(d))
    lg_f = lg[0:1, :]
    lg_b = lg[1:2, :]

    idx = lax.broadcasted_iota(jnp.int32, (ch, 1), 0).astype(F32)
    qd_f = jnp.exp(lg_f * (idx + 1.0)) * (RET_DK ** -0.5)
    kd_f = jnp.exp(lg_f * (ch - 1.0 - idx))
    qd_b = jnp.exp(lg_b * (ch - idx)) * (RET_DK ** -0.5)
    kd_b = jnp.exp(lg_b * idx)
    cd_f = jnp.exp(lg_f * ch)
    cd_b = jnp.exp(lg_b * ch)

    row = lax.broadcasted_iota(jnp.int32, (ch, ch), 0)
    col = lax.broadcasted_iota(jnp.int32, (ch, ch), 1)
    rel = (row - col).astype(F32)
    mask = jnp.where(rel > 0, jnp.exp(lg_f * rel),
                     jnp.where(rel < 0, jnp.exp(-lg_b * rel), 2.0)) * (RET_DK ** -0.5)

    pos = lax.broadcasted_iota(jnp.int32, (c_len, 1), 0).astype(F32)
    kc = kc_ref[0].astype(F32)
    vc = vc_ref[0]
    sf_ref[...] = _dot_tn((kc * jnp.exp(lg_f * (c_len - 1.0 - pos))).astype(BF16), vc)
    sb_ref[...] = _dot_tn((kc * jnp.exp(lg_b * pos)).astype(BF16), vc)

    def fwd(n, carry):
        rows = pl.ds(pl.multiple_of(n * ch, ch), ch)
        q = q_ref[0, rows, :]
        k = k_ref[0, rows, :]
        v = v_ref[0, rows, :]
        scores = _dot_nt(q, k) * mask
        intra = _dot(scores.astype(BF16), v)
        cross = _dot((q.astype(F32) * qd_f).astype(BF16), sf_ref[...].astype(BF16))
        acc_ref[rows, :] = intra + cross
        sf_ref[...] = sf_ref[...] * cd_f + _dot_tn((k.astype(F32) * kd_f).astype(BF16), v)
        return carry

    lax.fori_loop(0, n_chunks, fwd, 0)

    def bwd(i, carry):
        n = n_chunks - 1 - i
        rows = pl.ds(pl.multiple_of(n * ch, ch), ch)
        q = q_ref[0, rows, :]
        k = k_ref[0, rows, :]
        v = v_ref[0, rows, :]
        cross = _dot((q.astype(F32) * qd_b).astype(BF16), sb_ref[...].astype(BF16))
        o = acc_ref[rows, :] + cross
        ms = jnp.mean(o * o, axis=-1, keepdims=True)
        o = o * lax.rsqrt(ms + NORM_EPS)
        o_ref[0, rows, :] = (o * _silu(g_ref[0, rows, :].astype(F32))).astype(BF16)
        sb_ref[...] = sb_ref[...] * cd_b + _dot_tn((k.astype(F32) * kd_b).astype(BF16), v)
        return carry

    lax.fori_loop(0, n_chunks, bwd, 0)


def _retention(px3, pc3, decay3):
    b, s_len, _ = px3.shape
    c_len = pc3.shape[1]
    return pl.pallas_call(
        _retention_kernel,
        grid=(b, RET_HEADS),
        in_specs=[pl.BlockSpec((1, 2, 1), lambda i, h: (h, 0, 0)),
                  pl.BlockSpec((1, s_len, RET_DK), lambda i, h: (i, 0, OFF_RET_Q // RET_DK + h)),
                  pl.BlockSpec((1, s_len, RET_DK), lambda i, h: (i, 0, OFF_RET_K // RET_DK + h)),
                  pl.BlockSpec((1, s_len, RET_DV), lambda i, h: (i, 0, OFF_RET_V // RET_DV + h)),
                  pl.BlockSpec((1, s_len, RET_DV), lambda i, h: (i, 0, OFF_RET_G // RET_DV + h)),
                  pl.BlockSpec((1, c_len, RET_DK), lambda i, h: (i, 0, h)),
                  pl.BlockSpec((1, c_len, RET_DV), lambda i, h: (i, 0, RET_QK_W // RET_DV + h))],
        out_specs=pl.BlockSpec((1, s_len, RET_DV), lambda i, h: (i, 0, h)),
        out_shape=jax.ShapeDtypeStruct((b, s_len, RET_V_W), BF16),
        scratch_shapes=[pltpu.VMEM((RET_DK, RET_DV), F32),
                        pltpu.VMEM((RET_DK, RET_DV), F32),
                        pltpu.VMEM((s_len, RET_DV), F32)],
        compiler_params=pltpu.CompilerParams(
            dimension_semantics=("parallel", "arbitrary"), vmem_limit_bytes=VMEM_LIMIT),
        name="retention",
    )(decay3, px3, px3, px3, px3, pc3, pc3)


def _merge_kernel(x_ref, yr_ref, ya_ref, mg_ref, gate_ref, wr_ref, wa_ref, wo_ref, o_ref):
    a = _dot(yr_ref[...], wr_ref[...])
    b = _dot(ya_ref[...], wa_ref[...])
    mg = mg_ref[...].astype(F32)
    y = jax.nn.sigmoid(mg[:, :D_MODEL]) * a + jax.nn.sigmoid(mg[:, D_MODEL:]) * b
    out = _dot(y.astype(BF16), wo_ref[...])
    o_ref[...] = x_ref[...] + gate_ref[0] * out


def _merge(x2d, y_ret, y_att, px, gate, w_o_ret, w_o_att, w_out, rows_per_mod, tm):
    rows = x2d.shape[0]
    const = lambda i: (0, 0)
    return pl.pallas_call(
        _merge_kernel,
        grid=(rows // tm,),
        in_specs=[pl.BlockSpec((tm, D_MODEL), lambda i: (i, 0)),
                  pl.BlockSpec((tm, RET_V_W), lambda i: (i, 0)),
                  pl.BlockSpec((tm, ATT_Q_W), lambda i: (i, 0)),
                  pl.BlockSpec((tm, 2 * D_MODEL), lambda i: (i, OFF_MG // (2 * D_MODEL))),
                  pl.BlockSpec((1, 1, D_MODEL), lambda i: ((i * tm) // rows_per_mod, 0, 0)),
                  pl.BlockSpec((RET_V_W, D_MODEL), const),
                  pl.BlockSpec((ATT_Q_W, D_MODEL), const),
                  pl.BlockSpec((D_MODEL, D_MODEL), const)],
        out_specs=pl.BlockSpec((tm, D_MODEL), lambda i: (i, 0)),
        out_shape=jax.ShapeDtypeStruct((rows, D_MODEL), F32),
        compiler_params=pltpu.CompilerParams(
            dimension_semantics=("parallel",), vmem_limit_bytes=VMEM_LIMIT),
        name="merge",
    )(x2d, y_ret, y_att, px, gate, w_o_ret, w_o_att, w_out)


def _rope_tables(s_len):
    rows = s_len // GRID_W
    row = jnp.repeat(jnp.arange(rows, dtype=F32), GRID_W)
    col = jnp.tile(jnp.arange(GRID_W, dtype=F32), rows)
    half = ATT_HEAD_DIM // 2
    freqs = ROPE_THETA ** (-jnp.arange(0, half, 2, dtype=F32) / half)
    ang = jnp.concatenate([row[:, None] * freqs, col[:, None] * freqs], axis=-1)
    cos = jnp.repeat(jnp.cos(ang), 2, axis=-1)
    sin = jnp.repeat(jnp.sin(ang), 2, axis=-1)
    sign = jnp.tile(jnp.array([-1.0, 1.0], F32), ATT_HEAD_DIM // 2)
    return cos, sin * sign


def _layer(x, ctx, c, c_ctx, norm_w, w_ada, b_ada, w_in, ret_log2_decay,
           q_norm_w, k_norm_w, w_o_ret, w_o_att, w_out):
    b, s_len, _ = x.shape
    c_len = ctx.shape[1]

    mod = _adaln(jnp.concatenate([c, c_ctx[None]], axis=0), w_ada, b_ada)
    shift, scale, gate = (mod[:, i * D_MODEL:(i + 1) * D_MODEL] for i in range(3))
    lat = lambda t: t[:b].reshape(b, 1, D_MODEL)
    cx = lambda t: t[b:].reshape(1, 1, D_MODEL)

    w_lat = jnp.concatenate([w_in[:, :CTX_OFF_ATT_K], w_in[:, KV_COLS:],
                             w_in[:, CTX_OFF_ATT_K:KV_COLS]], axis=1).astype(BF16)
    w_kv = w_in[:, :KV_COLS].astype(BF16)

    x2d = x.reshape(b * s_len, D_MODEL)
    px = _inproj(x2d, norm_w, lat(scale), lat(shift), w_lat, IN_COLS, s_len, 1024, 1536)
    pc = _inproj(ctx.reshape(b * c_len, D_MODEL), norm_w, cx(scale), cx(shift), w_kv, KV_COLS,
                 b * c_len, 1024, 1792)
    px3 = px.reshape(b, s_len, IN_COLS)
    pc3 = pc.reshape(b, c_len, KV_COLS)

    cos, sin_signed = _rope_tables(s_len)
    k_all, v_all = _kv_prep(px3, pc3, k_norm_w, cos, sin_signed)
    y_att = _attention(px3, k_all, v_all, q_norm_w, cos, sin_signed, 256)

    decay3 = ret_log2_decay.astype(F32).T.reshape(RET_HEADS, 2, 1)
    y_ret = _retention(px3, pc3, decay3)

    x_new = _merge(x2d, y_ret.reshape(b * s_len, RET_V_W), y_att.reshape(b * s_len, ATT_Q_W), px,
                   lat(gate), w_o_ret.astype(BF16), w_o_att.astype(BF16), w_out.astype(BF16),
                   s_len, 512)
    return x_new.reshape(b, s_len, D_MODEL)


def kernel(x, c, ctx, c_ctx, norm_w, w_ada, b_ada, w_in, ret_log2_decay, q_norm_w, k_norm_w,
           w_o_ret, w_o_att, w_out):
    depth = norm_w.shape[0]
    assert depth == 1, "context update between layers is not implemented"
    return _layer(x, ctx, c, c_ctx, norm_w[0], w_ada[0], b_ada[0], w_in[0], ret_log2_decay[0],
                  q_norm_w[0], k_norm_w[0], w_o_ret[0], w_o_att[0], w_out[0])
```

```python
import jax
import jax.numpy as jnp
from jax import lax
from jax.experimental import pallas as pl
from jax.experimental.pallas import tpu as pltpu

D_MODEL = 1024
GRID_W = 64
RET_HEADS = 4
RET_DK = 256
RET_DV = 512
RET_QK_W = RET_HEADS * RET_DK
RET_V_W = RET_HEADS * RET_DV
ATT_HEADS = 8
ATT_KV_HEADS = 2
ATT_HEAD_DIM = 128
ATT_Q_W = ATT_HEADS * ATT_HEAD_DIM
ATT_KV_W = ATT_KV_HEADS * ATT_HEAD_DIM
ATT_REP = ATT_HEADS // ATT_KV_HEADS
ROPE_THETA = 10000.0
NORM_EPS = 1e-6
KV_COLS = RET_QK_W + RET_V_W + 2 * ATT_KV_W
LOG2E = 1.4426950408889634

OFF_RET_K = 0
OFF_RET_V = OFF_RET_K + RET_QK_W
OFF_RET_Q = OFF_RET_V + RET_V_W
OFF_RET_G = OFF_RET_Q + RET_QK_W
OFF_ATT_Q = OFF_RET_G + RET_V_W
OFF_ATT_G = OFF_ATT_Q + ATT_Q_W
OFF_MG = OFF_ATT_G + ATT_Q_W
OFF_ATT_K = OFF_MG + 2 * D_MODEL
OFF_ATT_V = OFF_ATT_K + ATT_KV_W
IN_COLS = OFF_ATT_V + ATT_KV_W
CTX_OFF_ATT_K = RET_QK_W + RET_V_W
CTX_OFF_ATT_V = CTX_OFF_ATT_K + ATT_KV_W

RET_CHUNK = 256
BF16_SUBLANES = 16
VT_ROWS = ATT_HEAD_DIM + BF16_SUBLANES
VMEM_LIMIT = 56 * 1024 * 1024

BF16 = jnp.bfloat16
F32 = jnp.float32


def _dot(a, b):
    return jnp.dot(a, b, preferred_element_type=F32)


def _dot_nt(a, b):
    return lax.dot_general(a, b, (((1,), (1,)), ((), ())), preferred_element_type=F32)


def _dot_tn(a, b):
    return lax.dot_general(a, b, (((0,), (0,)), ((), ())), preferred_element_type=F32)


def _silu(x):
    return x * jax.nn.sigmoid(x)


def _adaln_kernel(c_ref, w_ref, b_ref, o_ref):
    o_ref[...] = _dot(_silu(c_ref[...]), w_ref[...]) + b_ref[...]


def _adaln(cc, w_ada, b_ada):
    rows = cc.shape[0]
    tn = D_MODEL
    return pl.pallas_call(
        _adaln_kernel,
        grid=(3 * D_MODEL // tn,),
        in_specs=[pl.BlockSpec((rows, D_MODEL), lambda j: (0, 0)),
                  pl.BlockSpec((D_MODEL, tn), lambda j: (0, j)),
                  pl.BlockSpec((1, tn), lambda j: (0, j))],
        out_specs=pl.BlockSpec((rows, tn), lambda j: (0, j)),
        out_shape=jax.ShapeDtypeStruct((rows, 3 * D_MODEL), F32),
        compiler_params=pltpu.CompilerParams(vmem_limit_bytes=VMEM_LIMIT),
        name="adaln",
    )(cc, w_ada, b_ada.reshape(1, -1))


def _inproj_kernel(x_ref, nw_ref, sc_ref, sh_ref, w_ref, o_ref, h_ref):
    @pl.when(pl.program_id(1) == 0)
    def _():
        x = x_ref[...]
        ms = jnp.mean(x * x, axis=-1, keepdims=True)
        y = x * lax.rsqrt(ms + NORM_EPS) * nw_ref[...]
        h_ref[...] = (y * (1.0 + sc_ref[0]) + sh_ref[0]).astype(BF16)

    o_ref[...] = _dot(h_ref[...], w_ref[...]).astype(BF16)


def _inproj(x2d, norm_w, scale, shift, w_bf, n_cols, rows_per_mod, tm, tn):
    rows = x2d.shape[0]
    mod_map = lambda i, j: ((i * tm) // rows_per_mod, 0, 0)
    return pl.pallas_call(
        _inproj_kernel,
        grid=(rows // tm, n_cols // tn),
        in_specs=[pl.BlockSpec((tm, D_MODEL), lambda i, j: (i, 0)),
                  pl.BlockSpec((1, D_MODEL), lambda i, j: (0, 0)),
                  pl.BlockSpec((1, 1, D_MODEL), mod_map),
                  pl.BlockSpec((1, 1, D_MODEL), mod_map),
                  pl.BlockSpec((D_MODEL, tn), lambda i, j: (0, j))],
        out_specs=pl.BlockSpec((tm, tn), lambda i, j: (i, j)),
        out_shape=jax.ShapeDtypeStruct((rows, n_cols), BF16),
        scratch_shapes=[pltpu.VMEM((tm, D_MODEL), BF16)],
        compiler_params=pltpu.CompilerParams(
            dimension_semantics=("parallel", "arbitrary"), vmem_limit_bytes=VMEM_LIMIT),
        name="inproj",
    )(x2d, norm_w.reshape(1, -1), scale, shift, w_bf)


def _head_rms(t, w):
    ones = jnp.ones((ATT_HEAD_DIM, ATT_HEAD_DIM), BF16)
    ms = _dot((t * t).astype(BF16), ones) * (1.0 / ATT_HEAD_DIM)
    return t * lax.rsqrt(ms + NORM_EPS) * w


def _rope(t, cos, sin_signed):
    return t * cos + pltpu.roll(t, ATT_HEAD_DIM // 2, 1) * sin_signed


def _kv_prep_kernel(kl_ref, vl_ref, kc_ref, vc_ref, kw_ref, cos_ref, sin_ref, ko_ref, vt_ref):
    s_len = kl_ref.shape[1]
    l_k = ko_ref.shape[1]
    kw = kw_ref[...]
    cos = cos_ref[...]
    sin = sin_ref[...]
    for g in range(ATT_KV_HEADS):
        cols = slice(g * ATT_HEAD_DIM, (g + 1) * ATT_HEAD_DIM)
        kl = _head_rms(kl_ref[0, :, cols].astype(F32), kw)
        ko_ref[0, :s_len, cols] = _rope(kl, cos, sin).astype(BF16)
        kc = _head_rms(kc_ref[0, :, cols].astype(F32), kw)
        ko_ref[0, s_len:, cols] = kc.astype(BF16)
        vt_ref[0, g, :ATT_HEAD_DIM, :s_len] = vl_ref[0, :, cols].astype(F32).T.astype(BF16)
        vt_ref[0, g, :ATT_HEAD_DIM, s_len:] = vc_ref[0, :, cols].astype(F32).T.astype(BF16)
        vt_ref[0, g, ATT_HEAD_DIM:, :] = jnp.ones((VT_ROWS - ATT_HEAD_DIM, l_k), BF16)


def _kv_prep(px3, pc3, k_norm_w, cos, sin_signed):
    b, s_len, _ = px3.shape
    c_len = pc3.shape[1]
    l_k = s_len + c_len
    w = ATT_KV_W
    return pl.pallas_call(
        _kv_prep_kernel,
        grid=(b,),
        in_specs=[pl.BlockSpec((1, s_len, w), lambda i: (i, 0, OFF_ATT_K // w)),
                  pl.BlockSpec((1, s_len, w), lambda i: (i, 0, OFF_ATT_V // w)),
                  pl.BlockSpec((1, c_len, w), lambda i: (i, 0, CTX_OFF_ATT_K // w)),
                  pl.BlockSpec((1, c_len, w), lambda i: (i, 0, CTX_OFF_ATT_V // w)),
                  pl.BlockSpec((1, ATT_HEAD_DIM), lambda i: (0, 0)),
                  pl.BlockSpec((s_len, ATT_HEAD_DIM), lambda i: (0, 0)),
                  pl.BlockSpec((s_len, ATT_HEAD_DIM), lambda i: (0, 0))],
        out_specs=[pl.BlockSpec((1, l_k, w), lambda i: (i, 0, 0)),
                   pl.BlockSpec((1, ATT_KV_HEADS, VT_ROWS, l_k), lambda i: (i, 0, 0, 0))],
        out_shape=[jax.ShapeDtypeStruct((b, l_k, w), BF16),
                   jax.ShapeDtypeStruct((b, ATT_KV_HEADS, VT_ROWS, l_k), BF16)],
        compiler_params=pltpu.CompilerParams(
            dimension_semantics=("parallel",), vmem_limit_bytes=VMEM_LIMIT),
        name="kv_prep",
    )(px3, px3, pc3, pc3, k_norm_w.reshape(1, -1), cos, sin_signed)


ATT_UNIT = 512


def _attention_kernel(q_ref, k_ref, vt_ref, g_ref, qw_ref, cos_ref, sin_ref, o_ref, qs_ref):
    tq = q_ref.shape[1]
    qw = qw_ref[...] * (ATT_HEAD_DIM ** -0.5 * LOG2E)
    cos = cos_ref[...]
    sin = sin_ref[...]
    for r in range(ATT_REP):
        cols = slice(r * ATT_HEAD_DIM, (r + 1) * ATT_HEAD_DIM)
        q = _rope(_head_rms(q_ref[0, :, cols].astype(F32), qw), cos, sin)
        qs_ref[r * tq:(r + 1) * tq, :] = q.astype(BF16)
    k = k_ref[0]
    vt = vt_ref[0, 0]
    n_units = ATT_REP * tq // ATT_UNIT

    def logits(u):
        s_t = _dot_nt(k, qs_ref[u * ATT_UNIT:(u + 1) * ATT_UNIT, :])
        return s_t, jnp.max(s_t, axis=0, keepdims=True)

    def finish(u, s_t, m):
        o_t = _dot(vt, jnp.exp2(s_t - m).astype(BF16))
        o_n = o_t[:ATT_HEAD_DIM] * (1.0 / o_t[ATT_HEAD_DIM:ATT_HEAD_DIM + 1])
        piece = min(tq, ATT_UNIT)
        for c0 in range(0, ATT_UNIT, piece):
            r, row0 = divmod(u * ATT_UNIT + c0, tq)
            cols = slice(r * ATT_HEAD_DIM, (r + 1) * ATT_HEAD_DIM)
            rows = slice(row0, row0 + piece)
            o = o_n[:, c0:c0 + piece].T
            o_ref[0, rows, cols] = (o * _silu(g_ref[0, rows, cols].astype(F32))).astype(BF16)

    pending = logits(0)
    for u in range(1, n_units):
        nxt = logits(u)
        finish(u - 1, *pending)
        pending = nxt
    finish(n_units - 1, *pending)


def _attention(px3, k_all, vt_all, q_norm_w, cos, sin_signed, tq):
    b, s_len, _ = px3.shape
    l_k = k_all.shape[1]
    gw = ATT_REP * ATT_HEAD_DIM
    return pl.pallas_call(
        _attention_kernel,
        grid=(b, ATT_KV_HEADS, s_len // tq),
        in_specs=[pl.BlockSpec((1, tq, gw), lambda i, g, t: (i, t, OFF_ATT_Q // gw + g)),
                  pl.BlockSpec((1, l_k, ATT_HEAD_DIM), lambda i, g, t: (i, 0, g)),
                  pl.BlockSpec((1, 1, VT_ROWS, l_k), lambda i, g, t: (i, g, 0, 0)),
                  pl.BlockSpec((1, tq, gw), lambda i, g, t: (i, t, OFF_ATT_G // gw + g)),
                  pl.BlockSpec((1, ATT_HEAD_DIM), lambda i, g, t: (0, 0)),
                  pl.BlockSpec((tq, ATT_HEAD_DIM), lambda i, g, t: (t, 0)),
                  pl.BlockSpec((tq, ATT_HEAD_DIM), lambda i, g, t: (t, 0))],
        out_specs=pl.BlockSpec((1, tq, gw), lambda i, g, t: (i, t, g)),
        out_shape=jax.ShapeDtypeStruct((b, s_len, ATT_Q_W), BF16),
        scratch_shapes=[pltpu.VMEM((ATT_REP * tq, ATT_HEAD_DIM), BF16)],
        compiler_params=pltpu.CompilerParams(
            dimension_semantics=("parallel", "parallel", "arbitrary"),
            vmem_limit_bytes=VMEM_LIMIT),
        name="attention",
    )(px3, k_all, vt_all, px3, q_norm_w.reshape(1, -1), cos, sin_signed)


def _retention_kernel(d_ref, q_ref, k_ref, v_ref, g_ref, kc_ref, vc_ref, o_ref, acc_ref):
    s_len = q_ref.shape[1]
    c_len = kc_ref.shape[1]
    ch = RET_CHUNK
    n_chunks = s_len // ch

    d = d_ref[0]
    lg = jnp.log1p(-jnp.exp2(d))
    lg_f = lg[0:1, :]
    lg_b = lg[1:2, :]

    idx = lax.broadcasted_iota(jnp.int32, (ch, 1), 0).astype(F32)
    qd_f = (jnp.exp(lg_f * (idx + 1.0)) * (RET_DK ** -0.5)).astype(BF16)
    kd_f = jnp.exp(lg_f * (ch - 1.0 - idx)).astype(BF16)
    qd_b = (jnp.exp(lg_b * (ch - idx)) * (RET_DK ** -0.5)).astype(BF16)
    kd_b = jnp.exp(lg_b * idx).astype(BF16)
    cd_f = jnp.exp(lg_f * ch)
    cd_b = jnp.exp(lg_b * ch)

    row = lax.broadcasted_iota(jnp.int32, (ch, ch), 0)
    col = lax.broadcasted_iota(jnp.int32, (ch, ch), 1)
    rel = (row - col).astype(F32)
    mask = jnp.where(rel > 0, jnp.exp(lg_f * rel),
                     jnp.where(rel < 0, jnp.exp(-lg_b * rel), 2.0)) * (RET_DK ** -0.5)

    pos = lax.broadcasted_iota(jnp.int32, (c_len, 1), 0).astype(F32)
    kc = kc_ref[0].astype(F32)
    vc = vc_ref[0]
    s_f = _dot_tn((kc * jnp.exp(lg_f * (c_len - 1.0 - pos))).astype(BF16), vc)
    s_b = _dot_tn((kc * jnp.exp(lg_b * pos)).astype(BF16), vc)

    for n in range(n_chunks):
        rows = slice(n * ch, (n + 1) * ch)
        q = q_ref[0, rows, :]
        k = k_ref[0, rows, :]
        v = v_ref[0, rows, :]
        scores = (_dot_nt(q, k) * mask).astype(BF16)
        acc_ref[rows, :] = _dot(scores, v) + _dot(q * qd_f, s_f.astype(BF16))
        s_f = s_f * cd_f + _dot_tn(k * kd_f, v)

    for n in reversed(range(n_chunks)):
        rows = slice(n * ch, (n + 1) * ch)
        q = q_ref[0, rows, :]
        k = k_ref[0, rows, :]
        v = v_ref[0, rows, :]
        o = acc_ref[rows, :] + _dot(q * qd_b, s_b.astype(BF16))
        ms = jnp.mean(o * o, axis=-1, keepdims=True)
        o = (o * lax.rsqrt(ms + NORM_EPS)).astype(BF16)
        o_ref[0, rows, :] = o * _silu(g_ref[0, rows, :])
        s_b = s_b * cd_b + _dot_tn(k * kd_b, v)


def _retention(px3, pc3, decay3):
    b, s_len, _ = px3.shape
    c_len = pc3.shape[1]
    return pl.pallas_call(
        _retention_kernel,
        grid=(b, RET_HEADS),
        in_specs=[pl.BlockSpec((1, 2, 1), lambda i, h: (h, 0, 0)),
                  pl.BlockSpec((1, s_len, RET_DK), lambda i, h: (i, 0, OFF_RET_Q // RET_DK + h)),
                  pl.BlockSpec((1, s_len, RET_DK), lambda i, h: (i, 0, OFF_RET_K // RET_DK + h)),
                  pl.BlockSpec((1, s_len, RET_DV), lambda i, h: (i, 0, OFF_RET_V // RET_DV + h)),
                  pl.BlockSpec((1, s_len, RET_DV), lambda i, h: (i, 0, OFF_RET_G // RET_DV + h)),
                  pl.BlockSpec((1, c_len, RET_DK), lambda i, h: (i, 0, h)),
                  pl.BlockSpec((1, c_len, RET_DV), lambda i, h: (i, 0, RET_QK_W // RET_DV + h))],
        out_specs=pl.BlockSpec((1, s_len, RET_DV), lambda i, h: (i, 0, h)),
        out_shape=jax.ShapeDtypeStruct((b, s_len, RET_V_W), BF16),
        scratch_shapes=[pltpu.VMEM((s_len, RET_DV), F32)],
        compiler_params=pltpu.CompilerParams(
            dimension_semantics=("parallel", "arbitrary"), vmem_limit_bytes=VMEM_LIMIT),
        name="retention",
    )(decay3, px3, px3, px3, px3, pc3, pc3)


def _merge_kernel(x_ref, yr_ref, ya_ref, mg_ref, gate_ref, wr_ref, wa_ref, wo_ref, o_ref):
    a = _dot(yr_ref[...], wr_ref[...])
    b = _dot(ya_ref[...], wa_ref[...])
    mg = mg_ref[...].astype(F32)
    y = jax.nn.sigmoid(mg[:, :D_MODEL]) * a + jax.nn.sigmoid(mg[:, D_MODEL:]) * b
    out = _dot(y.astype(BF16), wo_ref[...])
    o_ref[...] = x_ref[...] + gate_ref[0] * out


def _merge(x2d, y_ret, y_att, px, gate, w_o_ret, w_o_att, w_out, rows_per_mod, tm):
    rows = x2d.shape[0]
    const = lambda i: (0, 0)
    return pl.pallas_call(
        _merge_kernel,
        grid=(rows // tm,),
        in_specs=[pl.BlockSpec((tm, D_MODEL), lambda i: (i, 0)),
                  pl.BlockSpec((tm, RET_V_W), lambda i: (i, 0)),
                  pl.BlockSpec((tm, ATT_Q_W), lambda i: (i, 0)),
                  pl.BlockSpec((tm, 2 * D_MODEL), lambda i: (i, OFF_MG // (2 * D_MODEL))),
                  pl.BlockSpec((1, 1, D_MODEL), lambda i: ((i * tm) // rows_per_mod, 0, 0)),
                  pl.BlockSpec((RET_V_W, D_MODEL), const),
                  pl.BlockSpec((ATT_Q_W, D_MODEL), const),
                  pl.BlockSpec((D_MODEL, D_MODEL), const)],
        out_specs=pl.BlockSpec((tm, D_MODEL), lambda i: (i, 0)),
        out_shape=jax.ShapeDtypeStruct((rows, D_MODEL), F32),
        compiler_params=pltpu.CompilerParams(
            dimension_semantics=("parallel",), vmem_limit_bytes=VMEM_LIMIT),
        name="merge",
    )(x2d, y_ret, y_att, px, gate, w_o_ret, w_o_att, w_out)


def _rope_tables(s_len):
    rows = s_len // GRID_W
    row = jnp.repeat(jnp.arange(rows, dtype=F32), GRID_W)
    col = jnp.tile(jnp.arange(GRID_W, dtype=F32), rows)
    half = ATT_HEAD_DIM // 2
    freqs = ROPE_THETA ** (-jnp.arange(0, half, 2, dtype=F32) / half)
    ang = jnp.concatenate([row[:, None] * freqs, col[:, None] * freqs], axis=-1)
    cos, sin = jnp.cos(ang), jnp.sin(ang)
    return jnp.concatenate([cos, cos], axis=-1), jnp.concatenate([-sin, sin], axis=-1)


def _split_pairs(t):
    lead = t.shape[:-1]
    t = t.reshape(*lead, -1, ATT_HEAD_DIM // 2, 2)
    return jnp.swapaxes(t, -1, -2).reshape(*lead, -1)


def _layer(x, ctx, c, c_ctx, norm_w, w_ada, b_ada, w_in, ret_log2_decay,
           q_norm_w, k_norm_w, w_o_ret, w_o_att, w_out):
    b, s_len, _ = x.shape
    c_len = ctx.shape[1]

    mod = _adaln(jnp.concatenate([c, c_ctx[None]], axis=0), w_ada, b_ada)
    shift, scale, gate = (mod[:, i * D_MODEL:(i + 1) * D_MODEL] for i in range(3))
    lat = lambda t: t[:b].reshape(b, 1, D_MODEL)
    cx = lambda t: t[b:].reshape(1, 1, D_MODEL)

    w_bf = w_in.astype(BF16)
    ref_att_q = KV_COLS + RET_QK_W + RET_V_W
    w_ret_kv = w_bf[:, :CTX_OFF_ATT_K]
    w_att_k = _split_pairs(w_bf[:, CTX_OFF_ATT_K:CTX_OFF_ATT_V])
    w_att_v = w_bf[:, CTX_OFF_ATT_V:KV_COLS]
    w_att_q = _split_pairs(w_bf[:, ref_att_q:ref_att_q + ATT_Q_W])
    w_lat = jnp.concatenate([w_ret_kv, w_bf[:, KV_COLS:ref_att_q], w_att_q,
                             w_bf[:, ref_att_q + ATT_Q_W:], w_att_k, w_att_v], axis=1)
    w_kv = jnp.concatenate([w_ret_kv, w_att_k, w_att_v], axis=1)
    q_norm_w = _split_pairs(q_norm_w)
    k_norm_w = _split_pairs(k_norm_w)

    x2d = x.reshape(b * s_len, D_MODEL)
    px = _inproj(x2d, norm_w, lat(scale), lat(shift), w_lat, IN_COLS, s_len, 1024, 1536)
    pc = _inproj(ctx.reshape(b * c_len, D_MODEL), norm_w, cx(scale), cx(shift), w_kv, KV_COLS,
                 b * c_len, 1024, 1792)
    px3 = px.reshape(b, s_len, IN_COLS)
    pc3 = pc.reshape(b, c_len, KV_COLS)

    cos, sin_signed = _rope_tables(s_len)
    k_all, vt_all = _kv_prep(px3, pc3, k_norm_w, cos, sin_signed)
    y_att = _attention(px3, k_all, vt_all, q_norm_w, cos, sin_signed, 512)

    decay3 = ret_log2_decay.astype(F32).T.reshape(RET_HEADS, 2, 1)
    y_ret = _retention(px3, pc3, decay3)

    x_new = _merge(x2d, y_ret.reshape(b * s_len, RET_V_W), y_att.reshape(b * s_len, ATT_Q_W), px,
                   lat(gate), w_o_ret.astype(BF16), w_o_att.astype(BF16), w_out.astype(BF16),
                   s_len, 512)
    return x_new.reshape(b, s_len, D_MODEL)


def kernel(x, c, ctx, c_ctx, norm_w, w_ada, b_ada, w_in, ret_log2_decay, q_norm_w, k_norm_w,
           w_o_ret, w_o_att, w_out):
    depth = norm_w.shape[0]
    assert depth == 1, "context update between layers is not implemented"
    return _layer(x, ctx, c, c_ctx, norm_w[0], w_ada[0], b_ada[0], w_in[0], ret_log2_decay[0],
                  q_norm_w[0], k_norm_w[0], w_o_ret[0], w_o_att[0], w_out[0])
```

```python
import jax
import jax.numpy as jnp
from jax import lax
from jax.experimental import pallas as pl
from jax.experimental.pallas import tpu as pltpu

D_MODEL = 1024
GRID_W = 64
RET_HEADS = 4
RET_DK = 256
RET_DV = 512
RET_QK_W = RET_HEADS * RET_DK
RET_V_W = RET_HEADS * RET_DV
ATT_HEADS = 8
ATT_KV_HEADS = 2
ATT_HEAD_DIM = 128
ATT_Q_W = ATT_HEADS * ATT_HEAD_DIM
ATT_KV_W = ATT_KV_HEADS * ATT_HEAD_DIM
ATT_REP = ATT_HEADS // ATT_KV_HEADS
ROPE_THETA = 10000.0
NORM_EPS = 1e-6
KV_COLS = RET_QK_W + RET_V_W + 2 * ATT_KV_W
LOG2E = 1.4426950408889634

OFF_RET_K = 0
OFF_RET_V = OFF_RET_K + RET_QK_W
OFF_RET_Q = OFF_RET_V + RET_V_W
OFF_RET_G = OFF_RET_Q + RET_QK_W
OFF_ATT_Q = OFF_RET_G + RET_V_W
OFF_ATT_G = OFF_ATT_Q + ATT_Q_W
OFF_MG = OFF_ATT_G + ATT_Q_W
OFF_ATT_K = OFF_MG + 2 * D_MODEL
OFF_ATT_V = OFF_ATT_K + ATT_KV_W
IN_COLS = OFF_ATT_V + ATT_KV_W
CTX_OFF_ATT_K = RET_QK_W + RET_V_W
CTX_OFF_ATT_V = CTX_OFF_ATT_K + ATT_KV_W

RET_CHUNK = 256
BF16_SUBLANES = 16
VT_ROWS = ATT_HEAD_DIM + BF16_SUBLANES
VMEM_LIMIT = 56 * 1024 * 1024

BF16 = jnp.bfloat16
F32 = jnp.float32


def _dot(a, b):
    return jnp.dot(a, b, preferred_element_type=F32)


def _dot_nt(a, b):
    return lax.dot_general(a, b, (((1,), (1,)), ((), ())), preferred_element_type=F32)


def _dot_tn(a, b):
    return lax.dot_general(a, b, (((0,), (0,)), ((), ())), preferred_element_type=F32)


def _silu(x):
    return x * jax.nn.sigmoid(x)


def _adaln_kernel(c_ref, w_ref, b_ref, o_ref):
    o_ref[...] = _dot(_silu(c_ref[...]), w_ref[...]) + b_ref[...]


def _adaln(cc, w_ada, b_ada):
    rows = cc.shape[0]
    tn = D_MODEL
    return pl.pallas_call(
        _adaln_kernel,
        grid=(3 * D_MODEL // tn,),
        in_specs=[pl.BlockSpec((rows, D_MODEL), lambda j: (0, 0)),
                  pl.BlockSpec((D_MODEL, tn), lambda j: (0, j)),
                  pl.BlockSpec((1, tn), lambda j: (0, j))],
        out_specs=pl.BlockSpec((rows, tn), lambda j: (0, j)),
        out_shape=jax.ShapeDtypeStruct((rows, 3 * D_MODEL), F32),
        compiler_params=pltpu.CompilerParams(vmem_limit_bytes=VMEM_LIMIT),
        name="adaln",
    )(cc, w_ada, b_ada.reshape(1, -1))


def _inproj_kernel(x_ref, nw_ref, sc_ref, sh_ref, w_ref, o_ref, h_ref):
    @pl.when(pl.program_id(1) == 0)
    def _():
        x = x_ref[...]
        ms = jnp.mean(x * x, axis=-1, keepdims=True)
        y = x * lax.rsqrt(ms + NORM_EPS) * nw_ref[...]
        h_ref[...] = (y * (1.0 + sc_ref[0]) + sh_ref[0]).astype(BF16)

    o_ref[...] = _dot(h_ref[...], w_ref[...]).astype(BF16)


def _inproj(x2d, norm_w, scale, shift, w_bf, n_cols, rows_per_mod, tm, tn):
    rows = x2d.shape[0]
    mod_map = lambda i, j: ((i * tm) // rows_per_mod, 0, 0)
    return pl.pallas_call(
        _inproj_kernel,
        grid=(rows // tm, n_cols // tn),
        in_specs=[pl.BlockSpec((tm, D_MODEL), lambda i, j: (i, 0)),
                  pl.BlockSpec((1, D_MODEL), lambda i, j: (0, 0)),
                  pl.BlockSpec((1, 1, D_MODEL), mod_map),
                  pl.BlockSpec((1, 1, D_MODEL), mod_map),
                  pl.BlockSpec((D_MODEL, tn), lambda i, j: (0, j))],
        out_specs=pl.BlockSpec((tm, tn), lambda i, j: (i, j)),
        out_shape=jax.ShapeDtypeStruct((rows, n_cols), BF16),
        scratch_shapes=[pltpu.VMEM((tm, D_MODEL), BF16)],
        compiler_params=pltpu.CompilerParams(
            dimension_semantics=("parallel", "arbitrary"), vmem_limit_bytes=VMEM_LIMIT),
        name="inproj",
    )(x2d, norm_w.reshape(1, -1), scale, shift, w_bf)


def _head_rms(t, w):
    ones = jnp.ones((ATT_HEAD_DIM, ATT_HEAD_DIM), BF16)
    ms = _dot((t * t).astype(BF16), ones) * (1.0 / ATT_HEAD_DIM)
    return t * lax.rsqrt(ms + NORM_EPS) * w


def _rope(t, cos, sin_signed):
    return t * cos + pltpu.roll(t, ATT_HEAD_DIM // 2, 1) * sin_signed


def _kv_prep_kernel(kl_ref, vl_ref, kc_ref, vc_ref, kw_ref, cos_ref, sin_ref, ko_ref, vt_ref):
    s_len = kl_ref.shape[1]
    l_k = ko_ref.shape[1]
    kw = kw_ref[...]
    cos = cos_ref[...]
    sin = sin_ref[...]
    for g in range(ATT_KV_HEADS):
        cols = slice(g * ATT_HEAD_DIM, (g + 1) * ATT_HEAD_DIM)
        kl = _head_rms(kl_ref[0, :, cols].astype(F32), kw)
        ko_ref[0, :s_len, cols] = _rope(kl, cos, sin).astype(BF16)
        kc = _head_rms(kc_ref[0, :, cols].astype(F32), kw)
        ko_ref[0, s_len:, cols] = kc.astype(BF16)
        vt_ref[0, g, :ATT_HEAD_DIM, :s_len] = vl_ref[0, :, cols].astype(F32).T.astype(BF16)
        vt_ref[0, g, :ATT_HEAD_DIM, s_len:] = vc_ref[0, :, cols].astype(F32).T.astype(BF16)
        vt_ref[0, g, ATT_HEAD_DIM:, :] = jnp.ones((VT_ROWS - ATT_HEAD_DIM, l_k), BF16)


def _kv_prep(px3, pc3, k_norm_w, cos, sin_signed):
    b, s_len, _ = px3.shape
    c_len = pc3.shape[1]
    l_k = s_len + c_len
    w = ATT_KV_W
    return pl.pallas_call(
        _kv_prep_kernel,
        grid=(b,),
        in_specs=[pl.BlockSpec((1, s_len, w), lambda i: (i, 0, OFF_ATT_K // w)),
                  pl.BlockSpec((1, s_len, w), lambda i: (i, 0, OFF_ATT_V // w)),
                  pl.BlockSpec((1, c_len, w), lambda i: (i, 0, CTX_OFF_ATT_K // w)),
                  pl.BlockSpec((1, c_len, w), lambda i: (i, 0, CTX_OFF_ATT_V // w)),
                  pl.BlockSpec((1, ATT_HEAD_DIM), lambda i: (0, 0)),
                  pl.BlockSpec((s_len, ATT_HEAD_DIM), lambda i: (0, 0)),
                  pl.BlockSpec((s_len, ATT_HEAD_DIM), lambda i: (0, 0))],
        out_specs=[pl.BlockSpec((1, l_k, w), lambda i: (i, 0, 0)),
                   pl.BlockSpec((1, ATT_KV_HEADS, VT_ROWS, l_k), lambda i: (i, 0, 0, 0))],
        out_shape=[jax.ShapeDtypeStruct((b, l_k, w), BF16),
                   jax.ShapeDtypeStruct((b, ATT_KV_HEADS, VT_ROWS, l_k), BF16)],
        compiler_params=pltpu.CompilerParams(
            dimension_semantics=("parallel",), vmem_limit_bytes=VMEM_LIMIT),
        name="kv_prep",
    )(px3, px3, pc3, pc3, k_norm_w.reshape(1, -1), cos, sin_signed)


ATT_UNIT = 512


def _attention_kernel(q_ref, k_ref, vt_ref, g_ref, qw_ref, cos_ref, sin_ref, o_ref, qs_ref):
    tq = q_ref.shape[1]
    qw = qw_ref[...] * (ATT_HEAD_DIM ** -0.5 * LOG2E)
    cos = cos_ref[...]
    sin = sin_ref[...]
    for r in range(ATT_REP):
        cols = slice(r * ATT_HEAD_DIM, (r + 1) * ATT_HEAD_DIM)
        q = _rope(_head_rms(q_ref[0, :, cols].astype(F32), qw), cos, sin)
        qs_ref[r * tq:(r + 1) * tq, :] = q.astype(BF16)
    k = k_ref[0]
    vt = vt_ref[0, 0]
    n_units = ATT_REP * tq // ATT_UNIT

    def logits(u):
        s_t = _dot_nt(k, qs_ref[u * ATT_UNIT:(u + 1) * ATT_UNIT, :])
        return s_t, jnp.max(s_t, axis=0, keepdims=True)

    def finish(u, s_t, m):
        o_t = _dot(vt, jnp.exp2(s_t - m).astype(BF16))
        o_n = o_t[:ATT_HEAD_DIM] * (1.0 / o_t[ATT_HEAD_DIM:ATT_HEAD_DIM + 1])
        piece = min(tq, ATT_UNIT)
        for c0 in range(0, ATT_UNIT, piece):
            r, row0 = divmod(u * ATT_UNIT + c0, tq)
            cols = slice(r * ATT_HEAD_DIM, (r + 1) * ATT_HEAD_DIM)
            rows = slice(row0, row0 + piece)
            o = o_n[:, c0:c0 + piece].T
            o_ref[0, rows, cols] = (o * _silu(g_ref[0, rows, cols].astype(F32))).astype(BF16)

    pending = logits(0)
    for u in range(1, n_units):
        nxt = logits(u)
        finish(u - 1, *pending)
        pending = nxt
    finish(n_units - 1, *pending)


def _attention(px3, k_all, vt_all, q_norm_w, cos, sin_signed, tq):
    b, s_len, _ = px3.shape
    l_k = k_all.shape[1]
    gw = ATT_REP * ATT_HEAD_DIM
    return pl.pallas_call(
        _attention_kernel,
        grid=(b, ATT_KV_HEADS, s_len // tq),
        in_specs=[pl.BlockSpec((1, tq, gw), lambda i, g, t: (i, t, OFF_ATT_Q // gw + g)),
                  pl.BlockSpec((1, l_k, ATT_HEAD_DIM), lambda i, g, t: (i, 0, g)),
                  pl.BlockSpec((1, 1, VT_ROWS, l_k), lambda i, g, t: (i, g, 0, 0)),
                  pl.BlockSpec((1, tq, gw), lambda i, g, t: (i, t, OFF_ATT_G // gw + g)),
                  pl.BlockSpec((1, ATT_HEAD_DIM), lambda i, g, t: (0, 0)),
                  pl.BlockSpec((tq, ATT_HEAD_DIM), lambda i, g, t: (t, 0)),
                  pl.BlockSpec((tq, ATT_HEAD_DIM), lambda i, g, t: (t, 0))],
        out_specs=pl.BlockSpec((1, tq, gw), lambda i, g, t: (i, t, g)),
        out_shape=jax.ShapeDtypeStruct((b, s_len, ATT_Q_W), BF16),
        scratch_shapes=[pltpu.VMEM((ATT_REP * tq, ATT_HEAD_DIM), BF16)],
        compiler_params=pltpu.CompilerParams(
            dimension_semantics=("parallel", "parallel", "arbitrary"),
            vmem_limit_bytes=VMEM_LIMIT),
        name="attention",
    )(px3, k_all, vt_all, px3, q_norm_w.reshape(1, -1), cos, sin_signed)


def _retention_kernel(d_ref, q_ref, k_ref, v_ref, g_ref, kc_ref, vc_ref, o_ref, acc_ref):
    s_len = q_ref.shape[1]
    c_len = kc_ref.shape[1]
    ch = RET_CHUNK
    n_chunks = s_len // ch

    d = d_ref[0]
    lg = jnp.log1p(-jnp.exp2(d))
    lg_f = lg[0:1, :]
    lg_b = lg[1:2, :]

    idx = lax.broadcasted_iota(jnp.int32, (ch, 1), 0).astype(F32)
    qd_f = (jnp.exp(lg_f * (idx + 1.0)) * (RET_DK ** -0.5)).astype(BF16)
    kd_f = jnp.exp(lg_f * (ch - 1.0 - idx)).astype(BF16)
    qd_b = (jnp.exp(lg_b * (ch - idx)) * (RET_DK ** -0.5)).astype(BF16)
    kd_b = jnp.exp(lg_b * idx).astype(BF16)
    cd_f = jnp.exp(lg_f * ch)
    cd_b = jnp.exp(lg_b * ch)

    row = lax.broadcasted_iota(jnp.int32, (ch, ch), 0)
    col = lax.broadcasted_iota(jnp.int32, (ch, ch), 1)
    rel = (row - col).astype(F32)
    mask = jnp.where(rel > 0, jnp.exp(lg_f * rel),
                     jnp.where(rel < 0, jnp.exp(-lg_b * rel), 2.0)) * (RET_DK ** -0.5)

    pos = lax.broadcasted_iota(jnp.int32, (c_len, 1), 0).astype(F32)
    kc = kc_ref[0].astype(F32)
    vc = vc_ref[0]
    s_f = _dot_tn((kc * jnp.exp(lg_f * (c_len - 1.0 - pos))).astype(BF16), vc)
    s_b = _dot_tn((kc * jnp.exp(lg_b * pos)).astype(BF16), vc)

    def chunk(n):
        return slice(n * ch, (n + 1) * ch)

    def masked_scores(n):
        return (_dot_nt(q_ref[0, chunk(n), :], k_ref[0, chunk(n), :]) * mask).astype(BF16)

    def finish(n, o):
        ms = jnp.mean(o * o, axis=-1, keepdims=True)
        o = (o * lax.rsqrt(ms + NORM_EPS)).astype(BF16)
        o_ref[0, chunk(n), :] = o * _silu(g_ref[0, chunk(n), :])

    scores = masked_scores(0)
    for i in range(n_chunks):
        nf, nb = i, n_chunks - 1 - i
        kv_f = _dot_tn(k_ref[0, chunk(nf), :] * kd_f, v_ref[0, chunk(nf), :])
        kv_b = _dot_tn(k_ref[0, chunk(nb), :] * kd_b, v_ref[0, chunk(nb), :])
        next_scores = masked_scores(nf + 1) if nf + 1 < n_chunks else None
        o_f = _dot(q_ref[0, chunk(nf), :] * qd_f, s_f.astype(BF16)) + _dot(scores, v_ref[0, chunk(nf), :])
        o_b = _dot(q_ref[0, chunk(nb), :] * qd_b, s_b.astype(BF16))
        if nf < nb:
            acc_ref[chunk(nf), :] = o_f
            acc_ref[chunk(nb), :] = o_b
        elif nf == nb:
            finish(nf, o_f + o_b)
        else:
            finish(nf, o_f + acc_ref[chunk(nf), :])
            finish(nb, o_b + acc_ref[chunk(nb), :])
        s_f = s_f * cd_f + kv_f
        s_b = s_b * cd_b + kv_b
        scores = next_scores


def _retention(px3, pc3, decay3):
    b, s_len, _ = px3.shape
    c_len = pc3.shape[1]
    return pl.pallas_call(
        _retention_kernel,
        grid=(b, RET_HEADS),
        in_specs=[pl.BlockSpec((1, 2, 1), lambda i, h: (h, 0, 0)),
                  pl.BlockSpec((1, s_len, RET_DK), lambda i, h: (i, 0, OFF_RET_Q // RET_DK + h)),
                  pl.BlockSpec((1, s_len, RET_DK), lambda i, h: (i, 0, OFF_RET_K // RET_DK + h)),
                  pl.BlockSpec((1, s_len, RET_DV), lambda i, h: (i, 0, OFF_RET_V // RET_DV + h)),
                  pl.BlockSpec((1, s_len, RET_DV), lambda i, h: (i, 0, OFF_RET_G // RET_DV + h)),
                  pl.BlockSpec((1, c_len, RET_DK), lambda i, h: (i, 0, h)),
                  pl.BlockSpec((1, c_len, RET_DV), lambda i, h: (i, 0, RET_QK_W // RET_DV + h))],
        out_specs=pl.BlockSpec((1, s_len, RET_DV), lambda i, h: (i, 0, h)),
        out_shape=jax.ShapeDtypeStruct((b, s_len, RET_V_W), BF16),
        scratch_shapes=[pltpu.VMEM((s_len, RET_DV), F32)],
        compiler_params=pltpu.CompilerParams(
            dimension_semantics=("parallel", "arbitrary"), vmem_limit_bytes=VMEM_LIMIT),
        name="retention",
    )(decay3, px3, px3, px3, px3, pc3, pc3)


def _merge_kernel(x_ref, yr_ref, ya_ref, mg_ref, gate_ref, wr_ref, wa_ref, wo_ref, o_ref):
    a = _dot(yr_ref[...], wr_ref[...])
    b = _dot(ya_ref[...], wa_ref[...])
    mg = mg_ref[...].astype(F32)
    y = jax.nn.sigmoid(mg[:, :D_MODEL]) * a + jax.nn.sigmoid(mg[:, D_MODEL:]) * b
    out = _dot(y.astype(BF16), wo_ref[...])
    o_ref[...] = x_ref[...] + gate_ref[0] * out


def _merge(x2d, y_ret, y_att, px, gate, w_o_ret, w_o_att, w_out, rows_per_mod, tm):
    rows = x2d.shape[0]
    const = lambda i: (0, 0)
    return pl.pallas_call(
        _merge_kernel,
        grid=(rows // tm,),
        in_specs=[pl.BlockSpec((tm, D_MODEL), lambda i: (i, 0)),
                  pl.BlockSpec((tm, RET_V_W), lambda i: (i, 0)),
                  pl.BlockSpec((tm, ATT_Q_W), lambda i: (i, 0)),
                  pl.BlockSpec((tm, 2 * D_MODEL), lambda i: (i, OFF_MG // (2 * D_MODEL))),
                  pl.BlockSpec((1, 1, D_MODEL), lambda i: ((i * tm) // rows_per_mod, 0, 0)),
                  pl.BlockSpec((RET_V_W, D_MODEL), const),
                  pl.BlockSpec((ATT_Q_W, D_MODEL), const),
                  pl.BlockSpec((D_MODEL, D_MODEL), const)],
        out_specs=pl.BlockSpec((tm, D_MODEL), lambda i: (i, 0)),
        out_shape=jax.ShapeDtypeStruct((rows, D_MODEL), F32),
        compiler_params=pltpu.CompilerParams(
            dimension_semantics=("parallel",), vmem_limit_bytes=VMEM_LIMIT),
        name="merge",
    )(x2d, y_ret, y_att, px, gate, w_o_ret, w_o_att, w_out)


def _rope_tables(s_len):
    rows = s_len // GRID_W
    row = jnp.repeat(jnp.arange(rows, dtype=F32), GRID_W)
    col = jnp.tile(jnp.arange(GRID_W, dtype=F32), rows)
    half = ATT_HEAD_DIM // 2
    freqs = ROPE_THETA ** (-jnp.arange(0, half, 2, dtype=F32) / half)
    ang = jnp.concatenate([row[:, None] * freqs, col[:, None] * freqs], axis=-1)
    cos, sin = jnp.cos(ang), jnp.sin(ang)
    return jnp.concatenate([cos, cos], axis=-1), jnp.concatenate([-sin, sin], axis=-1)


def _split_pairs(t):
    lead = t.shape[:-1]
    t = t.reshape(*lead, -1, ATT_HEAD_DIM // 2, 2)
    return jnp.swapaxes(t, -1, -2).reshape(*lead, -1)


def _layer(x, ctx, c, c_ctx, norm_w, w_ada, b_ada, w_in, ret_log2_decay,
           q_norm_w, k_norm_w, w_o_ret, w_o_att, w_out):
    b, s_len, _ = x.shape
    c_len = ctx.shape[1]

    mod = _adaln(jnp.concatenate([c, c_ctx[None]], axis=0), w_ada, b_ada)
    shift, scale, gate = (mod[:, i * D_MODEL:(i + 1) * D_MODEL] for i in range(3))
    lat = lambda t: t[:b].reshape(b, 1, D_MODEL)
    cx = lambda t: t[b:].reshape(1, 1, D_MODEL)

    w_bf = w_in.astype(BF16)
    ref_att_q = KV_COLS + RET_QK_W + RET_V_W
    w_ret_kv = w_bf[:, :CTX_OFF_ATT_K]
    w_att_k = _split_pairs(w_bf[:, CTX_OFF_ATT_K:CTX_OFF_ATT_V])
    w_att_v = w_bf[:, CTX_OFF_ATT_V:KV_COLS]
    w_att_q = _split_pairs(w_bf[:, ref_att_q:ref_att_q + ATT_Q_W])
    w_lat = jnp.concatenate([w_ret_kv, w_bf[:, KV_COLS:ref_att_q], w_att_q,
                             w_bf[:, ref_att_q + ATT_Q_W:], w_att_k, w_att_v], axis=1)
    w_kv = jnp.concatenate([w_ret_kv, w_att_k, w_att_v], axis=1)
    q_norm_w = _split_pairs(q_norm_w)
    k_norm_w = _split_pairs(k_norm_w)

    x2d = x.reshape(b * s_len, D_MODEL)
    px = _inproj(x2d, norm_w, lat(scale), lat(shift), w_lat, IN_COLS, s_len, 2048, 1536)
    pc = _inproj(ctx.reshape(b * c_len, D_MODEL), norm_w, cx(scale), cx(shift), w_kv, KV_COLS,
                 b * c_len, 1024, 1792)
    px3 = px.reshape(b, s_len, IN_COLS)
    pc3 = pc.reshape(b, c_len, KV_COLS)

    cos, sin_signed = _rope_tables(s_len)
    k_all, vt_all = _kv_prep(px3, pc3, k_norm_w, cos, sin_signed)
    y_att = _attention(px3, k_all, vt_all, q_norm_w, cos, sin_signed, 512)

    decay3 = ret_log2_decay.astype(F32).T.reshape(RET_HEADS, 2, 1)
    y_ret = _retention(px3, pc3, decay3)

    x_new = _merge(x2d, y_ret.reshape(b * s_len, RET_V_W), y_att.reshape(b * s_len, ATT_Q_W), px,
                   lat(gate), w_o_ret.astype(BF16), w_o_att.astype(BF16), w_out.astype(BF16),
                   s_len, 512)
    return x_new.reshape(b, s_len, D_MODEL)


def kernel(x, c, ctx, c_ctx, norm_w, w_ada, b_ada, w_in, ret_log2_decay, q_norm_w, k_norm_w,
           w_o_ret, w_o_att, w_out):
    depth = norm_w.shape[0]
    assert depth == 1, "context update between layers is not implemented"
    return _layer(x, ctx, c, c_ctx, norm_w[0], w_ada[0], b_ada[0], w_in[0], ret_log2_decay[0],
                  q_norm_w[0], k_norm_w[0], w_o_ret[0], w_o_att[0], w_out[0])
```

```python
import jax
import jax.numpy as jnp
from jax import lax
from jax.experimental import pallas as pl
from jax.experimental.pallas import tpu as pltpu

D_MODEL = 1024
GRID_W = 64
RET_HEADS = 4
RET_DK = 256
RET_DV = 512
RET_QK_W = RET_HEADS * RET_DK
RET_V_W = RET_HEADS * RET_DV
ATT_HEADS = 8
ATT_KV_HEADS = 2
ATT_HEAD_DIM = 128
ATT_Q_W = ATT_HEADS * ATT_HEAD_DIM
ATT_KV_W = ATT_KV_HEADS * ATT_HEAD_DIM
ATT_REP = ATT_HEADS // ATT_KV_HEADS
ROPE_THETA = 10000.0
NORM_EPS = 1e-6
KV_COLS = RET_QK_W + RET_V_W + 2 * ATT_KV_W
LOG2E = 1.4426950408889634

OFF_RET_K = 0
OFF_RET_V = OFF_RET_K + RET_QK_W
OFF_ATT_K = OFF_RET_V + RET_V_W
OFF_ATT_V = OFF_ATT_K + ATT_KV_W
OFF_RET_Q = OFF_ATT_V + ATT_KV_W
OFF_RET_G = OFF_RET_Q + RET_QK_W
OFF_ATT_Q = OFF_RET_G + RET_V_W
OFF_ATT_G = OFF_ATT_Q + ATT_Q_W
OFF_MG_RET = OFF_ATT_G + ATT_Q_W
OFF_MG_ATT = OFF_MG_RET + D_MODEL
IN_COLS = OFF_MG_ATT + D_MODEL
assert OFF_RET_Q == KV_COLS

RET_CHUNK = 256
BF16_SUBLANES = 16
VT_ROWS = ATT_HEAD_DIM + BF16_SUBLANES
VMEM_LIMIT = 56 * 1024 * 1024

BF16 = jnp.bfloat16
F32 = jnp.float32


def _dot(a, b):
    return jnp.dot(a, b, preferred_element_type=F32)


def _dot_nt(a, b):
    return lax.dot_general(a, b, (((1,), (1,)), ((), ())), preferred_element_type=F32)


def _dot_tn(a, b):
    return lax.dot_general(a, b, (((0,), (0,)), ((), ())), preferred_element_type=F32)


def _silu(x):
    return x * jax.nn.sigmoid(x)


def _adaln_kernel(c_ref, w_ref, b_ref, o_ref):
    o_ref[...] = _dot(_silu(c_ref[...]), w_ref[...]) + b_ref[...]


def _adaln(cc, w_ada, b_ada):
    rows = cc.shape[0]
    tn = D_MODEL
    return pl.pallas_call(
        _adaln_kernel,
        grid=(3 * D_MODEL // tn,),
        in_specs=[pl.BlockSpec((rows, D_MODEL), lambda j: (0, 0)),
                  pl.BlockSpec((D_MODEL, tn), lambda j: (0, j)),
                  pl.BlockSpec((1, tn), lambda j: (0, j))],
        out_specs=pl.BlockSpec((rows, tn), lambda j: (0, j)),
        out_shape=jax.ShapeDtypeStruct((rows, 3 * D_MODEL), F32),
        compiler_params=pltpu.CompilerParams(vmem_limit_bytes=VMEM_LIMIT),
        name="adaln",
    )(cc, w_ada, b_ada.reshape(1, -1))


def _inproj_kernel(x_ref, nw_ref, sc_ref, sh_ref, w_ref, o_ref, h_ref):
    @pl.when(pl.program_id(1) == 0)
    def _():
        x = x_ref[...]
        ms = jnp.mean(x * x, axis=-1, keepdims=True)
        y = x * lax.rsqrt(ms + NORM_EPS) * nw_ref[...]
        h = (y * (1.0 + sc_ref[0]) + sh_ref[0]).astype(BF16)
        h_ref[...] = h
        o_ref[...] = _dot(h, w_ref[...]).astype(BF16)

    @pl.when(pl.program_id(1) != 0)
    def _():
        o_ref[...] = _dot(h_ref[...], w_ref[...]).astype(BF16)


def _inproj(x2d, norm_w, scale, shift, w_bf, n_cols, rows_per_mod, tm, tn):
    rows = x2d.shape[0]
    mod_map = lambda i, j: ((i * tm) // rows_per_mod, 0, 0)
    return pl.pallas_call(
        _inproj_kernel,
        grid=(rows // tm, n_cols // tn),
        in_specs=[pl.BlockSpec((tm, D_MODEL), lambda i, j: (i, 0)),
                  pl.BlockSpec((1, D_MODEL), lambda i, j: (0, 0)),
                  pl.BlockSpec((1, 1, D_MODEL), mod_map),
                  pl.BlockSpec((1, 1, D_MODEL), mod_map),
                  pl.BlockSpec((D_MODEL, tn), lambda i, j: (0, j))],
        out_specs=pl.BlockSpec((tm, tn), lambda i, j: (i, j)),
        out_shape=jax.ShapeDtypeStruct((rows, n_cols), BF16),
        scratch_shapes=[pltpu.VMEM((tm, D_MODEL), BF16)],
        compiler_params=pltpu.CompilerParams(
            dimension_semantics=("parallel", "arbitrary"), vmem_limit_bytes=VMEM_LIMIT),
        name="inproj",
    )(x2d, norm_w.reshape(1, -1), scale, shift, w_bf)


def _head_rms(t, w):
    ones = jnp.ones((ATT_HEAD_DIM, ATT_HEAD_DIM), BF16)
    ms = _dot((t * t).astype(BF16), ones) * (1.0 / ATT_HEAD_DIM)
    return t * lax.rsqrt(ms + NORM_EPS) * w


def _rope(t, cos, sin_signed):
    return t * cos + pltpu.roll(t, ATT_HEAD_DIM // 2, 1) * sin_signed


def _kv_prep_kernel(kl_ref, vl_ref, kc_ref, vc_ref, kw_ref, cos_ref, sin_ref, ko_ref, vt_ref):
    s_len = kl_ref.shape[1]
    l_k = ko_ref.shape[1]
    kw = kw_ref[...]
    cos = cos_ref[...]
    sin = sin_ref[...]
    for g in range(ATT_KV_HEADS):
        cols = slice(g * ATT_HEAD_DIM, (g + 1) * ATT_HEAD_DIM)
        kl = _head_rms(kl_ref[0, :, cols].astype(F32), kw)
        ko_ref[0, :s_len, cols] = _rope(kl, cos, sin).astype(BF16)
        kc = _head_rms(kc_ref[0, :, cols].astype(F32), kw)
        ko_ref[0, s_len:, cols] = kc.astype(BF16)
        vt_ref[0, g, :ATT_HEAD_DIM, :s_len] = vl_ref[0, :, cols].astype(F32).T.astype(BF16)
        vt_ref[0, g, :ATT_HEAD_DIM, s_len:] = vc_ref[0, :, cols].astype(F32).T.astype(BF16)
        vt_ref[0, g, ATT_HEAD_DIM:, :] = jnp.ones((VT_ROWS - ATT_HEAD_DIM, l_k), BF16)


def _kv_prep(px3, pc3, k_norm_w, cos, sin_signed):
    b, s_len, _ = px3.shape
    c_len = pc3.shape[1]
    l_k = s_len + c_len
    w = ATT_KV_W
    return pl.pallas_call(
        _kv_prep_kernel,
        grid=(b,),
        in_specs=[pl.BlockSpec((1, s_len, w), lambda i: (i, 0, OFF_ATT_K // w)),
                  pl.BlockSpec((1, s_len, w), lambda i: (i, 0, OFF_ATT_V // w)),
                  pl.BlockSpec((1, c_len, w), lambda i: (i, 0, OFF_ATT_K // w)),
                  pl.BlockSpec((1, c_len, w), lambda i: (i, 0, OFF_ATT_V // w)),
                  pl.BlockSpec((1, ATT_HEAD_DIM), lambda i: (0, 0)),
                  pl.BlockSpec((s_len, ATT_HEAD_DIM), lambda i: (0, 0)),
                  pl.BlockSpec((s_len, ATT_HEAD_DIM), lambda i: (0, 0))],
        out_specs=[pl.BlockSpec((1, l_k, w), lambda i: (i, 0, 0)),
                   pl.BlockSpec((1, ATT_KV_HEADS, VT_ROWS, l_k), lambda i: (i, 0, 0, 0))],
        out_shape=[jax.ShapeDtypeStruct((b, l_k, w), BF16),
                   jax.ShapeDtypeStruct((b, ATT_KV_HEADS, VT_ROWS, l_k), BF16)],
        compiler_params=pltpu.CompilerParams(
            dimension_semantics=("parallel",), vmem_limit_bytes=VMEM_LIMIT),
        name="kv_prep",
    )(px3, px3, pc3, pc3, k_norm_w.reshape(1, -1), cos, sin_signed)


ATT_UNIT = 512


def _attention_kernel(q_ref, k_ref, vt_ref, g_ref, qw_ref, cos_ref, sin_ref, o_ref, qs_ref):
    qw = qw_ref[...] * (ATT_HEAD_DIM ** -0.5 * LOG2E)
    cos = cos_ref[...]
    sin = sin_ref[...]
    k = k_ref[0]
    vt = vt_ref[0, 0]

    for r in range(ATT_REP):
        cols = slice(r * ATT_HEAD_DIM, (r + 1) * ATT_HEAD_DIM)
        q = _rope(_head_rms(q_ref[0, :, cols].astype(F32), qw), cos, sin)
        qs_ref[r] = q.astype(BF16)

    def logits(r):
        s_t = _dot_nt(k, qs_ref[r])
        return s_t, jnp.max(s_t, axis=0, keepdims=True)

    def finish(r, s_t, m):
        cols = slice(r * ATT_HEAD_DIM, (r + 1) * ATT_HEAD_DIM)
        o_t = _dot(vt, jnp.exp2(s_t - m).astype(BF16))
        o = (o_t[:ATT_HEAD_DIM] * (1.0 / o_t[ATT_HEAD_DIM:ATT_HEAD_DIM + 1])).T
        o_ref[0, :, cols] = (o * _silu(g_ref[0, :, cols].astype(F32))).astype(BF16)

    pending = logits(0)
    for r in range(1, ATT_REP):
        nxt = logits(r)
        finish(r - 1, *pending)
        pending = nxt
    finish(ATT_REP - 1, *pending)


def _attention(px3, k_all, vt_all, q_norm_w, cos, sin_signed):
    b, s_len, _ = px3.shape
    l_k = k_all.shape[1]
    gw = ATT_REP * ATT_HEAD_DIM
    tq = ATT_UNIT
    return pl.pallas_call(
        _attention_kernel,
        grid=(b, ATT_KV_HEADS, s_len // tq),
        in_specs=[pl.BlockSpec((1, tq, gw), lambda i, g, t: (i, t, OFF_ATT_Q // gw + g)),
                  pl.BlockSpec((1, l_k, ATT_HEAD_DIM), lambda i, g, t: (i, 0, g)),
                  pl.BlockSpec((1, 1, VT_ROWS, l_k), lambda i, g, t: (i, g, 0, 0)),
                  pl.BlockSpec((1, tq, gw), lambda i, g, t: (i, t, OFF_ATT_G // gw + g)),
                  pl.BlockSpec((1, ATT_HEAD_DIM), lambda i, g, t: (0, 0)),
                  pl.BlockSpec((tq, ATT_HEAD_DIM), lambda i, g, t: (t, 0)),
                  pl.BlockSpec((tq, ATT_HEAD_DIM), lambda i, g, t: (t, 0))],
        out_specs=pl.BlockSpec((1, tq, gw), lambda i, g, t: (i, t, g)),
        out_shape=jax.ShapeDtypeStruct((b, s_len, ATT_Q_W), BF16),
        scratch_shapes=[pltpu.VMEM((ATT_REP, tq, ATT_HEAD_DIM), BF16)],
        compiler_params=pltpu.CompilerParams(
            dimension_semantics=("parallel", "parallel", "arbitrary"),
            vmem_limit_bytes=VMEM_LIMIT),
        name="attention",
    )(px3, k_all, vt_all, px3, q_norm_w.reshape(1, -1), cos, sin_signed)


def _retention_kernel(d_ref, q_ref, k_ref, v_ref, g_ref, kc_ref, vc_ref, o_ref, acc_ref):
    s_len = q_ref.shape[1]
    c_len = kc_ref.shape[1]
    ch = RET_CHUNK
    n_chunks = s_len // ch

    d = d_ref[0]
    lg = jnp.log1p(-jnp.exp2(d))
    lg_f = lg[0:1, :]
    lg_b = lg[1:2, :]

    idx = lax.broadcasted_iota(jnp.int32, (ch, 1), 0).astype(F32)
    qd_f = (jnp.exp(lg_f * (idx + 1.0)) * (RET_DK ** -0.5)).astype(BF16)
    kd_f = jnp.exp(lg_f * (ch - 1.0 - idx)).astype(BF16)
    qd_b = (jnp.exp(lg_b * (ch - idx)) * (RET_DK ** -0.5)).astype(BF16)
    kd_b = jnp.exp(lg_b * idx).astype(BF16)
    cd_f = jnp.exp(lg_f * ch)
    cd_b = jnp.exp(lg_b * ch)

    row = lax.broadcasted_iota(jnp.int32, (ch, ch), 0)
    col = lax.broadcasted_iota(jnp.int32, (ch, ch), 1)
    rel = (row - col).astype(F32)
    mask = jnp.where(rel > 0, jnp.exp(lg_f * rel),
                     jnp.where(rel < 0, jnp.exp(-lg_b * rel), 2.0)) * (RET_DK ** -0.5)

    pos = lax.broadcasted_iota(jnp.int32, (c_len, 1), 0).astype(F32)
    kc = kc_ref[0].astype(F32)
    vc = vc_ref[0]
    s_f = _dot_tn((kc * jnp.exp(lg_f * (c_len - 1.0 - pos))).astype(BF16), vc)
    s_b = _dot_tn((kc * jnp.exp(lg_b * pos)).astype(BF16), vc)

    def chunk(n):
        return slice(n * ch, (n + 1) * ch)

    def masked_scores(n):
        return (_dot_nt(q_ref[0, chunk(n), :], k_ref[0, chunk(n), :]) * mask).astype(BF16)

    def finish(n, o):
        ms = jnp.mean(o * o, axis=-1, keepdims=True)
        o = (o * lax.rsqrt(ms + NORM_EPS)).astype(BF16)
        o_ref[0, chunk(n), :] = o * _silu(g_ref[0, chunk(n), :])

    scores = masked_scores(0)
    for i in range(n_chunks):
        nf, nb = i, n_chunks - 1 - i
        kv_f = _dot_tn(k_ref[0, chunk(nf), :] * kd_f, v_ref[0, chunk(nf), :])
        kv_b = _dot_tn(k_ref[0, chunk(nb), :] * kd_b, v_ref[0, chunk(nb), :])
        next_scores = masked_scores(nf + 1) if nf + 1 < n_chunks else None
        o_f = _dot(q_ref[0, chunk(nf), :] * qd_f, s_f.astype(BF16)) + _dot(scores, v_ref[0, chunk(nf), :])
        o_b = _dot(q_ref[0, chunk(nb), :] * qd_b, s_b.astype(BF16))
        if nf < nb:
            acc_ref[chunk(nf), :] = o_f
            acc_ref[chunk(nb), :] = o_b
        elif nf == nb:
            finish(nf, o_f + o_b)
        else:
            finish(nf, o_f + acc_ref[chunk(nf), :])
            finish(nb, o_b + acc_ref[chunk(nb), :])
        s_f = s_f * cd_f + kv_f
        s_b = s_b * cd_b + kv_b
        scores = next_scores


def _retention(px3, pc3, decay3):
    b, s_len, _ = px3.shape
    c_len = pc3.shape[1]
    return pl.pallas_call(
        _retention_kernel,
        grid=(b, RET_HEADS),
        in_specs=[pl.BlockSpec((1, 2, 1), lambda i, h: (h, 0, 0)),
                  pl.BlockSpec((1, s_len, RET_DK), lambda i, h: (i, 0, OFF_RET_Q // RET_DK + h)),
                  pl.BlockSpec((1, s_len, RET_DK), lambda i, h: (i, 0, OFF_RET_K // RET_DK + h)),
                  pl.BlockSpec((1, s_len, RET_DV), lambda i, h: (i, 0, OFF_RET_V // RET_DV + h)),
                  pl.BlockSpec((1, s_len, RET_DV), lambda i, h: (i, 0, OFF_RET_G // RET_DV + h)),
                  pl.BlockSpec((1, c_len, RET_DK), lambda i, h: (i, 0, h)),
                  pl.BlockSpec((1, c_len, RET_DV), lambda i, h: (i, 0, RET_QK_W // RET_DV + h))],
        out_specs=pl.BlockSpec((1, s_len, RET_DV), lambda i, h: (i, 0, h)),
        out_shape=jax.ShapeDtypeStruct((b, s_len, RET_V_W), BF16),
        scratch_shapes=[pltpu.VMEM((s_len, RET_DV), F32)],
        compiler_params=pltpu.CompilerParams(
            dimension_semantics=("parallel", "arbitrary"), vmem_limit_bytes=VMEM_LIMIT),
        name="retention",
    )(decay3, px3, px3, px3, px3, pc3, pc3)


def _merge_kernel(x_ref, yr_ref, ya_ref, mr0_ref, mr1_ref, ma0_ref, ma1_ref, gate_ref,
                  wr_ref, wa_ref, wo_ref, o_ref):
    a = _dot(yr_ref[...], wr_ref[...])
    b = _dot(ya_ref[...], wa_ref[...])
    mg_ret = jnp.concatenate([mr0_ref[...], mr1_ref[...]], axis=-1).astype(F32)
    mg_att = jnp.concatenate([ma0_ref[...], ma1_ref[...]], axis=-1).astype(F32)
    y = jax.nn.sigmoid(mg_ret) * a + jax.nn.sigmoid(mg_att) * b
    out = _dot(y.astype(BF16), wo_ref[...])
    o_ref[...] = x_ref[...] + gate_ref[0] * out


def _merge(x2d, y_ret, y_att, px, gate, w_o_ret, w_o_att, w_out, rows_per_mod, tm):
    rows = x2d.shape[0]
    half = D_MODEL // 2
    weight = lambda shape: pl.BlockSpec(shape, lambda i: (0, 0), pipeline_mode=pl.Buffered(1))
    mg = lambda off: pl.BlockSpec((tm, half), lambda i: (i, off // half))
    return pl.pallas_call(
        _merge_kernel,
        grid=(rows // tm,),
        in_specs=[pl.BlockSpec((tm, D_MODEL), lambda i: (i, 0)),
                  pl.BlockSpec((tm, RET_V_W), lambda i: (i, 0)),
                  pl.BlockSpec((tm, ATT_Q_W), lambda i: (i, 0)),
                  mg(OFF_MG_RET), mg(OFF_MG_RET + half), mg(OFF_MG_ATT), mg(OFF_MG_ATT + half),
                  pl.BlockSpec((1, 1, D_MODEL), lambda i: ((i * tm) // rows_per_mod, 0, 0)),
                  weight((RET_V_W, D_MODEL)), weight((ATT_Q_W, D_MODEL)), weight((D_MODEL, D_MODEL))],
        out_specs=pl.BlockSpec((tm, D_MODEL), lambda i: (i, 0)),
        out_shape=jax.ShapeDtypeStruct((rows, D_MODEL), F32),
        compiler_params=pltpu.CompilerParams(
            dimension_semantics=("parallel",), vmem_limit_bytes=VMEM_LIMIT),
        name="merge",
    )(x2d, y_ret, y_att, px, px, px, px, gate, w_o_ret, w_o_att, w_out)


def _rope_tables(s_len):
    rows = s_len // GRID_W
    row = jnp.repeat(jnp.arange(rows, dtype=F32), GRID_W)
    col = jnp.tile(jnp.arange(GRID_W, dtype=F32), rows)
    half = ATT_HEAD_DIM // 2
    freqs = ROPE_THETA ** (-jnp.arange(0, half, 2, dtype=F32) / half)
    ang = jnp.concatenate([row[:, None] * freqs, col[:, None] * freqs], axis=-1)
    cos, sin = jnp.cos(ang), jnp.sin(ang)
    return jnp.concatenate([cos, cos], axis=-1), jnp.concatenate([-sin, sin], axis=-1)


def _split_pairs(t):
    lead = t.shape[:-1]
    t = t.reshape(*lead, -1, ATT_HEAD_DIM // 2, 2)
    return jnp.swapaxes(t, -1, -2).reshape(*lead, -1)


def _layer(x, ctx, c, c_ctx, norm_w, w_ada, b_ada, w_in, ret_log2_decay,
           q_norm_w, k_norm_w, w_o_ret, w_o_att, w_out):
    b, s_len, _ = x.shape
    c_len = ctx.shape[1]

    mod = _adaln(jnp.concatenate([c, c_ctx[None]], axis=0), w_ada, b_ada)
    shift, scale, gate = (mod[:, i * D_MODEL:(i + 1) * D_MODEL] for i in range(3))
    lat = lambda t: t[:b].reshape(b, 1, D_MODEL)
    cx = lambda t: t[b:].reshape(1, 1, D_MODEL)

    w_bf = w_in.astype(BF16)
    for off, width in ((OFF_ATT_K, ATT_KV_W), (OFF_ATT_Q, ATT_Q_W)):
        w_bf = w_bf.at[:, off:off + width].set(_split_pairs(w_bf[:, off:off + width]))
    q_norm_w = _split_pairs(q_norm_w)
    k_norm_w = _split_pairs(k_norm_w)

    x2d = x.reshape(b * s_len, D_MODEL)
    px = _inproj(x2d, norm_w, lat(scale), lat(shift), w_bf, IN_COLS, s_len, 2048, 1536)
    pc = _inproj(ctx.reshape(b * c_len, D_MODEL), norm_w, cx(scale), cx(shift), w_bf, KV_COLS,
                 b * c_len, 1024, 1792)
    px3 = px.reshape(b, s_len, IN_COLS)
    pc3 = pc.reshape(b, c_len, KV_COLS)

    cos, sin_signed = _rope_tables(s_len)
    k_all, vt_all = _kv_prep(px3, pc3, k_norm_w, cos, sin_signed)
    y_att = _attention(px3, k_all, vt_all, q_norm_w, cos, sin_signed)

    decay3 = ret_log2_decay.astype(F32).T.reshape(RET_HEADS, 2, 1)
    y_ret = _retention(px3, pc3, decay3)

    x_new = _merge(x2d, y_ret.reshape(b * s_len, RET_V_W), y_att.reshape(b * s_len, ATT_Q_W), px,
                   lat(gate), w_o_ret.astype(BF16), w_o_att.astype(BF16), w_out.astype(BF16),
                   s_len, 1024)
    return x_new.reshape(b, s_len, D_MODEL)


def kernel(x, c, ctx, c_ctx, norm_w, w_ada, b_ada, w_in, ret_log2_decay, q_norm_w, k_norm_w,
           w_o_ret, w_o_att, w_out):
    depth = norm_w.shape[0]
    assert depth == 1, "context update between layers is not implemented"
    return _layer(x, ctx, c, c_ctx, norm_w[0], w_ada[0], b_ada[0], w_in[0], ret_log2_decay[0],
                  q_norm_w[0], k_norm_w[0], w_o_ret[0], w_o_att[0], w_out[0])
```

```python
import jax
import jax.numpy as jnp
from jax import lax
from jax.experimental import pallas as pl
from jax.experimental.pallas import tpu as pltpu

D_MODEL = 1024
GRID_W = 64
RET_HEADS = 4
RET_DK = 256
RET_DV = 512
RET_QK_W = RET_HEADS * RET_DK
RET_V_W = RET_HEADS * RET_DV
ATT_HEADS = 8
ATT_KV_HEADS = 2
ATT_HEAD_DIM = 128
ATT_Q_W = ATT_HEADS * ATT_HEAD_DIM
ATT_KV_W = ATT_KV_HEADS * ATT_HEAD_DIM
ATT_REP = ATT_HEADS // ATT_KV_HEADS
ROPE_THETA = 10000.0
NORM_EPS = 1e-6
KV_COLS = RET_QK_W + RET_V_W + 2 * ATT_KV_W
LOG2E = 1.4426950408889634

OFF_RET_K = 0
OFF_RET_V = OFF_RET_K + RET_QK_W
OFF_ATT_K = OFF_RET_V + RET_V_W
OFF_ATT_V = OFF_ATT_K + ATT_KV_W
OFF_RET_Q = OFF_ATT_V + ATT_KV_W
OFF_RET_G = OFF_RET_Q + RET_QK_W
OFF_ATT_Q = OFF_RET_G + RET_V_W
OFF_ATT_G = OFF_ATT_Q + ATT_Q_W
OFF_MG_RET = OFF_ATT_G + ATT_Q_W
OFF_MG_ATT = OFF_MG_RET + D_MODEL
IN_COLS = OFF_MG_ATT + D_MODEL
assert OFF_RET_Q == KV_COLS

RET_CHUNK = 256
BF16_SUBLANES = 16
VT_ROWS = ATT_HEAD_DIM + BF16_SUBLANES
VMEM_LIMIT = 56 * 1024 * 1024

BF16 = jnp.bfloat16
F32 = jnp.float32


def _dot(a, b):
    return jnp.dot(a, b, preferred_element_type=F32)


def _dot_nt(a, b):
    return lax.dot_general(a, b, (((1,), (1,)), ((), ())), preferred_element_type=F32)


def _dot_tn(a, b):
    return lax.dot_general(a, b, (((0,), (0,)), ((), ())), preferred_element_type=F32)


def _silu(x):
    return x * jax.nn.sigmoid(x)


def _adaln_kernel(c_ref, w_ref, b_ref, o_ref):
    o_ref[...] = _dot(_silu(c_ref[...]), w_ref[...]) + b_ref[...]


def _adaln(cc, w_ada, b_ada):
    rows = cc.shape[0]
    tn = D_MODEL
    return pl.pallas_call(
        _adaln_kernel,
        grid=(3 * D_MODEL // tn,),
        in_specs=[pl.BlockSpec((rows, D_MODEL), lambda j: (0, 0)),
                  pl.BlockSpec((D_MODEL, tn), lambda j: (0, j)),
                  pl.BlockSpec((1, tn), lambda j: (0, j))],
        out_specs=pl.BlockSpec((rows, tn), lambda j: (0, j)),
        out_shape=jax.ShapeDtypeStruct((rows, 3 * D_MODEL), F32),
        compiler_params=pltpu.CompilerParams(vmem_limit_bytes=VMEM_LIMIT),
        name="adaln",
    )(cc, w_ada, b_ada.reshape(1, -1))


def _inproj_kernel(x_ref, nw_ref, sc_ref, sh_ref, w_ref, o_ref, h_ref):
    @pl.when(pl.program_id(1) == 0)
    def _():
        x = x_ref[...]
        ms = jnp.mean(x * x, axis=-1, keepdims=True)
        y = x * lax.rsqrt(ms + NORM_EPS) * nw_ref[...]
        h = (y * (1.0 + sc_ref[0]) + sh_ref[0]).astype(BF16)
        h_ref[...] = h
        o_ref[...] = _dot(h, w_ref[...]).astype(BF16)

    @pl.when(pl.program_id(1) != 0)
    def _():
        o_ref[...] = _dot(h_ref[...], w_ref[...]).astype(BF16)


def _inproj(x2d, norm_w, scale, shift, w_bf, n_cols, rows_per_mod, tm, tn):
    rows = x2d.shape[0]
    mod_map = lambda i, j: ((i * tm) // rows_per_mod, 0, 0)
    return pl.pallas_call(
        _inproj_kernel,
        grid=(rows // tm, n_cols // tn),
        in_specs=[pl.BlockSpec((tm, D_MODEL), lambda i, j: (i, 0)),
                  pl.BlockSpec((1, D_MODEL), lambda i, j: (0, 0)),
                  pl.BlockSpec((1, 1, D_MODEL), mod_map),
                  pl.BlockSpec((1, 1, D_MODEL), mod_map),
                  pl.BlockSpec((D_MODEL, tn), lambda i, j: (0, j))],
        out_specs=pl.BlockSpec((tm, tn), lambda i, j: (i, j)),
        out_shape=jax.ShapeDtypeStruct((rows, n_cols), BF16),
        scratch_shapes=[pltpu.VMEM((tm, D_MODEL), BF16)],
        compiler_params=pltpu.CompilerParams(
            dimension_semantics=("parallel", "arbitrary"), vmem_limit_bytes=VMEM_LIMIT),
        name="inproj",
    )(x2d, norm_w.reshape(1, -1), scale, shift, w_bf)


def _head_rms(t, w):
    ones = jnp.ones((ATT_HEAD_DIM, ATT_HEAD_DIM), BF16)
    ms = _dot((t * t).astype(BF16), ones) * (1.0 / ATT_HEAD_DIM)
    return t * lax.rsqrt(ms + NORM_EPS) * w


def _rope(t, cos, sin_signed):
    return t * cos + pltpu.roll(t, ATT_HEAD_DIM // 2, 1) * sin_signed


def _kv_prep_kernel(kl_ref, vl_ref, kc_ref, vc_ref, kw_ref, cos_ref, sin_ref, ko_ref, vt_ref):
    s_len = kl_ref.shape[1]
    l_k = ko_ref.shape[1]
    kw = kw_ref[...]
    cos = cos_ref[...]
    sin = sin_ref[...]
    for g in range(ATT_KV_HEADS):
        cols = slice(g * ATT_HEAD_DIM, (g + 1) * ATT_HEAD_DIM)
        kl = _head_rms(kl_ref[0, :, cols].astype(F32), kw)
        ko_ref[0, :s_len, cols] = _rope(kl, cos, sin).astype(BF16)
        kc = _head_rms(kc_ref[0, :, cols].astype(F32), kw)
        ko_ref[0, s_len:, cols] = kc.astype(BF16)
        vt_ref[0, g, :ATT_HEAD_DIM, :s_len] = vl_ref[0, :, cols].astype(F32).T.astype(BF16)
        vt_ref[0, g, :ATT_HEAD_DIM, s_len:] = vc_ref[0, :, cols].astype(F32).T.astype(BF16)
        vt_ref[0, g, ATT_HEAD_DIM:, :] = jnp.ones((VT_ROWS - ATT_HEAD_DIM, l_k), BF16)


def _kv_prep(px3, pc3, k_norm_w, cos, sin_signed):
    b, s_len, _ = px3.shape
    c_len = pc3.shape[1]
    l_k = s_len + c_len
    w = ATT_KV_W
    return pl.pallas_call(
        _kv_prep_kernel,
        grid=(b,),
        in_specs=[pl.BlockSpec((1, s_len, w), lambda i: (i, 0, OFF_ATT_K // w)),
                  pl.BlockSpec((1, s_len, w), lambda i: (i, 0, OFF_ATT_V // w)),
                  pl.BlockSpec((1, c_len, w), lambda i: (i, 0, OFF_ATT_K // w)),
                  pl.BlockSpec((1, c_len, w), lambda i: (i, 0, OFF_ATT_V // w)),
                  pl.BlockSpec((1, ATT_HEAD_DIM), lambda i: (0, 0)),
                  pl.BlockSpec((s_len, ATT_HEAD_DIM), lambda i: (0, 0)),
                  pl.BlockSpec((s_len, ATT_HEAD_DIM), lambda i: (0, 0))],
        out_specs=[pl.BlockSpec((1, l_k, w), lambda i: (i, 0, 0)),
                   pl.BlockSpec((1, ATT_KV_HEADS, VT_ROWS, l_k), lambda i: (i, 0, 0, 0))],
        out_shape=[jax.ShapeDtypeStruct((b, l_k, w), BF16),
                   jax.ShapeDtypeStruct((b, ATT_KV_HEADS, VT_ROWS, l_k), BF16)],
        compiler_params=pltpu.CompilerParams(
            dimension_semantics=("parallel",), vmem_limit_bytes=VMEM_LIMIT),
        name="kv_prep",
    )(px3, px3, pc3, pc3, k_norm_w.reshape(1, -1), cos, sin_signed)


KEY_TILE = 256
ATT_UNIT = 512
ATT_TQ = 1024


def _attention_kernel(q_ref, k_ref, vt_ref, g_ref, qw_ref, cos_ref, sin_ref, o_ref, qs_ref):
    qw = qw_ref[...] * (ATT_HEAD_DIM ** -0.5 * LOG2E)
    cos = cos_ref[...]
    sin = sin_ref[...]
    k = k_ref[0]
    vt = vt_ref[0, 0]

    for r in range(ATT_REP):
        cols = slice(r * ATT_HEAD_DIM, (r + 1) * ATT_HEAD_DIM)
        q = _rope(_head_rms(q_ref[0, :, cols].astype(F32), qw), cos, sin)
        qs_ref[r] = q.astype(BF16)

    tq = q_ref.shape[1]
    units = [(r, c0) for r in range(ATT_REP) for c0 in range(0, tq, ATT_UNIT)]

    def logits(u):
        r, c0 = units[u]
        s_t = _dot_nt(k, qs_ref[r, c0:c0 + ATT_UNIT, :])
        return s_t, jnp.max(s_t, axis=0, keepdims=True)

    def finish(u, s_t, m, s_next):
        r, c0 = units[u]
        rows = slice(c0, c0 + ATT_UNIT)
        cols = slice(r * ATT_HEAD_DIM, (r + 1) * ATT_HEAD_DIM)
        if s_next is None:
            p_t = jnp.exp2(s_t - m).astype(BF16)
        else:
            tiles = []
            for j in range(0, s_t.shape[0], KEY_TILE):
                probe = pltpu.bitcast(s_next[j:j + 8, :], jnp.uint32)
                zero = ((probe >> 16) >> 16).astype(F32)[0:1, :]
                tiles.append(jnp.exp2(s_t[j:j + KEY_TILE, :] - (m + zero)).astype(BF16))
            p_t = jnp.concatenate(tiles, axis=0)
        o_t = _dot(vt, p_t)
        o = (o_t[:ATT_HEAD_DIM] * (1.0 / o_t[ATT_HEAD_DIM:ATT_HEAD_DIM + 1])).T
        o_ref[0, rows, cols] = (o * _silu(g_ref[0, rows, cols].astype(F32))).astype(BF16)

    pending = logits(0)
    for u in range(1, len(units)):
        nxt = logits(u)
        finish(u - 1, *pending, nxt[0])
        pending = nxt
    finish(len(units) - 1, *pending, None)


def _attention(px3, k_all, vt_all, q_norm_w, cos, sin_signed):
    b, s_len, _ = px3.shape
    l_k = k_all.shape[1]
    gw = ATT_REP * ATT_HEAD_DIM
    tq = ATT_TQ
    return pl.pallas_call(
        _attention_kernel,
        grid=(b, ATT_KV_HEADS, s_len // tq),
        in_specs=[pl.BlockSpec((1, tq, gw), lambda i, g, t: (i, t, OFF_ATT_Q // gw + g)),
                  pl.BlockSpec((1, l_k, ATT_HEAD_DIM), lambda i, g, t: (i, 0, g)),
                  pl.BlockSpec((1, 1, VT_ROWS, l_k), lambda i, g, t: (i, g, 0, 0)),
                  pl.BlockSpec((1, tq, gw), lambda i, g, t: (i, t, OFF_ATT_G // gw + g)),
                  pl.BlockSpec((1, ATT_HEAD_DIM), lambda i, g, t: (0, 0)),
                  pl.BlockSpec((tq, ATT_HEAD_DIM), lambda i, g, t: (t, 0)),
                  pl.BlockSpec((tq, ATT_HEAD_DIM), lambda i, g, t: (t, 0))],
        out_specs=pl.BlockSpec((1, tq, gw), lambda i, g, t: (i, t, g)),
        out_shape=jax.ShapeDtypeStruct((b, s_len, ATT_Q_W), BF16),
        scratch_shapes=[pltpu.VMEM((ATT_REP, tq, ATT_HEAD_DIM), BF16)],
        compiler_params=pltpu.CompilerParams(
            dimension_semantics=("parallel", "parallel", "arbitrary"),
            vmem_limit_bytes=VMEM_LIMIT),
        name="attention",
    )(px3, k_all, vt_all, px3, q_norm_w.reshape(1, -1), cos, sin_signed)


def _retention_kernel(d_ref, q_ref, k_ref, v_ref, g_ref, kc_ref, vc_ref, o_ref, sb_ref):
    s_len = q_ref.shape[1]
    c_len = kc_ref.shape[1]
    ch = RET_CHUNK
    n_chunks = s_len // ch

    d = d_ref[0]
    lg = jnp.log1p(-jnp.exp2(d))
    lg_f = lg[0:1, :]
    lg_b = lg[1:2, :]

    idx = lax.broadcasted_iota(jnp.int32, (ch, 1), 0).astype(F32)
    qd_f = (jnp.exp(lg_f * (idx + 1.0)) * (RET_DK ** -0.5)).astype(BF16)
    kd_f = jnp.exp(lg_f * (ch - 1.0 - idx)).astype(BF16)
    qd_b = (jnp.exp(lg_b * (ch - idx)) * (RET_DK ** -0.5)).astype(BF16)
    kd_b = jnp.exp(lg_b * idx).astype(BF16)
    cd_f = jnp.exp(lg_f * ch)
    cd_b = jnp.exp(lg_b * ch)

    row = lax.broadcasted_iota(jnp.int32, (ch, ch), 0)
    col = lax.broadcasted_iota(jnp.int32, (ch, ch), 1)
    rel = (row - col).astype(F32)
    mask = jnp.where(rel > 0, jnp.exp(lg_f * rel),
                     jnp.where(rel < 0, jnp.exp(-lg_b * rel), 2.0)) * (RET_DK ** -0.5)

    pos = lax.broadcasted_iota(jnp.int32, (c_len, 1), 0).astype(F32)
    kc = kc_ref[0].astype(F32)
    vc = vc_ref[0]
    s_f = _dot_tn((kc * jnp.exp(lg_f * (c_len - 1.0 - pos))).astype(BF16), vc)
    s_b = _dot_tn((kc * jnp.exp(lg_b * pos)).astype(BF16), vc)

    def chunk(n):
        return slice(n * ch, (n + 1) * ch)

    def masked_scores(n):
        return (_dot_nt(q_ref[0, chunk(n), :], k_ref[0, chunk(n), :]) * mask).astype(BF16)

    for n in reversed(range(n_chunks)):
        sb_ref[n] = s_b.astype(BF16)
        if n > 0:
            s_b = s_b * cd_b + _dot_tn(k_ref[0, chunk(n), :] * kd_b, v_ref[0, chunk(n), :])

    scores = masked_scores(0)
    for n in range(n_chunks):
        q = q_ref[0, chunk(n), :]
        v = v_ref[0, chunk(n), :]
        last = n + 1 == n_chunks
        next_scores = None if last else masked_scores(n + 1)
        kv_f = None if last else _dot_tn(k_ref[0, chunk(n), :] * kd_f, v)
        o = _dot(q * qd_b, sb_ref[n]) + _dot(q * qd_f, s_f.astype(BF16)) + _dot(scores, v)
        ms = jnp.mean(o * o, axis=-1, keepdims=True)
        o = (o * lax.rsqrt(ms + NORM_EPS)).astype(BF16)
        o_ref[0, chunk(n), :] = o * _silu(g_ref[0, chunk(n), :])
        if not last:
            s_f = s_f * cd_f + kv_f
            scores = next_scores


def _retention(px3, pc3, decay3):
    b, s_len, _ = px3.shape
    c_len = pc3.shape[1]
    return pl.pallas_call(
        _retention_kernel,
        grid=(b, RET_HEADS),
        in_specs=[pl.BlockSpec((1, 2, 1), lambda i, h: (h, 0, 0)),
                  pl.BlockSpec((1, s_len, RET_DK), lambda i, h: (i, 0, OFF_RET_Q // RET_DK + h)),
                  pl.BlockSpec((1, s_len, RET_DK), lambda i, h: (i, 0, OFF_RET_K // RET_DK + h)),
                  pl.BlockSpec((1, s_len, RET_DV), lambda i, h: (i, 0, OFF_RET_V // RET_DV + h)),
                  pl.BlockSpec((1, s_len, RET_DV), lambda i, h: (i, 0, OFF_RET_G // RET_DV + h)),
                  pl.BlockSpec((1, c_len, RET_DK), lambda i, h: (i, 0, h)),
                  pl.BlockSpec((1, c_len, RET_DV), lambda i, h: (i, 0, RET_QK_W // RET_DV + h))],
        out_specs=pl.BlockSpec((1, s_len, RET_DV), lambda i, h: (i, 0, h)),
        out_shape=jax.ShapeDtypeStruct((b, s_len, RET_V_W), BF16),
        scratch_shapes=[pltpu.VMEM((s_len // RET_CHUNK, RET_DK, RET_DV), BF16)],
        compiler_params=pltpu.CompilerParams(
            dimension_semantics=("parallel", "arbitrary"), vmem_limit_bytes=VMEM_LIMIT),
        name="retention",
    )(decay3, px3, px3, px3, px3, pc3, pc3)


def _merge_kernel(x_ref, yr_ref, ya_ref, mr0_ref, mr1_ref, ma0_ref, ma1_ref, gate_ref,
                  wr_ref, wa_ref, wo_ref, o_ref):
    a = _dot(yr_ref[...], wr_ref[...])
    b = _dot(ya_ref[...], wa_ref[...])
    mg_ret = jnp.concatenate([mr0_ref[...], mr1_ref[...]], axis=-1).astype(F32)
    mg_att = jnp.concatenate([ma0_ref[...], ma1_ref[...]], axis=-1).astype(F32)
    y = jax.nn.sigmoid(mg_ret) * a + jax.nn.sigmoid(mg_att) * b
    out = _dot(y.astype(BF16), wo_ref[...])
    o_ref[...] = x_ref[...] + gate_ref[0] * out


def _merge(x2d, y_ret, y_att, px, gate, w_o_ret, w_o_att, w_out, rows_per_mod, tm):
    rows = x2d.shape[0]
    half = D_MODEL // 2
    weight = lambda shape: pl.BlockSpec(shape, lambda i: (0, 0), pipeline_mode=pl.Buffered(1))
    mg = lambda off: pl.BlockSpec((tm, half), lambda i: (i, off // half))
    return pl.pallas_call(
        _merge_kernel,
        grid=(rows // tm,),
        in_specs=[pl.BlockSpec((tm, D_MODEL), lambda i: (i, 0)),
                  pl.BlockSpec((tm, RET_V_W), lambda i: (i, 0)),
                  pl.BlockSpec((tm, ATT_Q_W), lambda i: (i, 0)),
                  mg(OFF_MG_RET), mg(OFF_MG_RET + half), mg(OFF_MG_ATT), mg(OFF_MG_ATT + half),
                  pl.BlockSpec((1, 1, D_MODEL), lambda i: ((i * tm) // rows_per_mod, 0, 0)),
                  weight((RET_V_W, D_MODEL)), weight((ATT_Q_W, D_MODEL)), weight((D_MODEL, D_MODEL))],
        out_specs=pl.BlockSpec((tm, D_MODEL), lambda i: (i, 0)),
        out_shape=jax.ShapeDtypeStruct((rows, D_MODEL), F32),
        compiler_params=pltpu.CompilerParams(
            dimension_semantics=("parallel",), vmem_limit_bytes=VMEM_LIMIT),
        name="merge",
    )(x2d, y_ret, y_att, px, px, px, px, gate, w_o_ret, w_o_att, w_out)


def _rope_tables(s_len):
    rows = s_len // GRID_W
    row = jnp.repeat(jnp.arange(rows, dtype=F32), GRID_W)
    col = jnp.tile(jnp.arange(GRID_W, dtype=F32), rows)
    half = ATT_HEAD_DIM // 2
    freqs = ROPE_THETA ** (-jnp.arange(0, half, 2, dtype=F32) / half)
    ang = jnp.concatenate([row[:, None] * freqs, col[:, None] * freqs], axis=-1)
    cos, sin = jnp.cos(ang), jnp.sin(ang)
    return jnp.concatenate([cos, cos], axis=-1), jnp.concatenate([-sin, sin], axis=-1)


def _split_pairs(t):
    lead = t.shape[:-1]
    t = t.reshape(*lead, -1, ATT_HEAD_DIM // 2, 2)
    return jnp.swapaxes(t, -1, -2).reshape(*lead, -1)


def _layer(x, ctx, c, c_ctx, norm_w, w_ada, b_ada, w_in, ret_log2_decay,
           q_norm_w, k_norm_w, w_o_ret, w_o_att, w_out):
    b, s_len, _ = x.shape
    c_len = ctx.shape[1]

    mod = _adaln(jnp.concatenate([c, c_ctx[None]], axis=0), w_ada, b_ada)
    shift, scale, gate = (mod[:, i * D_MODEL:(i + 1) * D_MODEL] for i in range(3))
    lat = lambda t: t[:b].reshape(b, 1, D_MODEL)
    cx = lambda t: t[b:].reshape(1, 1, D_MODEL)

    w_bf = w_in.astype(BF16)
    for off, width in ((OFF_ATT_K, ATT_KV_W), (OFF_ATT_Q, ATT_Q_W)):
        w_bf = w_bf.at[:, off:off + width].set(_split_pairs(w_bf[:, off:off + width]))
    q_norm_w = _split_pairs(q_norm_w)
    k_norm_w = _split_pairs(k_norm_w)

    x2d = x.reshape(b * s_len, D_MODEL)
    px = _inproj(x2d, norm_w, lat(scale), lat(shift), w_bf, IN_COLS, s_len, 2048, 1536)
    pc = _inproj(ctx.reshape(b * c_len, D_MODEL), norm_w, cx(scale), cx(shift), w_bf, KV_COLS,
                 b * c_len, 1024, 1792)
    px3 = px.reshape(b, s_len, IN_COLS)
    pc3 = pc.reshape(b, c_len, KV_COLS)

    cos, sin_signed = _rope_tables(s_len)
    k_all, vt_all = _kv_prep(px3, pc3, k_norm_w, cos, sin_signed)
    y_att = _attention(px3, k_all, vt_all, q_norm_w, cos, sin_signed)

    decay3 = ret_log2_decay.astype(F32).T.reshape(RET_HEADS, 2, 1)
    y_ret = _retention(px3, pc3, decay3)

    x_new = _merge(x2d, y_ret.reshape(b * s_len, RET_V_W), y_att.reshape(b * s_len, ATT_Q_W), px,
                   lat(gate), w_o_ret.astype(BF16), w_o_att.astype(BF16), w_out.astype(BF16),
                   s_len, 1024)
    return x_new.reshape(b, s_len, D_MODEL)


def kernel(x, c, ctx, c_ctx, norm_w, w_ada, b_ada, w_in, ret_log2_decay, q_norm_w, k_norm_w,
           w_o_ret, w_o_att, w_out):
    depth = norm_w.shape[0]
    assert depth == 1, "context update between layers is not implemented"
    return _layer(x, ctx, c, c_ctx, norm_w[0], w_ada[0], b_ada[0], w_in[0], ret_log2_decay[0],
                  q_norm_w[0], k_norm_w[0], w_o_ret[0], w_o_att[0], w_out[0])
```

```python
import functools

import jax
import jax.numpy as jnp
from jax import lax
from jax.experimental import pallas as pl
from jax.experimental.pallas import tpu as pltpu

D_MODEL = 1024
GRID_W = 64
RET_HEADS = 4
RET_DK = 256
RET_DV = 512
RET_QK_W = RET_HEADS * RET_DK
RET_V_W = RET_HEADS * RET_DV
ATT_HEADS = 8
ATT_KV_HEADS = 2
ATT_HEAD_DIM = 128
ATT_Q_W = ATT_HEADS * ATT_HEAD_DIM
ATT_KV_W = ATT_KV_HEADS * ATT_HEAD_DIM
ATT_REP = ATT_HEADS // ATT_KV_HEADS
ROPE_THETA = 10000.0
NORM_EPS = 1e-6
KV_COLS = RET_QK_W + RET_V_W + 2 * ATT_KV_W
LOG2E = 1.4426950408889634

OFF_RET_K = 0
OFF_RET_V = OFF_RET_K + RET_QK_W
OFF_ATT_K = OFF_RET_V + RET_V_W
OFF_ATT_V = OFF_ATT_K + ATT_KV_W
OFF_RET_Q = OFF_ATT_V + ATT_KV_W
OFF_RET_G = OFF_RET_Q + RET_QK_W
OFF_ATT_Q = OFF_RET_G + RET_V_W
OFF_ATT_G = OFF_ATT_Q + ATT_Q_W
OFF_MG_RET = OFF_ATT_G + ATT_Q_W
OFF_MG_ATT = OFF_MG_RET + D_MODEL
IN_COLS = OFF_MG_ATT + D_MODEL
assert OFF_RET_Q == KV_COLS

RET_CHUNK = 256
BF16_SUBLANES = 16
VT_ROWS = ATT_HEAD_DIM + BF16_SUBLANES
VMEM_LIMIT = 56 * 1024 * 1024

BF16 = jnp.bfloat16
F32 = jnp.float32


def _dot(a, b):
    return jnp.dot(a, b, preferred_element_type=F32)


def _dot_nt(a, b):
    return lax.dot_general(a, b, (((1,), (1,)), ((), ())), preferred_element_type=F32)


def _dot_tn(a, b):
    return lax.dot_general(a, b, (((0,), (0,)), ((), ())), preferred_element_type=F32)


def _silu(x):
    return x * jax.nn.sigmoid(x)


def _adaln_kernel(c_ref, w_ref, b_ref, o_ref):
    o_ref[...] = _dot(_silu(c_ref[...]), w_ref[...]) + b_ref[...]


def _adaln(cc, w_ada, b_ada):
    rows = cc.shape[0]
    tn = D_MODEL
    return pl.pallas_call(
        _adaln_kernel,
        grid=(3 * D_MODEL // tn,),
        in_specs=[pl.BlockSpec((rows, D_MODEL), lambda j: (0, 0)),
                  pl.BlockSpec((D_MODEL, tn), lambda j: (0, j)),
                  pl.BlockSpec((1, tn), lambda j: (0, j))],
        out_specs=pl.BlockSpec((rows, tn), lambda j: (0, j)),
        out_shape=jax.ShapeDtypeStruct((rows, 3 * D_MODEL), F32),
        compiler_params=pltpu.CompilerParams(vmem_limit_bytes=VMEM_LIMIT),
        name="adaln",
    )(cc, w_ada, b_ada.reshape(1, -1))


def _inproj_rows_kernel(x_ref, nw_ref, sc_ref, sh_ref, w_ref, o_ref, *, tn):
    x = x_ref[...]
    ms = jnp.mean(x * x, axis=-1, keepdims=True)
    y = x * lax.rsqrt(ms + NORM_EPS) * nw_ref[...]
    h = (y * (1.0 + sc_ref[0]) + sh_ref[0]).astype(BF16)
    for c0 in range(0, o_ref.shape[1], tn):
        o_ref[:, c0:c0 + tn] = _dot(h, w_ref[:, c0:c0 + tn]).astype(BF16)


def _inproj_rows(x2d, norm_w, scale, shift, w_bf, n_cols, rows_per_mod, tm, tn):
    rows = x2d.shape[0]
    mod_map = lambda i: ((i * tm) // rows_per_mod, 0, 0)
    return pl.pallas_call(
        functools.partial(_inproj_rows_kernel, tn=tn),
        grid=(rows // tm,),
        in_specs=[pl.BlockSpec((tm, D_MODEL), lambda i: (i, 0)),
                  pl.BlockSpec((1, D_MODEL), lambda i: (0, 0)),
                  pl.BlockSpec((1, 1, D_MODEL), mod_map),
                  pl.BlockSpec((1, 1, D_MODEL), mod_map),
                  pl.BlockSpec((D_MODEL, n_cols), lambda i: (0, 0), pipeline_mode=pl.Buffered(1))],
        out_specs=pl.BlockSpec((tm, n_cols), lambda i: (i, 0)),
        out_shape=jax.ShapeDtypeStruct((rows, n_cols), BF16),
        compiler_params=pltpu.CompilerParams(
            dimension_semantics=("parallel",), vmem_limit_bytes=VMEM_LIMIT),
        name="inproj_rows",
    )(x2d, norm_w.reshape(1, -1), scale, shift, w_bf)


def _head_rms(t, w):
    ones = jnp.ones((ATT_HEAD_DIM, ATT_HEAD_DIM), BF16)
    ms = _dot((t * t).astype(BF16), ones) * (1.0 / ATT_HEAD_DIM)
    return t * lax.rsqrt(ms + NORM_EPS) * w


def _rope(t, cos, sin_signed):
    return t * cos + pltpu.roll(t, ATT_HEAD_DIM // 2, 1) * sin_signed


def _kv_prep_kernel(kl_ref, vl_ref, kc_ref, vc_ref, kw_ref, cos_ref, sin_ref, ko_ref, vt_ref):
    s_len = kl_ref.shape[1]
    l_k = ko_ref.shape[1]
    kw = kw_ref[...]
    cos = cos_ref[...]
    sin = sin_ref[...]
    for g in range(ATT_KV_HEADS):
        cols = slice(g * ATT_HEAD_DIM, (g + 1) * ATT_HEAD_DIM)
        kl = _head_rms(kl_ref[0, :, cols].astype(F32), kw)
        ko_ref[0, :s_len, cols] = _rope(kl, cos, sin).astype(BF16)
        kc = _head_rms(kc_ref[0, :, cols].astype(F32), kw)
        ko_ref[0, s_len:, cols] = kc.astype(BF16)
        vt_ref[0, g, :ATT_HEAD_DIM, :s_len] = vl_ref[0, :, cols].astype(F32).T.astype(BF16)
        vt_ref[0, g, :ATT_HEAD_DIM, s_len:] = vc_ref[0, :, cols].astype(F32).T.astype(BF16)
        vt_ref[0, g, ATT_HEAD_DIM:, :] = jnp.ones((VT_ROWS - ATT_HEAD_DIM, l_k), BF16)


def _kv_prep(px3, pc3, k_norm_w, cos, sin_signed):
    b, s_len, _ = px3.shape
    c_len = pc3.shape[1]
    l_k = s_len + c_len
    w = ATT_KV_W
    return pl.pallas_call(
        _kv_prep_kernel,
        grid=(b,),
        in_specs=[pl.BlockSpec((1, s_len, w), lambda i: (i, 0, OFF_ATT_K // w)),
                  pl.BlockSpec((1, s_len, w), lambda i: (i, 0, OFF_ATT_V // w)),
                  pl.BlockSpec((1, c_len, w), lambda i: (i, 0, OFF_ATT_K // w)),
                  pl.BlockSpec((1, c_len, w), lambda i: (i, 0, OFF_ATT_V // w)),
                  pl.BlockSpec((1, ATT_HEAD_DIM), lambda i: (0, 0)),
                  pl.BlockSpec((s_len, ATT_HEAD_DIM), lambda i: (0, 0)),
                  pl.BlockSpec((s_len, ATT_HEAD_DIM), lambda i: (0, 0))],
        out_specs=[pl.BlockSpec((1, l_k, w), lambda i: (i, 0, 0)),
                   pl.BlockSpec((1, ATT_KV_HEADS, VT_ROWS, l_k), lambda i: (i, 0, 0, 0))],
        out_shape=[jax.ShapeDtypeStruct((b, l_k, w), BF16),
                   jax.ShapeDtypeStruct((b, ATT_KV_HEADS, VT_ROWS, l_k), BF16)],
        compiler_params=pltpu.CompilerParams(
            dimension_semantics=("parallel",), vmem_limit_bytes=VMEM_LIMIT),
        name="kv_prep",
    )(px3, px3, pc3, pc3, k_norm_w.reshape(1, -1), cos, sin_signed)


KEY_TILE = 256
ATT_UNIT = 512
ATT_TQ = 1024


def _attention_kernel(q_ref, k_ref, vt_ref, g_ref, qw_ref, cos_ref, sin_ref, o_ref, qs_ref):
    qw = qw_ref[...] * (ATT_HEAD_DIM ** -0.5 * LOG2E)
    cos = cos_ref[...]
    sin = sin_ref[...]
    k = k_ref[0]
    vt = vt_ref[0, 0]

    for r in range(ATT_REP):
        cols = slice(r * ATT_HEAD_DIM, (r + 1) * ATT_HEAD_DIM)
        q = _rope(_head_rms(q_ref[0, :, cols].astype(F32), qw), cos, sin)
        qs_ref[r] = q.astype(BF16)

    tq = q_ref.shape[1]
    units = [(r, c0) for r in range(ATT_REP) for c0 in range(0, tq, ATT_UNIT)]

    def logits(u):
        r, c0 = units[u]
        s_t = _dot_nt(k, qs_ref[r, c0:c0 + ATT_UNIT, :])
        return s_t, jnp.max(s_t, axis=0, keepdims=True)

    def finish(u, s_t, m, s_next):
        r, c0 = units[u]
        rows = slice(c0, c0 + ATT_UNIT)
        cols = slice(r * ATT_HEAD_DIM, (r + 1) * ATT_HEAD_DIM)
        if s_next is None:
            p_t = jnp.exp2(s_t - m).astype(BF16)
        else:
            tiles = []
            for j in range(0, s_t.shape[0], KEY_TILE):
                probe = pltpu.bitcast(s_next[j:j + 8, :], jnp.uint32)
                zero = ((probe >> 16) >> 16).astype(F32)[0:1, :]
                tiles.append(jnp.exp2(s_t[j:j + KEY_TILE, :] - (m + zero)).astype(BF16))
            p_t = jnp.concatenate(tiles, axis=0)
        o_t = _dot(vt, p_t)
        o = (o_t[:ATT_HEAD_DIM] * (1.0 / o_t[ATT_HEAD_DIM:ATT_HEAD_DIM + 1])).T
        o_ref[0, rows, cols] = (o * _silu(g_ref[0, rows, cols].astype(F32))).astype(BF16)

    pending = logits(0)
    for u in range(1, len(units)):
        nxt = logits(u)
        finish(u - 1, *pending, nxt[0])
        pending = nxt
    finish(len(units) - 1, *pending, None)


def _attention(px3, k_all, vt_all, q_norm_w, cos, sin_signed):
    b, s_len, _ = px3.shape
    l_k = k_all.shape[1]
    gw = ATT_REP * ATT_HEAD_DIM
    tq = ATT_TQ
    return pl.pallas_call(
        _attention_kernel,
        grid=(b, ATT_KV_HEADS, s_len // tq),
        in_specs=[pl.BlockSpec((1, tq, gw), lambda i, g, t: (i, t, OFF_ATT_Q // gw + g)),
                  pl.BlockSpec((1, l_k, ATT_HEAD_DIM), lambda i, g, t: (i, 0, g)),
                  pl.BlockSpec((1, 1, VT_ROWS, l_k), lambda i, g, t: (i, g, 0, 0)),
                  pl.BlockSpec((1, tq, gw), lambda i, g, t: (i, t, OFF_ATT_G // gw + g)),
                  pl.BlockSpec((1, ATT_HEAD_DIM), lambda i, g, t: (0, 0)),
                  pl.BlockSpec((tq, ATT_HEAD_DIM), lambda i, g, t: (t, 0)),
                  pl.BlockSpec((tq, ATT_HEAD_DIM), lambda i, g, t: (t, 0))],
        out_specs=pl.BlockSpec((1, tq, gw), lambda i, g, t: (i, t, g)),
        out_shape=jax.ShapeDtypeStruct((b, s_len, ATT_Q_W), BF16),
        scratch_shapes=[pltpu.VMEM((ATT_REP, tq, ATT_HEAD_DIM), BF16)],
        compiler_params=pltpu.CompilerParams(
            dimension_semantics=("parallel", "parallel", "arbitrary"),
            vmem_limit_bytes=VMEM_LIMIT),
        name="attention",
    )(px3, k_all, vt_all, px3, q_norm_w.reshape(1, -1), cos, sin_signed)


def _retention_kernel(d_ref, q_ref, k_ref, v_ref, g_ref, kc_ref, vc_ref, o_ref, sb_ref):
    s_len = q_ref.shape[1]
    c_len = kc_ref.shape[1]
    ch = RET_CHUNK
    n_chunks = s_len // ch

    d = d_ref[0]
    lg = jnp.log1p(-jnp.exp2(d))
    lg_f = lg[0:1, :]
    lg_b = lg[1:2, :]

    idx = lax.broadcasted_iota(jnp.int32, (ch, 1), 0).astype(F32)
    qd_f = (jnp.exp(lg_f * (idx + 1.0)) * (RET_DK ** -0.5)).astype(BF16)
    kd_f = jnp.exp(lg_f * (ch - 1.0 - idx)).astype(BF16)
    qd_b = (jnp.exp(lg_b * (ch - idx)) * (RET_DK ** -0.5)).astype(BF16)
    kd_b = jnp.exp(lg_b * idx).astype(BF16)
    cd_f = jnp.exp(lg_f * ch)
    cd_b = jnp.exp(lg_b * ch)

    row = lax.broadcasted_iota(jnp.int32, (ch, ch), 0)
    col = lax.broadcasted_iota(jnp.int32, (ch, ch), 1)
    rel = (row - col).astype(F32)
    mask = jnp.where(rel > 0, jnp.exp(lg_f * rel),
                     jnp.where(rel < 0, jnp.exp(-lg_b * rel), 2.0)) * (RET_DK ** -0.5)

    def chunk(n):
        return slice(n * ch, (n + 1) * ch)

    def masked_scores(n):
        return (_dot_nt(q_ref[0, chunk(n), :], k_ref[0, chunk(n), :]) * mask).astype(BF16)

    pos = lax.broadcasted_iota(jnp.int32, (c_len, 1), 0).astype(F32)
    kc = kc_ref[0].astype(F32)
    vc = vc_ref[0]
    s_f = _dot_tn((kc * jnp.exp(lg_f * (c_len - 1.0 - pos))).astype(BF16), vc)
    s_b = _dot_tn((kc * jnp.exp(lg_b * pos)).astype(BF16), vc)

    for n in reversed(range(n_chunks)):
        sb_ref[n] = s_b.astype(BF16)
        if n > 0:
            s_b = s_b * cd_b + _dot_tn(k_ref[0, chunk(n), :] * kd_b, v_ref[0, chunk(n), :])

    scores = masked_scores(0)
    for n in range(n_chunks):
        q = q_ref[0, chunk(n), :]
        v = v_ref[0, chunk(n), :]
        last = n + 1 == n_chunks
        next_scores = None if last else masked_scores(n + 1)
        kv_f = None if last else _dot_tn(k_ref[0, chunk(n), :] * kd_f, v)
        o = _dot(q * qd_b, sb_ref[n]) + _dot(q * qd_f, s_f.astype(BF16)) + _dot(scores, v)
        ms = jnp.mean(o * o, axis=-1, keepdims=True)
        o = (o * lax.rsqrt(ms + NORM_EPS)).astype(BF16)
        o_ref[0, chunk(n), :] = o * _silu(g_ref[0, chunk(n), :])
        if not last:
            s_f = s_f * cd_f + kv_f
            scores = next_scores


def _retention(px3, pc3, decay3):
    b, s_len, _ = px3.shape
    c_len = pc3.shape[1]
    return pl.pallas_call(
        _retention_kernel,
        grid=(b, RET_HEADS),
        in_specs=[pl.BlockSpec((1, 2, 1), lambda i, h: (h, 0, 0)),
                  pl.BlockSpec((1, s_len, RET_DK), lambda i, h: (i, 0, OFF_RET_Q // RET_DK + h)),
                  pl.BlockSpec((1, s_len, RET_DK), lambda i, h: (i, 0, OFF_RET_K // RET_DK + h)),
                  pl.BlockSpec((1, s_len, RET_DV), lambda i, h: (i, 0, OFF_RET_V // RET_DV + h)),
                  pl.BlockSpec((1, s_len, RET_DV), lambda i, h: (i, 0, OFF_RET_G // RET_DV + h)),
                  pl.BlockSpec((1, c_len, RET_DK), lambda i, h: (i, 0, h)),
                  pl.BlockSpec((1, c_len, RET_DV), lambda i, h: (i, 0, RET_QK_W // RET_DV + h))],
        out_specs=pl.BlockSpec((1, s_len, RET_DV), lambda i, h: (i, 0, h)),
        out_shape=jax.ShapeDtypeStruct((b, s_len, RET_V_W), BF16),
        scratch_shapes=[pltpu.VMEM((s_len // RET_CHUNK, RET_DK, RET_DV), BF16)],
        compiler_params=pltpu.CompilerParams(
            dimension_semantics=("parallel", "arbitrary"), vmem_limit_bytes=VMEM_LIMIT),
        name="retention",
    )(decay3, px3, px3, px3, px3, pc3, pc3)


def _merge_kernel(x_ref, yr_ref, ya_ref, mr0_ref, mr1_ref, ma0_ref, ma1_ref, gate_ref,
                  wr_ref, wa_ref, wo_ref, o_ref):
    a = _dot(yr_ref[...], wr_ref[...])
    b = _dot(ya_ref[...], wa_ref[...])
    mg_ret = jnp.concatenate([mr0_ref[...], mr1_ref[...]], axis=-1).astype(F32)
    mg_att = jnp.concatenate([ma0_ref[...], ma1_ref[...]], axis=-1).astype(F32)
    y = jax.nn.sigmoid(mg_ret) * a + jax.nn.sigmoid(mg_att) * b
    out = _dot(y.astype(BF16), wo_ref[...])
    o_ref[...] = x_ref[...] + gate_ref[0] * out


def _merge(x2d, y_ret, y_att, px, gate, w_o_ret, w_o_att, w_out, rows_per_mod, tm):
    rows = x2d.shape[0]
    half = D_MODEL // 2
    weight = lambda shape: pl.BlockSpec(shape, lambda i: (0, 0), pipeline_mode=pl.Buffered(1))
    mg = lambda off: pl.BlockSpec((tm, half), lambda i: (i, off // half))
    return pl.pallas_call(
        _merge_kernel,
        grid=(rows // tm,),
        in_specs=[pl.BlockSpec((tm, D_MODEL), lambda i: (i, 0)),
                  pl.BlockSpec((tm, RET_V_W), lambda i: (i, 0)),
                  pl.BlockSpec((tm, ATT_Q_W), lambda i: (i, 0)),
                  mg(OFF_MG_RET), mg(OFF_MG_RET + half), mg(OFF_MG_ATT), mg(OFF_MG_ATT + half),
                  pl.BlockSpec((1, 1, D_MODEL), lambda i: ((i * tm) // rows_per_mod, 0, 0)),
                  weight((RET_V_W, D_MODEL)), weight((ATT_Q_W, D_MODEL)), weight((D_MODEL, D_MODEL))],
        out_specs=pl.BlockSpec((tm, D_MODEL), lambda i: (i, 0)),
        out_shape=jax.ShapeDtypeStruct((rows, D_MODEL), F32),
        compiler_params=pltpu.CompilerParams(
            dimension_semantics=("parallel",), vmem_limit_bytes=VMEM_LIMIT),
        name="merge",
    )(x2d, y_ret, y_att, px, px, px, px, gate, w_o_ret, w_o_att, w_out)


def _rope_tables(s_len):
    rows = s_len // GRID_W
    row = jnp.repeat(jnp.arange(rows, dtype=F32), GRID_W)
    col = jnp.tile(jnp.arange(GRID_W, dtype=F32), rows)
    half = ATT_HEAD_DIM // 2
    freqs = ROPE_THETA ** (-jnp.arange(0, half, 2, dtype=F32) / half)
    ang = jnp.concatenate([row[:, None] * freqs, col[:, None] * freqs], axis=-1)
    cos, sin = jnp.cos(ang), jnp.sin(ang)
    return jnp.concatenate([cos, cos], axis=-1), jnp.concatenate([-sin, sin], axis=-1)


def _split_pairs(t):
    lead = t.shape[:-1]
    t = t.reshape(*lead, -1, ATT_HEAD_DIM // 2, 2)
    return jnp.swapaxes(t, -1, -2).reshape(*lead, -1)


def _layer(x, ctx, c, c_ctx, norm_w, w_ada, b_ada, w_in, ret_log2_decay,
           q_norm_w, k_norm_w, w_o_ret, w_o_att, w_out):
    b, s_len, _ = x.shape
    c_len = ctx.shape[1]

    mod = _adaln(jnp.concatenate([c, c_ctx[None]], axis=0), w_ada, b_ada)
    shift, scale, gate = (mod[:, i * D_MODEL:(i + 1) * D_MODEL] for i in range(3))
    lat = lambda t: t[:b].reshape(b, 1, D_MODEL)
    cx = lambda t: t[b:].reshape(1, 1, D_MODEL)

    w_bf = w_in.astype(BF16)
    for off, width in ((OFF_ATT_K, ATT_KV_W), (OFF_ATT_Q, ATT_Q_W)):
        w_bf = w_bf.at[:, off:off + width].set(_split_pairs(w_bf[:, off:off + width]))
    q_norm_w = _split_pairs(q_norm_w)
    k_norm_w = _split_pairs(k_norm_w)

    x2d = x.reshape(b * s_len, D_MODEL)
    px = _inproj_rows(x2d, norm_w, lat(scale), lat(shift), w_bf, IN_COLS, s_len, 512, 1536)
    pc = _inproj_rows(ctx.reshape(b * c_len, D_MODEL), norm_w, cx(scale), cx(shift), w_bf, KV_COLS,
                      b * c_len, 512, 1792)
    px3 = px.reshape(b, s_len, IN_COLS)
    pc3 = pc.reshape(b, c_len, KV_COLS)

    cos, sin_signed = _rope_tables(s_len)
    k_all, vt_all = _kv_prep(px3, pc3, k_norm_w, cos, sin_signed)
    y_att = _attention(px3, k_all, vt_all, q_norm_w, cos, sin_signed)

    decay3 = ret_log2_decay.astype(F32).T.reshape(RET_HEADS, 2, 1)
    y_ret = _retention(px3, pc3, decay3)

    x_new = _merge(x2d, y_ret.reshape(b * s_len, RET_V_W), y_att.reshape(b * s_len, ATT_Q_W), px,
                   lat(gate), w_o_ret.astype(BF16), w_o_att.astype(BF16), w_out.astype(BF16),
                   s_len, 1024)
    return x_new.reshape(b, s_len, D_MODEL)


def kernel(x, c, ctx, c_ctx, norm_w, w_ada, b_ada, w_in, ret_log2_decay, q_norm_w, k_norm_w,
           w_o_ret, w_o_att, w_out):
    depth = norm_w.shape[0]
    assert depth == 1, "context update between layers is not implemented"
    return _layer(x, ctx, c, c_ctx, norm_w[0], w_ada[0], b_ada[0], w_in[0], ret_log2_decay[0],
                  q_norm_w[0], k_norm_w[0], w_o_ret[0], w_o_att[0], w_out[0])
```

```python
import functools

import jax
import jax.numpy as jnp
from jax import lax
from jax.experimental import pallas as pl
from jax.experimental.pallas import tpu as pltpu

D_MODEL = 1024
GRID_W = 64
RET_HEADS = 4
RET_DK = 256
RET_DV = 512
RET_QK_W = RET_HEADS * RET_DK
RET_V_W = RET_HEADS * RET_DV
ATT_HEADS = 8
ATT_KV_HEADS = 2
ATT_HEAD_DIM = 128
ATT_Q_W = ATT_HEADS * ATT_HEAD_DIM
ATT_KV_W = ATT_KV_HEADS * ATT_HEAD_DIM
ATT_REP = ATT_HEADS // ATT_KV_HEADS
ROPE_THETA = 10000.0
NORM_EPS = 1e-6
KV_COLS = RET_QK_W + RET_V_W + 2 * ATT_KV_W
LOG2E = 1.4426950408889634

OFF_RET_K = 0
OFF_RET_V = OFF_RET_K + RET_QK_W
OFF_ATT_K = OFF_RET_V + RET_V_W
OFF_ATT_V = OFF_ATT_K + ATT_KV_W
OFF_RET_Q = OFF_ATT_V + ATT_KV_W
OFF_RET_G = OFF_RET_Q + RET_QK_W
OFF_ATT_Q = OFF_RET_G + RET_V_W
OFF_ATT_G = OFF_ATT_Q + ATT_Q_W
OFF_MG_RET = OFF_ATT_G + ATT_Q_W
OFF_MG_ATT = OFF_MG_RET + D_MODEL
IN_COLS = OFF_MG_ATT + D_MODEL
assert OFF_RET_Q == KV_COLS

RET_CHUNK = 256
BF16_SUBLANES = 16
VT_ROWS = ATT_HEAD_DIM + BF16_SUBLANES
VMEM_LIMIT = 56 * 1024 * 1024

BF16 = jnp.bfloat16
F32 = jnp.float32


def _dot(a, b):
    return jnp.dot(a, b, preferred_element_type=F32)


def _dot_nt(a, b):
    return lax.dot_general(a, b, (((1,), (1,)), ((), ())), preferred_element_type=F32)


def _dot_tn(a, b):
    return lax.dot_general(a, b, (((0,), (0,)), ((), ())), preferred_element_type=F32)


def _silu(x):
    return x * jax.nn.sigmoid(x)


def _adaln_kernel(c_ref, w_ref, b_ref, o_ref):
    o_ref[...] = _dot(_silu(c_ref[...]), w_ref[...]) + b_ref[...]


def _adaln(cc, w_ada, b_ada):
    rows = cc.shape[0]
    tn = D_MODEL
    return pl.pallas_call(
        _adaln_kernel,
        grid=(3 * D_MODEL // tn,),
        in_specs=[pl.BlockSpec((rows, D_MODEL), lambda j: (0, 0)),
                  pl.BlockSpec((D_MODEL, tn), lambda j: (0, j)),
                  pl.BlockSpec((1, tn), lambda j: (0, j))],
        out_specs=pl.BlockSpec((rows, tn), lambda j: (0, j)),
        out_shape=jax.ShapeDtypeStruct((rows, 3 * D_MODEL), F32),
        compiler_params=pltpu.CompilerParams(vmem_limit_bytes=VMEM_LIMIT),
        name="adaln",
    )(cc, w_ada, b_ada.reshape(1, -1))


def _inproj_rows_kernel(x_ref, nw_ref, sc_ref, sh_ref, w_ref, o_ref, h0_ref, h1_ref, *, tn):
    t = pl.program_id(0)

    def normalise(h_ref):
        x = x_ref[...]
        ms = jnp.mean(x * x, axis=-1, keepdims=True)
        gain = nw_ref[...] * (1.0 + sc_ref[0])
        h_ref[...] = (x * lax.rsqrt(ms + NORM_EPS) * gain + sh_ref[0]).astype(BF16)

    def project(h_ref):
        h = h_ref[...]
        for c0 in range(0, o_ref.shape[1], tn):
            o_ref[:, c0:c0 + tn] = _dot(h, w_ref[:, c0:c0 + tn]).astype(BF16)

    @pl.when(t == 0)
    def _():
        normalise(h0_ref)

    @pl.when(t % 2 == 1)
    def _():
        normalise(h1_ref)
        project(h0_ref)

    @pl.when(jnp.logical_and(t > 0, t % 2 == 0))
    def _():
        normalise(h0_ref)
        project(h1_ref)


def _inproj_rows(x2d, norm_w, scale, shift, w_bf, n_cols, rows_per_mod, tm, tn):
    n_tiles = x2d.shape[0] // tm
    norm_tile = lambda t: jnp.minimum(t, n_tiles - 1)
    mod_map = lambda t: ((norm_tile(t) * tm) // rows_per_mod, 0, 0)
    return pl.pallas_call(
        functools.partial(_inproj_rows_kernel, tn=tn),
        grid=(n_tiles + 1,),
        in_specs=[pl.BlockSpec((tm, D_MODEL), lambda t: (norm_tile(t), 0)),
                  pl.BlockSpec((1, D_MODEL), lambda t: (0, 0)),
                  pl.BlockSpec((1, 1, D_MODEL), mod_map),
                  pl.BlockSpec((1, 1, D_MODEL), mod_map),
                  pl.BlockSpec((D_MODEL, n_cols), lambda t: (0, 0), pipeline_mode=pl.Buffered(1))],
        out_specs=pl.BlockSpec((tm, n_cols), lambda t: (jnp.maximum(t - 1, 0), 0)),
        out_shape=jax.ShapeDtypeStruct((n_tiles * tm, n_cols), BF16),
        scratch_shapes=[pltpu.VMEM((tm, D_MODEL), BF16), pltpu.VMEM((tm, D_MODEL), BF16)],
        compiler_params=pltpu.CompilerParams(
            dimension_semantics=("arbitrary",), vmem_limit_bytes=VMEM_LIMIT),
        name="inproj_rows",
    )(x2d, norm_w.reshape(1, -1), scale, shift, w_bf)


def _head_rms(t, w):
    ones = jnp.ones((ATT_HEAD_DIM, ATT_HEAD_DIM), BF16)
    ms = _dot((t * t).astype(BF16), ones) * (1.0 / ATT_HEAD_DIM)
    return t * lax.rsqrt(ms + NORM_EPS) * w


def _rope(t, cos, sin_signed):
    return t * cos + pltpu.roll(t, ATT_HEAD_DIM // 2, 1) * sin_signed


def _kv_prep_kernel(kl_ref, vl_ref, kc_ref, vc_ref, kw_ref, cos_ref, sin_ref, ko_ref, vt_ref):
    s_len = kl_ref.shape[1]
    l_k = ko_ref.shape[1]
    kw = kw_ref[...]
    cos = cos_ref[...]
    sin = sin_ref[...]
    for g in range(ATT_KV_HEADS):
        cols = slice(g * ATT_HEAD_DIM, (g + 1) * ATT_HEAD_DIM)
        kl = _head_rms(kl_ref[0, :, cols].astype(F32), kw)
        ko_ref[0, :s_len, cols] = _rope(kl, cos, sin).astype(BF16)
        kc = _head_rms(kc_ref[0, :, cols].astype(F32), kw)
        ko_ref[0, s_len:, cols] = kc.astype(BF16)
        vt_ref[0, g, :ATT_HEAD_DIM, :s_len] = vl_ref[0, :, cols].astype(F32).T.astype(BF16)
        vt_ref[0, g, :ATT_HEAD_DIM, s_len:] = vc_ref[0, :, cols].astype(F32).T.astype(BF16)
        vt_ref[0, g, ATT_HEAD_DIM:, :] = jnp.ones((VT_ROWS - ATT_HEAD_DIM, l_k), BF16)


def _kv_prep(px3, pc3, k_norm_w, cos, sin_signed):
    b, s_len, _ = px3.shape
    c_len = pc3.shape[1]
    l_k = s_len + c_len
    w = ATT_KV_W
    return pl.pallas_call(
        _kv_prep_kernel,
        grid=(b,),
        in_specs=[pl.BlockSpec((1, s_len, w), lambda i: (i, 0, OFF_ATT_K // w)),
                  pl.BlockSpec((1, s_len, w), lambda i: (i, 0, OFF_ATT_V // w)),
                  pl.BlockSpec((1, c_len, w), lambda i: (i, 0, OFF_ATT_K // w)),
                  pl.BlockSpec((1, c_len, w), lambda i: (i, 0, OFF_ATT_V // w)),
                  pl.BlockSpec((1, ATT_HEAD_DIM), lambda i: (0, 0)),
                  pl.BlockSpec((s_len, ATT_HEAD_DIM), lambda i: (0, 0)),
                  pl.BlockSpec((s_len, ATT_HEAD_DIM), lambda i: (0, 0))],
        out_specs=[pl.BlockSpec((1, l_k, w), lambda i: (i, 0, 0)),
                   pl.BlockSpec((1, ATT_KV_HEADS, VT_ROWS, l_k), lambda i: (i, 0, 0, 0))],
        out_shape=[jax.ShapeDtypeStruct((b, l_k, w), BF16),
                   jax.ShapeDtypeStruct((b, ATT_KV_HEADS, VT_ROWS, l_k), BF16)],
        compiler_params=pltpu.CompilerParams(
            dimension_semantics=("parallel",), vmem_limit_bytes=VMEM_LIMIT),
        name="kv_prep",
    )(px3, px3, pc3, pc3, k_norm_w.reshape(1, -1), cos, sin_signed)


KEY_TILE = 256
ATT_UNIT = 512
ATT_TQ = 1024


def _attention_kernel(q_ref, k_ref, vt_ref, g_ref, qw_ref, cos_ref, sin_ref, o_ref, qs_ref):
    qw = qw_ref[...] * (ATT_HEAD_DIM ** -0.5 * LOG2E)
    cos = cos_ref[...]
    sin = sin_ref[...]
    k = k_ref[0]
    vt = vt_ref[0, 0]

    for r in range(ATT_REP):
        cols = slice(r * ATT_HEAD_DIM, (r + 1) * ATT_HEAD_DIM)
        q = _rope(_head_rms(q_ref[0, :, cols].astype(F32), qw), cos, sin)
        qs_ref[r] = q.astype(BF16)

    tq = q_ref.shape[1]
    units = [(r, c0) for r in range(ATT_REP) for c0 in range(0, tq, ATT_UNIT)]

    def logits(u):
        r, c0 = units[u]
        s_t = _dot_nt(k, qs_ref[r, c0:c0 + ATT_UNIT, :])
        return s_t, jnp.max(s_t, axis=0, keepdims=True)

    def finish(u, s_t, m, s_next):
        r, c0 = units[u]
        rows = slice(c0, c0 + ATT_UNIT)
        cols = slice(r * ATT_HEAD_DIM, (r + 1) * ATT_HEAD_DIM)
        if s_next is None:
            p_t = jnp.exp2(s_t - m).astype(BF16)
        else:
            tiles = []
            for j in range(0, s_t.shape[0], KEY_TILE):
                probe = pltpu.bitcast(s_next[j:j + 8, :], jnp.uint32)
                zero = ((probe >> 16) >> 16).astype(F32)[0:1, :]
                tiles.append(jnp.exp2(s_t[j:j + KEY_TILE, :] - (m + zero)).astype(BF16))
            p_t = jnp.concatenate(tiles, axis=0)
        o_t = _dot(vt, p_t)
        o = (o_t[:ATT_HEAD_DIM] * (1.0 / o_t[ATT_HEAD_DIM:ATT_HEAD_DIM + 1])).T
        o_ref[0, rows, cols] = (o * _silu(g_ref[0, rows, cols].astype(F32))).astype(BF16)

    pending = logits(0)
    for u in range(1, len(units)):
        nxt = logits(u)
        finish(u - 1, *pending, nxt[0])
        pending = nxt
    finish(len(units) - 1, *pending, None)


def _attention(px3, k_all, vt_all, q_norm_w, cos, sin_signed):
    b, s_len, _ = px3.shape
    l_k = k_all.shape[1]
    gw = ATT_REP * ATT_HEAD_DIM
    tq = ATT_TQ
    return pl.pallas_call(
        _attention_kernel,
        grid=(b, ATT_KV_HEADS, s_len // tq),
        in_specs=[pl.BlockSpec((1, tq, gw), lambda i, g, t: (i, t, OFF_ATT_Q // gw + g)),
                  pl.BlockSpec((1, l_k, ATT_HEAD_DIM), lambda i, g, t: (i, 0, g)),
                  pl.BlockSpec((1, 1, VT_ROWS, l_k), lambda i, g, t: (i, g, 0, 0)),
                  pl.BlockSpec((1, tq, gw), lambda i, g, t: (i, t, OFF_ATT_G // gw + g)),
                  pl.BlockSpec((1, ATT_HEAD_DIM), lambda i, g, t: (0, 0)),
                  pl.BlockSpec((tq, ATT_HEAD_DIM), lambda i, g, t: (t, 0)),
                  pl.BlockSpec((tq, ATT_HEAD_DIM), lambda i, g, t: (t, 0))],
        out_specs=pl.BlockSpec((1, tq, gw), lambda i, g, t: (i, t, g)),
        out_shape=jax.ShapeDtypeStruct((b, s_len, ATT_Q_W), BF16),
        scratch_shapes=[pltpu.VMEM((ATT_REP, tq, ATT_HEAD_DIM), BF16)],
        compiler_params=pltpu.CompilerParams(
            dimension_semantics=("parallel", "parallel", "arbitrary"),
            vmem_limit_bytes=VMEM_LIMIT),
        name="attention",
    )(px3, k_all, vt_all, px3, q_norm_w.reshape(1, -1), cos, sin_signed)


def _retention_kernel(d_ref, q_ref, k_ref, v_ref, g_ref, kc_ref, vc_ref, o_ref, sb_ref):
    s_len = q_ref.shape[1]
    c_len = kc_ref.shape[1]
    ch = RET_CHUNK
    n_chunks = s_len // ch

    d = d_ref[0]
    lg = jnp.log1p(-jnp.exp2(d))
    lg_f = lg[0:1, :]
    lg_b = lg[1:2, :]

    idx = lax.broadcasted_iota(jnp.int32, (ch, 1), 0).astype(F32)
    qd_f = (jnp.exp(lg_f * (idx + 1.0)) * (RET_DK ** -0.5)).astype(BF16)
    kd_f = jnp.exp(lg_f * (ch - 1.0 - idx)).astype(BF16)
    qd_b = (jnp.exp(lg_b * (ch - idx)) * (RET_DK ** -0.5)).astype(BF16)
    kd_b = jnp.exp(lg_b * idx).astype(BF16)
    cd_f = jnp.exp(lg_f * ch)
    cd_b = jnp.exp(lg_b * ch)

    row = lax.broadcasted_iota(jnp.int32, (ch, ch), 0)
    col = lax.broadcasted_iota(jnp.int32, (ch, ch), 1)
    rel = (row - col).astype(F32)
    mask = jnp.where(rel > 0, jnp.exp(lg_f * rel),
                     jnp.where(rel < 0, jnp.exp(-lg_b * rel), 2.0)) * (RET_DK ** -0.5)

    def chunk(n):
        return slice(n * ch, (n + 1) * ch)

    def masked_scores(n):
        return (_dot_nt(q_ref[0, chunk(n), :], k_ref[0, chunk(n), :]) * mask).astype(BF16)

    pos = lax.broadcasted_iota(jnp.int32, (c_len, 1), 0).astype(F32)
    kc = kc_ref[0].astype(F32)
    vc = vc_ref[0]
    s_f = _dot_tn((kc * jnp.exp(lg_f * (c_len - 1.0 - pos))).astype(BF16), vc)
    s_b = _dot_tn((kc * jnp.exp(lg_b * pos)).astype(BF16), vc)

    for n in reversed(range(n_chunks)):
        sb_ref[n] = s_b.astype(BF16)
        if n > 0:
            s_b = s_b * cd_b + _dot_tn(k_ref[0, chunk(n), :] * kd_b, v_ref[0, chunk(n), :])

    scores = masked_scores(0)
    for n in range(n_chunks):
        q = q_ref[0, chunk(n), :]
        v = v_ref[0, chunk(n), :]
        last = n + 1 == n_chunks
        next_scores = None if last else masked_scores(n + 1)
        kv_f = None if last else _dot_tn(k_ref[0, chunk(n), :] * kd_f, v)
        o = _dot(q * qd_b, sb_ref[n]) + _dot(q * qd_f, s_f.astype(BF16)) + _dot(scores, v)
        ms = jnp.mean(o * o, axis=-1, keepdims=True)
        o = (o * lax.rsqrt(ms + NORM_EPS)).astype(BF16)
        o_ref[0, chunk(n), :] = o * _silu(g_ref[0, chunk(n), :])
        if not last:
            s_f = s_f * cd_f + kv_f
            scores = next_scores


def _retention(px3, pc3, decay3):
    b, s_len, _ = px3.shape
    c_len = pc3.shape[1]
    return pl.pallas_call(
        _retention_kernel,
        grid=(b, RET_HEADS),
        in_specs=[pl.BlockSpec((1, 2, 1), lambda i, h: (h, 0, 0)),
                  pl.BlockSpec((1, s_len, RET_DK), lambda i, h: (i, 0, OFF_RET_Q // RET_DK + h)),
                  pl.BlockSpec((1, s_len, RET_DK), lambda i, h: (i, 0, OFF_RET_K // RET_DK + h)),
                  pl.BlockSpec((1, s_len, RET_DV), lambda i, h: (i, 0, OFF_RET_V // RET_DV + h)),
                  pl.BlockSpec((1, s_len, RET_DV), lambda i, h: (i, 0, OFF_RET_G // RET_DV + h)),
                  pl.BlockSpec((1, c_len, RET_DK), lambda i, h: (i, 0, h)),
                  pl.BlockSpec((1, c_len, RET_DV), lambda i, h: (i, 0, RET_QK_W // RET_DV + h))],
        out_specs=pl.BlockSpec((1, s_len, RET_DV), lambda i, h: (i, 0, h)),
        out_shape=jax.ShapeDtypeStruct((b, s_len, RET_V_W), BF16),
        scratch_shapes=[pltpu.VMEM((s_len // RET_CHUNK, RET_DK, RET_DV), BF16)],
        compiler_params=pltpu.CompilerParams(
            dimension_semantics=("parallel", "arbitrary"), vmem_limit_bytes=VMEM_LIMIT),
        name="retention",
    )(decay3, px3, px3, px3, px3, pc3, pc3)


def _merge_kernel(x_ref, yr_ref, ya_ref, mr0_ref, mr1_ref, ma0_ref, ma1_ref, gate_ref,
                  wr_ref, wa_ref, wo_ref, o_ref):
    a = _dot(yr_ref[...], wr_ref[...])
    b = _dot(ya_ref[...], wa_ref[...])
    mg_ret = jnp.concatenate([mr0_ref[...], mr1_ref[...]], axis=-1).astype(F32)
    mg_att = jnp.concatenate([ma0_ref[...], ma1_ref[...]], axis=-1).astype(F32)
    y = jax.nn.sigmoid(mg_ret) * a + jax.nn.sigmoid(mg_att) * b
    out = _dot(y.astype(BF16), wo_ref[...])
    o_ref[...] = x_ref[...] + gate_ref[0] * out


def _merge(x2d, y_ret, y_att, px, gate, w_o_ret, w_o_att, w_out, rows_per_mod, tm):
    rows = x2d.shape[0]
    half = D_MODEL // 2
    weight = lambda shape: pl.BlockSpec(shape, lambda i: (0, 0), pipeline_mode=pl.Buffered(1))
    mg = lambda off: pl.BlockSpec((tm, half), lambda i: (i, off // half))
    return pl.pallas_call(
        _merge_kernel,
        grid=(rows // tm,),
        in_specs=[pl.BlockSpec((tm, D_MODEL), lambda i: (i, 0)),
                  pl.BlockSpec((tm, RET_V_W), lambda i: (i, 0)),
                  pl.BlockSpec((tm, ATT_Q_W), lambda i: (i, 0)),
                  mg(OFF_MG_RET), mg(OFF_MG_RET + half), mg(OFF_MG_ATT), mg(OFF_MG_ATT + half),
                  pl.BlockSpec((1, 1, D_MODEL), lambda i: ((i * tm) // rows_per_mod, 0, 0)),
                  weight((RET_V_W, D_MODEL)), weight((ATT_Q_W, D_MODEL)), weight((D_MODEL, D_MODEL))],
        out_specs=pl.BlockSpec((tm, D_MODEL), lambda i: (i, 0)),
        out_shape=jax.ShapeDtypeStruct((rows, D_MODEL), F32),
        compiler_params=pltpu.CompilerParams(
            dimension_semantics=("parallel",), vmem_limit_bytes=VMEM_LIMIT),
        name="merge",
    )(x2d, y_ret, y_att, px, px, px, px, gate, w_o_ret, w_o_att, w_out)


def _rope_tables(s_len):
    rows = s_len // GRID_W
    row = jnp.repeat(jnp.arange(rows, dtype=F32), GRID_W)
    col = jnp.tile(jnp.arange(GRID_W, dtype=F32), rows)
    half = ATT_HEAD_DIM // 2
    freqs = ROPE_THETA ** (-jnp.arange(0, half, 2, dtype=F32) / half)
    ang = jnp.concatenate([row[:, None] * freqs, col[:, None] * freqs], axis=-1)
    cos, sin = jnp.cos(ang), jnp.sin(ang)
    return jnp.concatenate([cos, cos], axis=-1), jnp.concatenate([-sin, sin], axis=-1)


def _split_pairs(t):
    lead = t.shape[:-1]
    t = t.reshape(*lead, -1, ATT_HEAD_DIM // 2, 2)
    return jnp.swapaxes(t, -1, -2).reshape(*lead, -1)


def _layer(x, ctx, c, c_ctx, norm_w, w_ada, b_ada, w_in, ret_log2_decay,
           q_norm_w, k_norm_w, w_o_ret, w_o_att, w_out):
    b, s_len, _ = x.shape
    c_len = ctx.shape[1]

    mod = _adaln(jnp.concatenate([c, c_ctx[None]], axis=0), w_ada, b_ada)
    shift, scale, gate = (mod[:, i * D_MODEL:(i + 1) * D_MODEL] for i in range(3))
    lat = lambda t: t[:b].reshape(b, 1, D_MODEL)
    cx = lambda t: t[b:].reshape(1, 1, D_MODEL)

    w_bf = w_in.astype(BF16)
    for off, width in ((OFF_ATT_K, ATT_KV_W), (OFF_ATT_Q, ATT_Q_W)):
        w_bf = w_bf.at[:, off:off + width].set(_split_pairs(w_bf[:, off:off + width]))
    q_norm_w = _split_pairs(q_norm_w)
    k_norm_w = _split_pairs(k_norm_w)

    x2d = x.reshape(b * s_len, D_MODEL)
    px = _inproj_rows(x2d, norm_w, lat(scale), lat(shift), w_bf, IN_COLS, s_len, 512, 1536)
    pc = _inproj_rows(ctx.reshape(b * c_len, D_MODEL), norm_w, cx(scale), cx(shift), w_bf, KV_COLS,
                      b * c_len, 512, 1792)
    px3 = px.reshape(b, s_len, IN_COLS)
    pc3 = pc.reshape(b, c_len, KV_COLS)

    cos, sin_signed = _rope_tables(s_len)
    k_all, vt_all = _kv_prep(px3, pc3, k_norm_w, cos, sin_signed)
    y_att = _attention(px3, k_all, vt_all, q_norm_w, cos, sin_signed)

    decay3 = ret_log2_decay.astype(F32).T.reshape(RET_HEADS, 2, 1)
    y_ret = _retention(px3, pc3, decay3)

    x_new = _merge(x2d, y_ret.reshape(b * s_len, RET_V_W), y_att.reshape(b * s_len, ATT_Q_W), px,
                   lat(gate), w_o_ret.astype(BF16), w_o_att.astype(BF16), w_out.astype(BF16),
                   s_len, 1024)
    return x_new.reshape(b, s_len, D_MODEL)


def kernel(x, c, ctx, c_ctx, norm_w, w_ada, b_ada, w_in, ret_log2_decay, q_norm_w, k_norm_w,
           w_o_ret, w_o_att, w_out):
    depth = norm_w.shape[0]
    assert depth == 1, "context update between layers is not implemented"
    return _layer(x, ctx, c, c_ctx, norm_w[0], w_ada[0], b_ada[0], w_in[0], ret_log2_decay[0],
                  q_norm_w[0], k_norm_w[0], w_o_ret[0], w_o_att[0], w_out[0])
```

```python
import functools

import jax
import jax.numpy as jnp
from jax import lax
from jax.experimental import pallas as pl
from jax.experimental.pallas import tpu as pltpu

D_MODEL = 1024
GRID_W = 64
RET_HEADS = 4
RET_DK = 256
RET_DV = 512
RET_QK_W = RET_HEADS * RET_DK
RET_V_W = RET_HEADS * RET_DV
ATT_HEADS = 8
ATT_KV_HEADS = 2
ATT_HEAD_DIM = 128
ATT_Q_W = ATT_HEADS * ATT_HEAD_DIM
ATT_KV_W = ATT_KV_HEADS * ATT_HEAD_DIM
ATT_REP = ATT_HEADS // ATT_KV_HEADS
ROPE_THETA = 10000.0
NORM_EPS = 1e-6
KV_COLS = RET_QK_W + RET_V_W + 2 * ATT_KV_W
LOG2E = 1.4426950408889634

OFF_RET_K = 0
OFF_RET_V = OFF_RET_K + RET_QK_W
OFF_ATT_K = OFF_RET_V + RET_V_W
OFF_ATT_V = OFF_ATT_K + ATT_KV_W
OFF_RET_Q = OFF_ATT_V + ATT_KV_W
OFF_RET_G = OFF_RET_Q + RET_QK_W
OFF_ATT_Q = OFF_RET_G + RET_V_W
OFF_ATT_G = OFF_ATT_Q + ATT_Q_W
OFF_MG_RET = OFF_ATT_G + ATT_Q_W
OFF_MG_ATT = OFF_MG_RET + D_MODEL
IN_COLS = OFF_MG_ATT + D_MODEL
assert OFF_RET_Q == KV_COLS

RET_CHUNK = 256
PROJ_TM = 512
PROJ_TN = 1536
PROJ_TN_CTX = 1792
MERGE_TM = 1024
BF16_SUBLANES = 16
VT_ROWS = ATT_HEAD_DIM + BF16_SUBLANES
VMEM_LIMIT = 56 * 1024 * 1024

BF16 = jnp.bfloat16
F32 = jnp.float32


def _dot(a, b):
    return jnp.dot(a, b, preferred_element_type=F32)


def _dot_nt(a, b):
    return lax.dot_general(a, b, (((1,), (1,)), ((), ())), preferred_element_type=F32)


def _dot_tn(a, b):
    return lax.dot_general(a, b, (((0,), (0,)), ((), ())), preferred_element_type=F32)


def _silu(x):
    return x * jax.nn.sigmoid(x)


def _adaln_kernel(c_ref, w_ref, b_ref, o_ref):
    o_ref[...] = _dot(_silu(c_ref[...]), w_ref[...]) + b_ref[...]


def _adaln(cc, w_ada, b_ada):
    rows = cc.shape[0]
    tn = D_MODEL
    return pl.pallas_call(
        _adaln_kernel,
        grid=(3 * D_MODEL // tn,),
        in_specs=[pl.BlockSpec((rows, D_MODEL), lambda j: (0, 0)),
                  pl.BlockSpec((D_MODEL, tn), lambda j: (0, j)),
                  pl.BlockSpec((1, tn), lambda j: (0, j))],
        out_specs=pl.BlockSpec((rows, tn), lambda j: (0, j)),
        out_shape=jax.ShapeDtypeStruct((rows, 3 * D_MODEL), F32),
        compiler_params=pltpu.CompilerParams(vmem_limit_bytes=VMEM_LIMIT),
        name="adaln",
    )(cc, w_ada, b_ada.reshape(1, -1))


def _head_rms(t, w):
    ms = jnp.mean(t * t, axis=-1, keepdims=True)
    return t * lax.rsqrt(ms + NORM_EPS) * w


def _rope(t, cos, sin_signed):
    return t * cos + pltpu.roll(t, ATT_HEAD_DIM // 2, 1) * sin_signed


def _inproj_rows_kernel(x_ref, nw_ref, sc_ref, sh_ref, w_ref, qw_ref, kw_ref, cos_ref, sin_ref,
                        o_ref, *, tn, rope, prep_q):
    x = x_ref[...]
    ms = jnp.mean(x * x, axis=-1, keepdims=True)
    y = x * lax.rsqrt(ms + NORM_EPS) * nw_ref[...]
    h = (y * (1.0 + sc_ref[0]) + sh_ref[0]).astype(BF16)
    n_cols = o_ref.shape[1]
    heads = [(OFF_ATT_K + g * ATT_HEAD_DIM, kw_ref[...]) for g in range(ATT_KV_HEADS)]
    if prep_q:
        qw = qw_ref[...] * (ATT_HEAD_DIM ** -0.5 * LOG2E)
        heads += [(OFF_ATT_Q + r * ATT_HEAD_DIM, qw) for r in range(ATT_HEADS)]
    for c0 in range(0, n_cols, tn):
        res = _dot(h, w_ref[:, c0:c0 + tn])
        o_ref[:, c0:c0 + tn] = res.astype(BF16)
        for col, w in heads:
            if c0 <= col < c0 + tn:
                t = _head_rms(res[:, col - c0:col - c0 + ATT_HEAD_DIM], w)
                if rope:
                    t = _rope(t, cos_ref[...], sin_ref[...])
                o_ref[:, col:col + ATT_HEAD_DIM] = t.astype(BF16)


def _inproj_rows(x2d, norm_w, scale, shift, w_bf, q_norm_w, k_norm_w, cos, sin_signed,
                 n_cols, rows_per_mod, tm, tn, rope):
    rows = x2d.shape[0]
    pos_tiles = cos.shape[0] // tm
    mod_map = lambda i: ((i * tm) // rows_per_mod, 0, 0)
    vec = lambda i: (0, 0)
    return pl.pallas_call(
        functools.partial(_inproj_rows_kernel, tn=tn, rope=rope, prep_q=n_cols > OFF_ATT_Q),
        grid=(rows // tm,),
        in_specs=[pl.BlockSpec((tm, D_MODEL), lambda i: (i, 0)),
                  pl.BlockSpec((1, D_MODEL), vec),
                  pl.BlockSpec((1, 1, D_MODEL), mod_map),
                  pl.BlockSpec((1, 1, D_MODEL), mod_map),
                  pl.BlockSpec((D_MODEL, n_cols), vec, pipeline_mode=pl.Buffered(1)),
                  pl.BlockSpec((1, ATT_HEAD_DIM), vec),
                  pl.BlockSpec((1, ATT_HEAD_DIM), vec),
                  pl.BlockSpec((tm, ATT_HEAD_DIM), lambda i: (i % pos_tiles, 0)),
                  pl.BlockSpec((tm, ATT_HEAD_DIM), lambda i: (i % pos_tiles, 0))],
        out_specs=pl.BlockSpec((tm, n_cols), lambda i: (i, 0)),
        out_shape=jax.ShapeDtypeStruct((rows, n_cols), BF16),
        compiler_params=pltpu.CompilerParams(
            dimension_semantics=("parallel",), vmem_limit_bytes=VMEM_LIMIT),
        name="inproj_rows",
    )(x2d, norm_w.reshape(1, -1), scale, shift, w_bf, q_norm_w.reshape(1, -1),
      k_norm_w.reshape(1, -1), cos, sin_signed)


def _kv_prep_kernel(kl_ref, vl_ref, kc_ref, vc_ref, ko_ref, vt_ref):
    s_len = kl_ref.shape[1]
    l_k = ko_ref.shape[1]
    ko_ref[0, :s_len, :] = kl_ref[0]
    ko_ref[0, s_len:, :] = kc_ref[0]
    for g in range(ATT_KV_HEADS):
        cols = slice(g * ATT_HEAD_DIM, (g + 1) * ATT_HEAD_DIM)
        vt_ref[0, g, :ATT_HEAD_DIM, :s_len] = vl_ref[0, :, cols].T
        vt_ref[0, g, :ATT_HEAD_DIM, s_len:] = vc_ref[0, :, cols].T
        vt_ref[0, g, ATT_HEAD_DIM:, :] = jnp.ones((VT_ROWS - ATT_HEAD_DIM, l_k), BF16)


def _kv_prep(px3, pc3):
    b, s_len, _ = px3.shape
    c_len = pc3.shape[1]
    l_k = s_len + c_len
    w = ATT_KV_W
    return pl.pallas_call(
        _kv_prep_kernel,
        grid=(b,),
        in_specs=[pl.BlockSpec((1, s_len, w), lambda i: (i, 0, OFF_ATT_K // w)),
                  pl.BlockSpec((1, s_len, w), lambda i: (i, 0, OFF_ATT_V // w)),
                  pl.BlockSpec((1, c_len, w), lambda i: (i, 0, OFF_ATT_K // w)),
                  pl.BlockSpec((1, c_len, w), lambda i: (i, 0, OFF_ATT_V // w))],
        out_specs=[pl.BlockSpec((1, l_k, w), lambda i: (i, 0, 0)),
                   pl.BlockSpec((1, ATT_KV_HEADS, VT_ROWS, l_k), lambda i: (i, 0, 0, 0))],
        out_shape=[jax.ShapeDtypeStruct((b, l_k, w), BF16),
                   jax.ShapeDtypeStruct((b, ATT_KV_HEADS, VT_ROWS, l_k), BF16)],
        compiler_params=pltpu.CompilerParams(
            dimension_semantics=("parallel",), vmem_limit_bytes=VMEM_LIMIT),
        name="kv_prep",
    )(px3, px3, pc3, pc3)


KEY_TILE = 256
ATT_UNIT = 512
ATT_TQ = 1024


def _attention_kernel(q_ref, k_ref, vt_ref, g_ref, o_ref):
    k = k_ref[0]
    vt = vt_ref[0, 0]

    tq = q_ref.shape[1]
    units = [(r, c0) for r in range(ATT_REP) for c0 in range(0, tq, ATT_UNIT)]

    def logits(u):
        r, c0 = units[u]
        q = q_ref[0, c0:c0 + ATT_UNIT, r * ATT_HEAD_DIM:(r + 1) * ATT_HEAD_DIM]
        s_t = _dot_nt(k, q)
        return s_t, jnp.max(s_t, axis=0, keepdims=True)

    def finish(u, s_t, m, s_next):
        r, c0 = units[u]
        rows = slice(c0, c0 + ATT_UNIT)
        cols = slice(r * ATT_HEAD_DIM, (r + 1) * ATT_HEAD_DIM)
        if s_next is None:
            p_t = jnp.exp2(s_t - m).astype(BF16)
        else:
            tiles = []
            for j in range(0, s_t.shape[0], KEY_TILE):
                probe = pltpu.bitcast(s_next[j:j + 8, :], jnp.uint32)
                zero = ((probe >> 16) >> 16).astype(F32)[0:1, :]
                tiles.append(jnp.exp2(s_t[j:j + KEY_TILE, :] - (m + zero)).astype(BF16))
            p_t = jnp.concatenate(tiles, axis=0)
        o_t = _dot(vt, p_t)
        o = (o_t[:ATT_HEAD_DIM] * (1.0 / o_t[ATT_HEAD_DIM:ATT_HEAD_DIM + 1])).T
        o_ref[0, rows, cols] = (o * _silu(g_ref[0, rows, cols].astype(F32))).astype(BF16)

    pending = logits(0)
    for u in range(1, len(units)):
        nxt = logits(u)
        finish(u - 1, *pending, nxt[0])
        pending = nxt
    finish(len(units) - 1, *pending, None)


def _attention(px3, k_all, vt_all):
    b, s_len, _ = px3.shape
    l_k = k_all.shape[1]
    gw = ATT_REP * ATT_HEAD_DIM
    tq = ATT_TQ
    return pl.pallas_call(
        _attention_kernel,
        grid=(b, ATT_KV_HEADS, s_len // tq),
        in_specs=[pl.BlockSpec((1, tq, gw), lambda i, g, t: (i, t, OFF_ATT_Q // gw + g)),
                  pl.BlockSpec((1, l_k, ATT_HEAD_DIM), lambda i, g, t: (i, 0, g)),
                  pl.BlockSpec((1, 1, VT_ROWS, l_k), lambda i, g, t: (i, g, 0, 0)),
                  pl.BlockSpec((1, tq, gw), lambda i, g, t: (i, t, OFF_ATT_G // gw + g))],
        out_specs=pl.BlockSpec((1, tq, gw), lambda i, g, t: (i, t, g)),
        out_shape=jax.ShapeDtypeStruct((b, s_len, ATT_Q_W), BF16),
        compiler_params=pltpu.CompilerParams(
            dimension_semantics=("parallel", "parallel", "arbitrary"),
            vmem_limit_bytes=VMEM_LIMIT),
        name="attention",
    )(px3, k_all, vt_all, px3)


def _retention_kernel(d_ref, q_ref, k_ref, v_ref, g_ref, kc_ref, vc_ref, o_ref, sb_ref):
    s_len = q_ref.shape[1]
    c_len = kc_ref.shape[1]
    ch = RET_CHUNK
    n_chunks = s_len // ch

    d = d_ref[0]
    lg = jnp.log1p(-jnp.exp2(d))
    lg_f = lg[0:1, :]
    lg_b = lg[1:2, :]

    idx = lax.broadcasted_iota(jnp.int32, (ch, 1), 0).astype(F32)
    qd_f = (jnp.exp(lg_f * (idx + 1.0)) * (RET_DK ** -0.5)).astype(BF16)
    kd_f = jnp.exp(lg_f * (ch - 1.0 - idx)).astype(BF16)
    qd_b = (jnp.exp(lg_b * (ch - idx)) * (RET_DK ** -0.5)).astype(BF16)
    kd_b = jnp.exp(lg_b * idx).astype(BF16)
    cd_f = jnp.exp(lg_f * ch)
    cd_b = jnp.exp(lg_b * ch)

    row = lax.broadcasted_iota(jnp.int32, (ch, ch), 0)
    col = lax.broadcasted_iota(jnp.int32, (ch, ch), 1)
    rel = (row - col).astype(F32)
    mask = jnp.where(rel > 0, jnp.exp(lg_f * rel),
                     jnp.where(rel < 0, jnp.exp(-lg_b * rel), 2.0)) * (RET_DK ** -0.5)

    def chunk(n):
        return slice(n * ch, (n + 1) * ch)

    def masked_scores(n):
        return (_dot_nt(q_ref[0, chunk(n), :], k_ref[0, chunk(n), :]) * mask).astype(BF16)

    pos = lax.broadcasted_iota(jnp.int32, (c_len, 1), 0).astype(F32)
    kc = kc_ref[0].astype(F32)
    vc = vc_ref[0]
    s_f = _dot_tn((kc * jnp.exp(lg_f * (c_len - 1.0 - pos))).astype(BF16), vc)
    s_b = _dot_tn((kc * jnp.exp(lg_b * pos)).astype(BF16), vc)

    for n in reversed(range(n_chunks)):
        sb_ref[n] = s_b.astype(BF16)
        if n > 0:
            s_b = s_b * cd_b + _dot_tn(k_ref[0, chunk(n), :] * kd_b, v_ref[0, chunk(n), :])

    scores = masked_scores(0)
    for n in range(n_chunks):
        q = q_ref[0, chunk(n), :]
        v = v_ref[0, chunk(n), :]
        last = n + 1 == n_chunks
        next_scores = None if last else masked_scores(n + 1)
        kv_f = None if last else _dot_tn(k_ref[0, chunk(n), :] * kd_f, v)
        o = _dot(q * qd_b, sb_ref[n]) + _dot(q * qd_f, s_f.astype(BF16)) + _dot(scores, v)
        ms = jnp.mean(o * o, axis=-1, keepdims=True)
        o = (o * lax.rsqrt(ms + NORM_EPS)).astype(BF16)
        o_ref[0, chunk(n), :] = o * _silu(g_ref[0, chunk(n), :])
        if not last:
            s_f = s_f * cd_f + kv_f
            scores = next_scores


def _retention(px3, pc3, decay3):
    b, s_len, _ = px3.shape
    c_len = pc3.shape[1]
    return pl.pallas_call(
        _retention_kernel,
        grid=(b, RET_HEADS),
        in_specs=[pl.BlockSpec((1, 2, 1), lambda i, h: (h, 0, 0)),
                  pl.BlockSpec((1, s_len, RET_DK), lambda i, h: (i, 0, OFF_RET_Q // RET_DK + h)),
                  pl.BlockSpec((1, s_len, RET_DK), lambda i, h: (i, 0, OFF_RET_K // RET_DK + h)),
                  pl.BlockSpec((1, s_len, RET_DV), lambda i, h: (i, 0, OFF_RET_V // RET_DV + h)),
                  pl.BlockSpec((1, s_len, RET_DV), lambda i, h: (i, 0, OFF_RET_G // RET_DV + h)),
                  pl.BlockSpec((1, c_len, RET_DK), lambda i, h: (i, 0, h)),
                  pl.BlockSpec((1, c_len, RET_DV), lambda i, h: (i, 0, RET_QK_W // RET_DV + h))],
        out_specs=pl.BlockSpec((1, s_len, RET_DV), lambda i, h: (i, 0, h)),
        out_shape=jax.ShapeDtypeStruct((b, s_len, RET_V_W), BF16),
        scratch_shapes=[pltpu.VMEM((s_len // RET_CHUNK, RET_DK, RET_DV), BF16)],
        compiler_params=pltpu.CompilerParams(
            dimension_semantics=("parallel", "arbitrary"), vmem_limit_bytes=VMEM_LIMIT),
        name="retention",
    )(decay3, px3, px3, px3, px3, pc3, pc3)


def _merge_kernel(x_ref, yr_ref, ya_ref, mr0_ref, mr1_ref, ma0_ref, ma1_ref, gate_ref,
                  wr_ref, wa_ref, wo_ref, o_ref):
    a = _dot(yr_ref[...], wr_ref[...])
    b = _dot(ya_ref[...], wa_ref[...])
    mg_ret = jnp.concatenate([mr0_ref[...], mr1_ref[...]], axis=-1).astype(F32)
    mg_att = jnp.concatenate([ma0_ref[...], ma1_ref[...]], axis=-1).astype(F32)
    y = jax.nn.sigmoid(mg_ret) * a + jax.nn.sigmoid(mg_att) * b
    out = _dot(y.astype(BF16), wo_ref[...])
    o_ref[...] = x_ref[...] + gate_ref[0] * out


def _merge(x2d, y_ret, y_att, px, gate, w_o_ret, w_o_att, w_out, rows_per_mod, tm):
    rows = x2d.shape[0]
    half = D_MODEL // 2
    weight = lambda shape: pl.BlockSpec(shape, lambda i: (0, 0), pipeline_mode=pl.Buffered(1))
    mg = lambda off: pl.BlockSpec((tm, half), lambda i: (i, off // half))
    return pl.pallas_call(
        _merge_kernel,
        grid=(rows // tm,),
        in_specs=[pl.BlockSpec((tm, D_MODEL), lambda i: (i, 0)),
                  pl.BlockSpec((tm, RET_V_W), lambda i: (i, 0)),
                  pl.BlockSpec((tm, ATT_Q_W), lambda i: (i, 0)),
                  mg(OFF_MG_RET), mg(OFF_MG_RET + half), mg(OFF_MG_ATT), mg(OFF_MG_ATT + half),
                  pl.BlockSpec((1, 1, D_MODEL), lambda i: ((i * tm) // rows_per_mod, 0, 0)),
                  weight((RET_V_W, D_MODEL)), weight((ATT_Q_W, D_MODEL)), weight((D_MODEL, D_MODEL))],
        out_specs=pl.BlockSpec((tm, D_MODEL), lambda i: (i, 0)),
        out_shape=jax.ShapeDtypeStruct((rows, D_MODEL), F32),
        compiler_params=pltpu.CompilerParams(
            dimension_semantics=("parallel",), vmem_limit_bytes=VMEM_LIMIT),
        name="merge",
    )(x2d, y_ret, y_att, px, px, px, px, gate, w_o_ret, w_o_att, w_out)


def _rope_tables(s_len):
    rows = s_len // GRID_W
    row = jnp.repeat(jnp.arange(rows, dtype=F32), GRID_W)
    col = jnp.tile(jnp.arange(GRID_W, dtype=F32), rows)
    half = ATT_HEAD_DIM // 2
    freqs = ROPE_THETA ** (-jnp.arange(0, half, 2, dtype=F32) / half)
    ang = jnp.concatenate([row[:, None] * freqs, col[:, None] * freqs], axis=-1)
    cos, sin = jnp.cos(ang), jnp.sin(ang)
    return jnp.concatenate([cos, cos], axis=-1), jnp.concatenate([-sin, sin], axis=-1)


def _split_pairs(t):
    lead = t.shape[:-1]
    t = t.reshape(*lead, -1, ATT_HEAD_DIM // 2, 2)
    return jnp.swapaxes(t, -1, -2).reshape(*lead, -1)


def _layer(x, ctx, c, c_ctx, norm_w, w_ada, b_ada, w_in, ret_log2_decay,
           q_norm_w, k_norm_w, w_o_ret, w_o_att, w_out):
    b, s_len, _ = x.shape
    c_len = ctx.shape[1]

    mod = _adaln(jnp.concatenate([c, c_ctx[None]], axis=0), w_ada, b_ada)
    shift, scale, gate = (mod[:, i * D_MODEL:(i + 1) * D_MODEL] for i in range(3))
    lat = lambda t: t[:b].reshape(b, 1, D_MODEL)
    cx = lambda t: t[b:].reshape(1, 1, D_MODEL)

    w_bf = w_in.astype(BF16)
    for off, width in ((OFF_ATT_K, ATT_KV_W), (OFF_ATT_Q, ATT_Q_W)):
        w_bf = w_bf.at[:, off:off + width].set(_split_pairs(w_bf[:, off:off + width]))
    q_norm_w = _split_pairs(q_norm_w)
    k_norm_w = _split_pairs(k_norm_w)

    cos, sin_signed = _rope_tables(s_len)
    x2d = x.reshape(b * s_len, D_MODEL)
    px = _inproj_rows(x2d, norm_w, lat(scale), lat(shift), w_bf, q_norm_w, k_norm_w, cos, sin_signed,
                      IN_COLS, s_len, PROJ_TM, PROJ_TN, rope=True)
    pc = _inproj_rows(ctx.reshape(b * c_len, D_MODEL), norm_w, cx(scale), cx(shift), w_bf,
                      q_norm_w, k_norm_w, cos, sin_signed,
                      KV_COLS, b * c_len, PROJ_TM, PROJ_TN_CTX, rope=False)
    px3 = px.reshape(b, s_len, IN_COLS)
    pc3 = pc.reshape(b, c_len, KV_COLS)

    k_all, vt_all = _kv_prep(px3, pc3)
    y_att = _attention(px3, k_all, vt_all)

    decay3 = ret_log2_decay.astype(F32).T.reshape(RET_HEADS, 2, 1)
    y_ret = _retention(px3, pc3, decay3)

    x_new = _merge(x2d, y_ret.reshape(b * s_len, RET_V_W), y_att.reshape(b * s_len, ATT_Q_W), px,
                   lat(gate), w_o_ret.astype(BF16), w_o_att.astype(BF16), w_out.astype(BF16),
                   s_len, MERGE_TM)
    return x_new.reshape(b, s_len, D_MODEL)


def kernel(x, c, ctx, c_ctx, norm_w, w_ada, b_ada, w_in, ret_log2_decay, q_norm_w, k_norm_w,
           w_o_ret, w_o_att, w_out):
    depth = norm_w.shape[0]
    assert depth == 1, "context update between layers is not implemented"
    return _layer(x, ctx, c, c_ctx, norm_w[0], w_ada[0], b_ada[0], w_in[0], ret_log2_decay[0],
                  q_norm_w[0], k_norm_w[0], w_o_ret[0], w_o_att[0], w_out[0])
```

```python
import functools

import jax
import jax.numpy as jnp
from jax import lax
from jax.experimental import pallas as pl
from jax.experimental.pallas import tpu as pltpu

D_MODEL = 1024
GRID_W = 64
RET_HEADS = 4
RET_DK = 256
RET_DV = 512
RET_QK_W = RET_HEADS * RET_DK
RET_V_W = RET_HEADS * RET_DV
ATT_HEADS = 8
ATT_KV_HEADS = 2
ATT_HEAD_DIM = 128
ATT_Q_W = ATT_HEADS * ATT_HEAD_DIM
ATT_KV_W = ATT_KV_HEADS * ATT_HEAD_DIM
ATT_REP = ATT_HEADS // ATT_KV_HEADS
ROPE_THETA = 10000.0
NORM_EPS = 1e-6
KV_COLS = RET_QK_W + RET_V_W + 2 * ATT_KV_W
LOG2E = 1.4426950408889634

OFF_RET_K = 0
OFF_RET_V = OFF_RET_K + RET_QK_W
OFF_ATT_K = OFF_RET_V + RET_V_W
OFF_ATT_V = OFF_ATT_K + ATT_KV_W
OFF_RET_Q = OFF_ATT_V + ATT_KV_W
OFF_RET_G = OFF_RET_Q + RET_QK_W
OFF_ATT_Q = OFF_RET_G + RET_V_W
OFF_ATT_G = OFF_ATT_Q + ATT_Q_W
OFF_MG_RET = OFF_ATT_G + ATT_Q_W
OFF_MG_ATT = OFF_MG_RET + D_MODEL
IN_COLS = OFF_MG_ATT + D_MODEL
assert OFF_RET_Q == KV_COLS

RET_CHUNK = 256
PROJ_TM = 512
PROJ_TN = 1536
PROJ_TN_CTX = 1792
MERGE_TM = 1024
BF16_SUBLANES = 16
VT_ROWS = ATT_HEAD_DIM + BF16_SUBLANES
VMEM_LIMIT = 56 * 1024 * 1024

BF16 = jnp.bfloat16
F32 = jnp.float32


def _dot(a, b):
    return jnp.dot(a, b, preferred_element_type=F32)


def _dot_nt(a, b):
    return lax.dot_general(a, b, (((1,), (1,)), ((), ())), preferred_element_type=F32)


def _dot_tn(a, b):
    return lax.dot_general(a, b, (((0,), (0,)), ((), ())), preferred_element_type=F32)


def _silu(x):
    return x * jax.nn.sigmoid(x)


def _adaln_kernel(c_ref, w_ref, b_ref, o_ref):
    o_ref[...] = _dot(_silu(c_ref[...]), w_ref[...]) + b_ref[...]


def _adaln(cc, w_ada, b_ada):
    rows = cc.shape[0]
    tn = D_MODEL
    return pl.pallas_call(
        _adaln_kernel,
        grid=(3 * D_MODEL // tn,),
        in_specs=[pl.BlockSpec((rows, D_MODEL), lambda j: (0, 0)),
                  pl.BlockSpec((D_MODEL, tn), lambda j: (0, j)),
                  pl.BlockSpec((1, tn), lambda j: (0, j))],
        out_specs=pl.BlockSpec((rows, tn), lambda j: (0, j)),
        out_shape=jax.ShapeDtypeStruct((rows, 3 * D_MODEL), F32),
        compiler_params=pltpu.CompilerParams(vmem_limit_bytes=VMEM_LIMIT),
        name="adaln",
    )(cc, w_ada, b_ada.reshape(1, -1))


def _head_rms(t, w):
    ms = jnp.mean(t * t, axis=-1, keepdims=True)
    return t * lax.rsqrt(ms + NORM_EPS) * w


def _rope(t, cos, sin_signed):
    return t * cos + pltpu.roll(t, ATT_HEAD_DIM // 2, 1) * sin_signed


def _inproj_rows_kernel(x_ref, nw_ref, sc_ref, sh_ref, w_ref, qw_ref, kw_ref, cos_ref, sin_ref,
                        o_ref, *, tn, rope, prep_q):
    x = x_ref[...]
    ms = jnp.mean(x * x, axis=-1, keepdims=True)
    y = x * lax.rsqrt(ms + NORM_EPS) * nw_ref[...]
    h = (y * (1.0 + sc_ref[0]) + sh_ref[0]).astype(BF16)
    n_cols = o_ref.shape[1]
    heads = [(OFF_ATT_K + g * ATT_HEAD_DIM, kw_ref[...]) for g in range(ATT_KV_HEADS)]
    if prep_q:
        qw = qw_ref[...] * (ATT_HEAD_DIM ** -0.5 * LOG2E)
        heads += [(OFF_ATT_Q + r * ATT_HEAD_DIM, qw) for r in range(ATT_HEADS)]
    for c0 in range(0, n_cols, tn):
        res = _dot(h, w_ref[:, c0:c0 + tn])
        o_ref[:, c0:c0 + tn] = res.astype(BF16)
        for col, w in heads:
            if c0 <= col < c0 + tn:
                t = _head_rms(res[:, col - c0:col - c0 + ATT_HEAD_DIM], w)
                if rope:
                    t = _rope(t, cos_ref[...], sin_ref[...])
                o_ref[:, col:col + ATT_HEAD_DIM] = t.astype(BF16)


def _inproj_rows(x2d, norm_w, scale, shift, w_bf, q_norm_w, k_norm_w, cos, sin_signed,
                 n_cols, rows_per_mod, tm, tn, rope):
    rows = x2d.shape[0]
    pos_tiles = cos.shape[0] // tm
    mod_map = lambda i: ((i * tm) // rows_per_mod, 0, 0)
    vec = lambda i: (0, 0)
    return pl.pallas_call(
        functools.partial(_inproj_rows_kernel, tn=tn, rope=rope, prep_q=n_cols > OFF_ATT_Q),
        grid=(rows // tm,),
        in_specs=[pl.BlockSpec((tm, D_MODEL), lambda i: (i, 0)),
                  pl.BlockSpec((1, D_MODEL), vec),
                  pl.BlockSpec((1, 1, D_MODEL), mod_map),
                  pl.BlockSpec((1, 1, D_MODEL), mod_map),
                  pl.BlockSpec((D_MODEL, n_cols), vec, pipeline_mode=pl.Buffered(1)),
                  pl.BlockSpec((1, ATT_HEAD_DIM), vec),
                  pl.BlockSpec((1, ATT_HEAD_DIM), vec),
                  pl.BlockSpec((tm, ATT_HEAD_DIM), lambda i: (i % pos_tiles, 0)),
                  pl.BlockSpec((tm, ATT_HEAD_DIM), lambda i: (i % pos_tiles, 0))],
        out_specs=pl.BlockSpec((tm, n_cols), lambda i: (i, 0)),
        out_shape=jax.ShapeDtypeStruct((rows, n_cols), BF16),
        compiler_params=pltpu.CompilerParams(
            dimension_semantics=("parallel",), vmem_limit_bytes=VMEM_LIMIT),
        name="inproj_rows",
    )(x2d, norm_w.reshape(1, -1), scale, shift, w_bf, q_norm_w.reshape(1, -1),
      k_norm_w.reshape(1, -1), cos, sin_signed)


def _vt_prep_kernel(vl_ref, vc_ref, vt_ref):
    s_len = vl_ref.shape[1]
    l_k = vt_ref.shape[3]
    for g in range(ATT_KV_HEADS):
        cols = slice(g * ATT_HEAD_DIM, (g + 1) * ATT_HEAD_DIM)
        vt_ref[0, g, :ATT_HEAD_DIM, :s_len] = vl_ref[0, :, cols].T
        vt_ref[0, g, :ATT_HEAD_DIM, s_len:] = vc_ref[0, :, cols].T
        vt_ref[0, g, ATT_HEAD_DIM:, :] = jnp.ones((VT_ROWS - ATT_HEAD_DIM, l_k), BF16)


def _vt_prep(px3, pc3):
    b, s_len, _ = px3.shape
    c_len = pc3.shape[1]
    l_k = s_len + c_len
    w = ATT_KV_W
    return pl.pallas_call(
        _vt_prep_kernel,
        grid=(b,),
        in_specs=[pl.BlockSpec((1, s_len, w), lambda i: (i, 0, OFF_ATT_V // w)),
                  pl.BlockSpec((1, c_len, w), lambda i: (i, 0, OFF_ATT_V // w))],
        out_specs=pl.BlockSpec((1, ATT_KV_HEADS, VT_ROWS, l_k), lambda i: (i, 0, 0, 0)),
        out_shape=jax.ShapeDtypeStruct((b, ATT_KV_HEADS, VT_ROWS, l_k), BF16),
        compiler_params=pltpu.CompilerParams(
            dimension_semantics=("parallel",), vmem_limit_bytes=VMEM_LIMIT),
        name="vt_prep",
    )(px3, pc3)


KEY_TILE = 256
ATT_UNIT = 512
ATT_TQ = 1024


def _attention_kernel(q_ref, kl_ref, kc_ref, vt_ref, g_ref, o_ref):
    k = jnp.concatenate([kl_ref[0], kc_ref[0]], axis=0)
    vt = vt_ref[0, 0]

    tq = q_ref.shape[1]
    units = [(r, c0) for r in range(ATT_REP) for c0 in range(0, tq, ATT_UNIT)]

    def logits(u):
        r, c0 = units[u]
        q = q_ref[0, c0:c0 + ATT_UNIT, r * ATT_HEAD_DIM:(r + 1) * ATT_HEAD_DIM]
        s_t = _dot_nt(k, q)
        return s_t, jnp.max(s_t, axis=0, keepdims=True)

    def finish(u, s_t, m, s_next):
        r, c0 = units[u]
        rows = slice(c0, c0 + ATT_UNIT)
        cols = slice(r * ATT_HEAD_DIM, (r + 1) * ATT_HEAD_DIM)
        if s_next is None:
            p_t = jnp.exp2(s_t - m).astype(BF16)
        else:
            tiles = []
            for j in range(0, s_t.shape[0], KEY_TILE):
                probe = pltpu.bitcast(s_next[j:j + 8, :], jnp.uint32)
                zero = ((probe >> 16) >> 16).astype(F32)[0:1, :]
                tiles.append(jnp.exp2(s_t[j:j + KEY_TILE, :] - (m + zero)).astype(BF16))
            p_t = jnp.concatenate(tiles, axis=0)
        o_t = _dot(vt, p_t)
        o = (o_t[:ATT_HEAD_DIM] * (1.0 / o_t[ATT_HEAD_DIM:ATT_HEAD_DIM + 1])).T
        o_ref[0, rows, cols] = (o * _silu(g_ref[0, rows, cols].astype(F32))).astype(BF16)

    pending = logits(0)
    for u in range(1, len(units)):
        nxt = logits(u)
        finish(u - 1, *pending, nxt[0])
        pending = nxt
    finish(len(units) - 1, *pending, None)


def _attention(px3, pc3, vt_all):
    b, s_len, _ = px3.shape
    c_len = pc3.shape[1]
    l_k = s_len + c_len
    gw = ATT_REP * ATT_HEAD_DIM
    tq = ATT_TQ
    k_col = lambda i, g, t: (i, 0, OFF_ATT_K // ATT_HEAD_DIM + g)
    return pl.pallas_call(
        _attention_kernel,
        grid=(b, ATT_KV_HEADS, s_len // tq),
        in_specs=[pl.BlockSpec((1, tq, gw), lambda i, g, t: (i, t, OFF_ATT_Q // gw + g)),
                  pl.BlockSpec((1, s_len, ATT_HEAD_DIM), k_col),
                  pl.BlockSpec((1, c_len, ATT_HEAD_DIM), k_col),
                  pl.BlockSpec((1, 1, VT_ROWS, l_k), lambda i, g, t: (i, g, 0, 0)),
                  pl.BlockSpec((1, tq, gw), lambda i, g, t: (i, t, OFF_ATT_G // gw + g))],
        out_specs=pl.BlockSpec((1, tq, gw), lambda i, g, t: (i, t, g)),
        out_shape=jax.ShapeDtypeStruct((b, s_len, ATT_Q_W), BF16),
        compiler_params=pltpu.CompilerParams(
            dimension_semantics=("parallel", "parallel", "arbitrary"),
            vmem_limit_bytes=VMEM_LIMIT),
        name="attention",
    )(px3, px3, pc3, vt_all, px3)


RET_HEADS_PER_STEP = 2


def _retention_head(d, q_ref, k_ref, v_ref, g_ref, kc_ref, vc_ref, o_ref, sb_ref, hh):
    s_len = q_ref.shape[1]
    c_len = kc_ref.shape[1]
    ch = RET_CHUNK
    n_chunks = s_len // ch
    qk_cols = slice(hh * RET_DK, (hh + 1) * RET_DK)
    v_cols = slice(hh * RET_DV, (hh + 1) * RET_DV)

    lg = jnp.log1p(-jnp.exp2(d))
    lg_f = lg[0:1, :]
    lg_b = lg[1:2, :]

    idx = lax.broadcasted_iota(jnp.int32, (ch, 1), 0).astype(F32)
    qd_f = (jnp.exp(lg_f * (idx + 1.0)) * (RET_DK ** -0.5)).astype(BF16)
    kd_f = jnp.exp(lg_f * (ch - 1.0 - idx)).astype(BF16)
    qd_b = (jnp.exp(lg_b * (ch - idx)) * (RET_DK ** -0.5)).astype(BF16)
    kd_b = jnp.exp(lg_b * idx).astype(BF16)
    cd_f = jnp.exp(lg_f * ch)
    cd_b = jnp.exp(lg_b * ch)

    row = lax.broadcasted_iota(jnp.int32, (ch, ch), 0)
    col = lax.broadcasted_iota(jnp.int32, (ch, ch), 1)
    rel = (row - col).astype(F32)
    mask = jnp.where(rel > 0, jnp.exp(lg_f * rel),
                     jnp.where(rel < 0, jnp.exp(-lg_b * rel), 2.0)) * (RET_DK ** -0.5)

    def chunk(n):
        return slice(n * ch, (n + 1) * ch)

    def q_at(n):
        return q_ref[0, chunk(n), qk_cols]

    def k_at(n):
        return k_ref[0, chunk(n), qk_cols]

    def v_at(n):
        return v_ref[0, chunk(n), v_cols]

    def masked_scores(n):
        return (_dot_nt(q_at(n), k_at(n)) * mask).astype(BF16)

    pos = lax.broadcasted_iota(jnp.int32, (c_len, 1), 0).astype(F32)
    kc = kc_ref[0, :, qk_cols].astype(F32)
    vc = vc_ref[0, :, v_cols]
    s_f = _dot_tn((kc * jnp.exp(lg_f * (c_len - 1.0 - pos))).astype(BF16), vc)
    s_b = _dot_tn((kc * jnp.exp(lg_b * pos)).astype(BF16), vc)

    for n in reversed(range(n_chunks)):
        sb_ref[hh, n] = s_b.astype(BF16)
        if n > 0:
            s_b = s_b * cd_b + _dot_tn(k_at(n) * kd_b, v_at(n))

    scores = masked_scores(0)
    for n in range(n_chunks):
        q = q_at(n)
        v = v_at(n)
        last = n + 1 == n_chunks
        next_scores = None if last else masked_scores(n + 1)
        kv_f = None if last else _dot_tn(k_at(n) * kd_f, v)
        o = _dot(q * qd_b, sb_ref[hh, n]) + _dot(q * qd_f, s_f.astype(BF16)) + _dot(scores, v)
        ms = jnp.mean(o * o, axis=-1, keepdims=True)
        o = (o * lax.rsqrt(ms + NORM_EPS)).astype(BF16)
        o_ref[0, chunk(n), v_cols] = o * _silu(g_ref[0, chunk(n), :])
        if not last:
            s_f = s_f * cd_f + kv_f
            scores = next_scores


def _retention_kernel(d_ref, q_ref, k_ref, v_ref, *rest):
    g_refs = rest[:RET_HEADS_PER_STEP]
    kc_ref, vc_ref, o_ref, sb_ref = rest[RET_HEADS_PER_STEP:]
    for hh in range(RET_HEADS_PER_STEP):
        _retention_head(d_ref[hh], q_ref, k_ref, v_ref, g_refs[hh], kc_ref, vc_ref, o_ref, sb_ref, hh)


def _retention(px3, pc3, decay3):
    b, s_len, _ = px3.shape
    c_len = pc3.shape[1]
    hp = RET_HEADS_PER_STEP
    qk_w, v_w = hp * RET_DK, hp * RET_DV
    gate = lambda hh: pl.BlockSpec((1, s_len, RET_DV), lambda i, h: (i, 0, OFF_RET_G // RET_DV + hp * h + hh))
    return pl.pallas_call(
        _retention_kernel,
        grid=(b, RET_HEADS // hp),
        in_specs=[pl.BlockSpec((hp, 2, 1), lambda i, h: (h, 0, 0)),
                  pl.BlockSpec((1, s_len, qk_w), lambda i, h: (i, 0, OFF_RET_Q // qk_w + h)),
                  pl.BlockSpec((1, s_len, qk_w), lambda i, h: (i, 0, OFF_RET_K // qk_w + h)),
                  pl.BlockSpec((1, s_len, v_w), lambda i, h: (i, 0, OFF_RET_V // v_w + h)),
                  *[gate(hh) for hh in range(hp)],
                  pl.BlockSpec((1, c_len, qk_w), lambda i, h: (i, 0, OFF_RET_K // qk_w + h)),
                  pl.BlockSpec((1, c_len, v_w), lambda i, h: (i, 0, OFF_RET_V // v_w + h))],
        out_specs=pl.BlockSpec((1, s_len, v_w), lambda i, h: (i, 0, h)),
        out_shape=jax.ShapeDtypeStruct((b, s_len, RET_V_W), BF16),
        scratch_shapes=[pltpu.VMEM((hp, s_len // RET_CHUNK, RET_DK, RET_DV), BF16)],
        compiler_params=pltpu.CompilerParams(
            dimension_semantics=("parallel", "arbitrary"), vmem_limit_bytes=VMEM_LIMIT),
        name="retention",
    )(decay3, px3, px3, px3, *[px3] * hp, pc3, pc3)


def _merge_kernel(x_ref, yr_ref, ya_ref, mr0_ref, mr1_ref, ma0_ref, ma1_ref, gate_ref,
                  wr_ref, wa_ref, wo_ref, o_ref):
    a = _dot(yr_ref[...], wr_ref[...])
    b = _dot(ya_ref[...], wa_ref[...])
    mg_ret = jnp.concatenate([mr0_ref[...], mr1_ref[...]], axis=-1).astype(F32)
    mg_att = jnp.concatenate([ma0_ref[...], ma1_ref[...]], axis=-1).astype(F32)
    y = jax.nn.sigmoid(mg_ret) * a + jax.nn.sigmoid(mg_att) * b
    out = _dot(y.astype(BF16), wo_ref[...])
    o_ref[...] = x_ref[...] + gate_ref[0] * out


def _merge(x2d, y_ret, y_att, px, gate, w_o_ret, w_o_att, w_out, rows_per_mod, tm):
    rows = x2d.shape[0]
    half = D_MODEL // 2
    weight = lambda shape: pl.BlockSpec(shape, lambda i: (0, 0), pipeline_mode=pl.Buffered(1))
    mg = lambda off: pl.BlockSpec((tm, half), lambda i: (i, off // half))
    return pl.pallas_call(
        _merge_kernel,
        grid=(rows // tm,),
        in_specs=[pl.BlockSpec((tm, D_MODEL), lambda i: (i, 0)),
                  pl.BlockSpec((tm, RET_V_W), lambda i: (i, 0)),
                  pl.BlockSpec((tm, ATT_Q_W), lambda i: (i, 0)),
                  mg(OFF_MG_RET), mg(OFF_MG_RET + half), mg(OFF_MG_ATT), mg(OFF_MG_ATT + half),
                  pl.BlockSpec((1, 1, D_MODEL), lambda i: ((i * tm) // rows_per_mod, 0, 0)),
                  weight((RET_V_W, D_MODEL)), weight((ATT_Q_W, D_MODEL)), weight((D_MODEL, D_MODEL))],
        out_specs=pl.BlockSpec((tm, D_MODEL), lambda i: (i, 0)),
        out_shape=jax.ShapeDtypeStruct((rows, D_MODEL), F32),
        compiler_params=pltpu.CompilerParams(
            dimension_semantics=("parallel",), vmem_limit_bytes=VMEM_LIMIT),
        name="merge",
    )(x2d, y_ret, y_att, px, px, px, px, gate, w_o_ret, w_o_att, w_out)


def _rope_tables(s_len):
    rows = s_len // GRID_W
    row = jnp.repeat(jnp.arange(rows, dtype=F32), GRID_W)
    col = jnp.tile(jnp.arange(GRID_W, dtype=F32), rows)
    half = ATT_HEAD_DIM // 2
    freqs = ROPE_THETA ** (-jnp.arange(0, half, 2, dtype=F32) / half)
    ang = jnp.concatenate([row[:, None] * freqs, col[:, None] * freqs], axis=-1)
    cos, sin = jnp.cos(ang), jnp.sin(ang)
    return jnp.concatenate([cos, cos], axis=-1), jnp.concatenate([-sin, sin], axis=-1)


def _split_pairs(t):
    lead = t.shape[:-1]
    t = t.reshape(*lead, -1, ATT_HEAD_DIM // 2, 2)
    return jnp.swapaxes(t, -1, -2).reshape(*lead, -1)


def _layer(x, ctx, c, c_ctx, norm_w, w_ada, b_ada, w_in, ret_log2_decay,
           q_norm_w, k_norm_w, w_o_ret, w_o_att, w_out):
    b, s_len, _ = x.shape
    c_len = ctx.shape[1]

    mod = _adaln(jnp.concatenate([c, c_ctx[None]], axis=0), w_ada, b_ada)
    shift, scale, gate = (mod[:, i * D_MODEL:(i + 1) * D_MODEL] for i in range(3))
    lat = lambda t: t[:b].reshape(b, 1, D_MODEL)
    cx = lambda t: t[b:].reshape(1, 1, D_MODEL)

    w_bf = w_in.astype(BF16)
    for off, width in ((OFF_ATT_K, ATT_KV_W), (OFF_ATT_Q, ATT_Q_W)):
        w_bf = w_bf.at[:, off:off + width].set(_split_pairs(w_bf[:, off:off + width]))
    q_norm_w = _split_pairs(q_norm_w)
    k_norm_w = _split_pairs(k_norm_w)

    cos, sin_signed = _rope_tables(s_len)
    x2d = x.reshape(b * s_len, D_MODEL)
    px = _inproj_rows(x2d, norm_w, lat(scale), lat(shift), w_bf, q_norm_w, k_norm_w, cos, sin_signed,
                      IN_COLS, s_len, PROJ_TM, PROJ_TN, rope=True)
    pc = _inproj_rows(ctx.reshape(b * c_len, D_MODEL), norm_w, cx(scale), cx(shift), w_bf,
                      q_norm_w, k_norm_w, cos, sin_signed,
                      KV_COLS, b * c_len, PROJ_TM, PROJ_TN_CTX, rope=False)
    px3 = px.reshape(b, s_len, IN_COLS)
    pc3 = pc.reshape(b, c_len, KV_COLS)

    y_att = _attention(px3, pc3, _vt_prep(px3, pc3))

    decay3 = ret_log2_decay.astype(F32).T.reshape(RET_HEADS, 2, 1)
    y_ret = _retention(px3, pc3, decay3)

    x_new = _merge(x2d, y_ret.reshape(b * s_len, RET_V_W), y_att.reshape(b * s_len, ATT_Q_W), px,
                   lat(gate), w_o_ret.astype(BF16), w_o_att.astype(BF16), w_out.astype(BF16),
                   s_len, MERGE_TM)
    return x_new.reshape(b, s_len, D_MODEL)


def kernel(x, c, ctx, c_ctx, norm_w, w_ada, b_ada, w_in, ret_log2_decay, q_norm_w, k_norm_w,
           w_o_ret, w_o_att, w_out):
    depth = norm_w.shape[0]
    assert depth == 1, "context update between layers is not implemented"
    return _layer(x, ctx, c, c_ctx, norm_w[0], w_ada[0], b_ada[0], w_in[0], ret_log2_decay[0],
                  q_norm_w[0], k_norm_w[0], w_o_ret[0], w_o_att[0], w_out[0])
```

```python
import functools

import jax
import jax.numpy as jnp
from jax import lax
from jax.experimental import pallas as pl
from jax.experimental.pallas import tpu as pltpu

D_MODEL = 1024
GRID_W = 64
RET_HEADS = 4
RET_DK = 256
RET_DV = 512
RET_QK_W = RET_HEADS * RET_DK
RET_V_W = RET_HEADS * RET_DV
ATT_HEADS = 8
ATT_KV_HEADS = 2
ATT_HEAD_DIM = 128
ATT_Q_W = ATT_HEADS * ATT_HEAD_DIM
ATT_KV_W = ATT_KV_HEADS * ATT_HEAD_DIM
ATT_REP = ATT_HEADS // ATT_KV_HEADS
ROPE_THETA = 10000.0
NORM_EPS = 1e-6
KV_COLS = RET_QK_W + RET_V_W + 2 * ATT_KV_W
LOG2E = 1.4426950408889634

OFF_RET_K = 0
OFF_RET_V = OFF_RET_K + RET_QK_W
OFF_ATT_K = OFF_RET_V + RET_V_W
OFF_ATT_V = OFF_ATT_K + ATT_KV_W
OFF_RET_Q = OFF_ATT_V + ATT_KV_W
OFF_RET_G = OFF_RET_Q + RET_QK_W
OFF_ATT_Q = OFF_RET_G + RET_V_W
OFF_ATT_G = OFF_ATT_Q + ATT_Q_W
OFF_MG_RET = OFF_ATT_G + ATT_Q_W
OFF_MG_ATT = OFF_MG_RET + D_MODEL
IN_COLS = OFF_MG_ATT + D_MODEL
assert OFF_RET_Q == KV_COLS

RET_CHUNK = 256
PROJ_TM = 512
PROJ_TN = 1536
PROJ_TN_CTX = 1792
MERGE_TM = 1024
BF16_SUBLANES = 16
VT_ROWS = ATT_HEAD_DIM + BF16_SUBLANES
VMEM_LIMIT = 56 * 1024 * 1024

BF16 = jnp.bfloat16
F32 = jnp.float32


def _dot(a, b):
    return jnp.dot(a, b, preferred_element_type=F32)


def _dot_nt(a, b):
    return lax.dot_general(a, b, (((1,), (1,)), ((), ())), preferred_element_type=F32)


def _dot_tn(a, b):
    return lax.dot_general(a, b, (((0,), (0,)), ((), ())), preferred_element_type=F32)


def _silu(x):
    return x * jax.nn.sigmoid(x)


def _adaln_kernel(c_ref, w_ref, b_ref, o_ref):
    o_ref[...] = _dot(_silu(c_ref[...]), w_ref[...]) + b_ref[...]


def _adaln(cc, w_ada, b_ada):
    rows = cc.shape[0]
    tn = D_MODEL
    return pl.pallas_call(
        _adaln_kernel,
        grid=(3 * D_MODEL // tn,),
        in_specs=[pl.BlockSpec((rows, D_MODEL), lambda j: (0, 0)),
                  pl.BlockSpec((D_MODEL, tn), lambda j: (0, j)),
                  pl.BlockSpec((1, tn), lambda j: (0, j))],
        out_specs=pl.BlockSpec((rows, tn), lambda j: (0, j)),
        out_shape=jax.ShapeDtypeStruct((rows, 3 * D_MODEL), F32),
        compiler_params=pltpu.CompilerParams(vmem_limit_bytes=VMEM_LIMIT),
        name="adaln",
    )(cc, w_ada, b_ada.reshape(1, -1))


def _head_rms(t, w):
    ms = jnp.mean(t * t, axis=-1, keepdims=True)
    return t * lax.rsqrt(ms + NORM_EPS) * w


def _rope(t, cos, sin_signed):
    return t * cos + pltpu.roll(t, ATT_HEAD_DIM // 2, 1) * sin_signed


def _inproj_rows_kernel(x_ref, nw_ref, sc_ref, sh_ref, w_ref, qw_ref, kw_ref, cos_ref, sin_ref,
                        o_ref, *, tn, rope, prep_q):
    x = x_ref[...]
    ms = jnp.mean(x * x, axis=-1, keepdims=True)
    y = x * lax.rsqrt(ms + NORM_EPS) * nw_ref[...]
    h = (y * (1.0 + sc_ref[0]) + sh_ref[0]).astype(BF16)
    n_cols = o_ref.shape[1]
    heads = [(OFF_ATT_K + g * ATT_HEAD_DIM, kw_ref[...]) for g in range(ATT_KV_HEADS)]
    if prep_q:
        qw = qw_ref[...] * (ATT_HEAD_DIM ** -0.5 * LOG2E)
        heads += [(OFF_ATT_Q + r * ATT_HEAD_DIM, qw) for r in range(ATT_HEADS)]
    for c0 in range(0, n_cols, tn):
        res = _dot(h, w_ref[:, c0:c0 + tn])
        o_ref[:, c0:c0 + tn] = res.astype(BF16)
        for col, w in heads:
            if c0 <= col < c0 + tn:
                t = _head_rms(res[:, col - c0:col - c0 + ATT_HEAD_DIM], w)
                if rope:
                    t = _rope(t, cos_ref[...], sin_ref[...])
                o_ref[:, col:col + ATT_HEAD_DIM] = t.astype(BF16)


def _inproj_rows(x2d, norm_w, scale, shift, w_bf, q_norm_w, k_norm_w, cos, sin_signed,
                 n_cols, rows_per_mod, tm, tn, rope):
    rows = x2d.shape[0]
    pos_tiles = cos.shape[0] // tm
    mod_map = lambda i: ((i * tm) // rows_per_mod, 0, 0)
    vec = lambda i: (0, 0)
    return pl.pallas_call(
        functools.partial(_inproj_rows_kernel, tn=tn, rope=rope, prep_q=n_cols > OFF_ATT_Q),
        grid=(rows // tm,),
        in_specs=[pl.BlockSpec((tm, D_MODEL), lambda i: (i, 0)),
                  pl.BlockSpec((1, D_MODEL), vec),
                  pl.BlockSpec((1, 1, D_MODEL), mod_map),
                  pl.BlockSpec((1, 1, D_MODEL), mod_map),
                  pl.BlockSpec((D_MODEL, n_cols), vec, pipeline_mode=pl.Buffered(1)),
                  pl.BlockSpec((1, ATT_HEAD_DIM), vec),
                  pl.BlockSpec((1, ATT_HEAD_DIM), vec),
                  pl.BlockSpec((tm, ATT_HEAD_DIM), lambda i: (i % pos_tiles, 0)),
                  pl.BlockSpec((tm, ATT_HEAD_DIM), lambda i: (i % pos_tiles, 0))],
        out_specs=pl.BlockSpec((tm, n_cols), lambda i: (i, 0)),
        out_shape=jax.ShapeDtypeStruct((rows, n_cols), BF16),
        compiler_params=pltpu.CompilerParams(
            dimension_semantics=("parallel",), vmem_limit_bytes=VMEM_LIMIT),
        name="inproj_rows",
    )(x2d, norm_w.reshape(1, -1), scale, shift, w_bf, q_norm_w.reshape(1, -1),
      k_norm_w.reshape(1, -1), cos, sin_signed)


def _vt_prep_kernel(vl_ref, vc_ref, vt_ref):
    s_len = vl_ref.shape[1]
    l_k = vt_ref.shape[3]
    for g in range(ATT_KV_HEADS):
        cols = slice(g * ATT_HEAD_DIM, (g + 1) * ATT_HEAD_DIM)
        vt_ref[0, g, :ATT_HEAD_DIM, :s_len] = vl_ref[0, :, cols].T
        vt_ref[0, g, :ATT_HEAD_DIM, s_len:] = vc_ref[0, :, cols].T
        vt_ref[0, g, ATT_HEAD_DIM:, :] = jnp.ones((VT_ROWS - ATT_HEAD_DIM, l_k), BF16)


def _vt_prep(px3, pc3):
    b, s_len, _ = px3.shape
    c_len = pc3.shape[1]
    l_k = s_len + c_len
    w = ATT_KV_W
    return pl.pallas_call(
        _vt_prep_kernel,
        grid=(b,),
        in_specs=[pl.BlockSpec((1, s_len, w), lambda i: (i, 0, OFF_ATT_V // w)),
                  pl.BlockSpec((1, c_len, w), lambda i: (i, 0, OFF_ATT_V // w))],
        out_specs=pl.BlockSpec((1, ATT_KV_HEADS, VT_ROWS, l_k), lambda i: (i, 0, 0, 0)),
        out_shape=jax.ShapeDtypeStruct((b, ATT_KV_HEADS, VT_ROWS, l_k), BF16),
        compiler_params=pltpu.CompilerParams(
            dimension_semantics=("parallel",), vmem_limit_bytes=VMEM_LIMIT),
        name="vt_prep",
    )(px3, pc3)


KEY_TILE = 256
ATT_UNIT = 512
ATT_TQ = 1024


def _attention_kernel(q_ref, kl_ref, kc_ref, vt_ref, g_ref, o_ref):
    k = jnp.concatenate([kl_ref[0], kc_ref[0]], axis=0)
    vt = vt_ref[0, 0]

    tq = q_ref.shape[1]
    units = [(r, c0) for r in range(ATT_REP) for c0 in range(0, tq, ATT_UNIT)]

    def logits(u):
        r, c0 = units[u]
        q = q_ref[0, c0:c0 + ATT_UNIT, r * ATT_HEAD_DIM:(r + 1) * ATT_HEAD_DIM]
        s_t = _dot_nt(k, q)
        return s_t, jnp.max(s_t, axis=0, keepdims=True)

    def finish(u, s_t, m, s_next):
        r, c0 = units[u]
        rows = slice(c0, c0 + ATT_UNIT)
        cols = slice(r * ATT_HEAD_DIM, (r + 1) * ATT_HEAD_DIM)
        if s_next is None:
            p_t = jnp.exp2(s_t - m).astype(BF16)
        else:
            tiles = []
            for j in range(0, s_t.shape[0], KEY_TILE):
                probe = pltpu.bitcast(s_next[j:j + 8, :], jnp.uint32)
                zero = ((probe >> 16) >> 16).astype(F32)[0:1, :]
                tiles.append(jnp.exp2(s_t[j:j + KEY_TILE, :] - (m + zero)).astype(BF16))
            p_t = jnp.concatenate(tiles, axis=0)
        o_t = _dot(vt, p_t)
        o = (o_t[:ATT_HEAD_DIM] * (1.0 / o_t[ATT_HEAD_DIM:ATT_HEAD_DIM + 1])).T
        o_ref[0, rows, cols] = (o * _silu(g_ref[0, rows, cols].astype(F32))).astype(BF16)

    pending = logits(0)
    for u in range(1, len(units)):
        nxt = logits(u)
        finish(u - 1, *pending, nxt[0])
        pending = nxt
    finish(len(units) - 1, *pending, None)


def _attention(px3, pc3, vt_all):
    b, s_len, _ = px3.shape
    c_len = pc3.shape[1]
    l_k = s_len + c_len
    gw = ATT_REP * ATT_HEAD_DIM
    tq = ATT_TQ
    k_col = lambda i, g, t: (i, 0, OFF_ATT_K // ATT_HEAD_DIM + g)
    return pl.pallas_call(
        _attention_kernel,
        grid=(b, ATT_KV_HEADS, s_len // tq),
        in_specs=[pl.BlockSpec((1, tq, gw), lambda i, g, t: (i, t, OFF_ATT_Q // gw + g)),
                  pl.BlockSpec((1, s_len, ATT_HEAD_DIM), k_col),
                  pl.BlockSpec((1, c_len, ATT_HEAD_DIM), k_col),
                  pl.BlockSpec((1, 1, VT_ROWS, l_k), lambda i, g, t: (i, g, 0, 0)),
                  pl.BlockSpec((1, tq, gw), lambda i, g, t: (i, t, OFF_ATT_G // gw + g))],
        out_specs=pl.BlockSpec((1, tq, gw), lambda i, g, t: (i, t, g)),
        out_shape=jax.ShapeDtypeStruct((b, s_len, ATT_Q_W), BF16),
        compiler_params=pltpu.CompilerParams(
            dimension_semantics=("parallel", "parallel", "arbitrary"),
            vmem_limit_bytes=VMEM_LIMIT),
        name="attention",
    )(px3, px3, pc3, vt_all, px3)


def _retention_kernel(d_ref, q_ref, k_ref, v_ref, g_ref, kc_ref, vc_ref, o_ref, sb_ref):
    s_len = q_ref.shape[1]
    c_len = kc_ref.shape[1]
    ch = RET_CHUNK
    n_chunks = s_len // ch

    d = d_ref[0]
    lg = jnp.log1p(-jnp.exp2(d))
    lg_f = lg[0:1, :]
    lg_b = lg[1:2, :]

    idx = lax.broadcasted_iota(jnp.int32, (ch, 1), 0).astype(F32)
    qd_f = (jnp.exp(lg_f * (idx + 1.0)) * (RET_DK ** -0.5)).astype(BF16)
    kd_f = jnp.exp(lg_f * (ch - 1.0 - idx)).astype(BF16)
    qd_b = (jnp.exp(lg_b * (ch - idx)) * (RET_DK ** -0.5)).astype(BF16)
    kd_b = jnp.exp(lg_b * idx).astype(BF16)
    cd_f = jnp.exp(lg_f * ch)
    cd_b = jnp.exp(lg_b * ch)

    row = lax.broadcasted_iota(jnp.int32, (ch, ch), 0)
    col = lax.broadcasted_iota(jnp.int32, (ch, ch), 1)
    rel = (row - col).astype(F32)
    mask = jnp.where(rel > 0, jnp.exp(lg_f * rel),
                     jnp.where(rel < 0, jnp.exp(-lg_b * rel), 2.0)) * (RET_DK ** -0.5)

    def chunk(n):
        return slice(n * ch, (n + 1) * ch)

    def masked_scores(n):
        return (_dot_nt(q_ref[0, chunk(n), :], k_ref[0, chunk(n), :]) * mask).astype(BF16)

    pos = lax.broadcasted_iota(jnp.int32, (c_len, 1), 0).astype(F32)
    kc = kc_ref[0].astype(F32)
    vc = vc_ref[0]
    s_f = _dot_tn((kc * jnp.exp(lg_f * (c_len - 1.0 - pos))).astype(BF16), vc)
    s_b = _dot_tn((kc * jnp.exp(lg_b * pos)).astype(BF16), vc)

    for n in reversed(range(n_chunks)):
        sb_ref[n] = s_b.astype(BF16)
        if n > 0:
            s_b = s_b * cd_b + _dot_tn(k_ref[0, chunk(n), :] * kd_b, v_ref[0, chunk(n), :])

    scores = masked_scores(0)
    for n in range(n_chunks):
        q = q_ref[0, chunk(n), :]
        v = v_ref[0, chunk(n), :]
        last = n + 1 == n_chunks
        next_scores = None if last else masked_scores(n + 1)
        kv_f = None if last else _dot_tn(k_ref[0, chunk(n), :] * kd_f, v)
        o = _dot(q * qd_b, sb_ref[n]) + _dot(q * qd_f, s_f.astype(BF16)) + _dot(scores, v)
        ms = jnp.mean(o * o, axis=-1, keepdims=True)
        o = (o * lax.rsqrt(ms + NORM_EPS)).astype(BF16)
        o_ref[0, chunk(n), :] = o * _silu(g_ref[0, chunk(n), :])
        if not last:
            s_f = s_f * cd_f + kv_f
            scores = next_scores


def _retention(px3, pc3, decay3):
    b, s_len, _ = px3.shape
    c_len = pc3.shape[1]
    return pl.pallas_call(
        _retention_kernel,
        grid=(b, RET_HEADS),
        in_specs=[pl.BlockSpec((1, 2, 1), lambda i, h: (h, 0, 0)),
                  pl.BlockSpec((1, s_len, RET_DK), lambda i, h: (i, 0, OFF_RET_Q // RET_DK + h)),
                  pl.BlockSpec((1, s_len, RET_DK), lambda i, h: (i, 0, OFF_RET_K // RET_DK + h)),
                  pl.BlockSpec((1, s_len, RET_DV), lambda i, h: (i, 0, OFF_RET_V // RET_DV + h)),
                  pl.BlockSpec((1, s_len, RET_DV), lambda i, h: (i, 0, OFF_RET_G // RET_DV + h)),
                  pl.BlockSpec((1, c_len, RET_DK), lambda i, h: (i, 0, h)),
                  pl.BlockSpec((1, c_len, RET_DV), lambda i, h: (i, 0, RET_QK_W // RET_DV + h))],
        out_specs=pl.BlockSpec((1, s_len, RET_DV), lambda i, h: (i, 0, h)),
        out_shape=jax.ShapeDtypeStruct((b, s_len, RET_V_W), BF16),
        scratch_shapes=[pltpu.VMEM((s_len // RET_CHUNK, RET_DK, RET_DV), BF16)],
        compiler_params=pltpu.CompilerParams(
            dimension_semantics=("parallel", "arbitrary"), vmem_limit_bytes=VMEM_LIMIT),
        name="retention",
    )(decay3, px3, px3, px3, px3, pc3, pc3)


def _merge_kernel(x_ref, yr_ref, ya_ref, mr0_ref, mr1_ref, ma0_ref, ma1_ref, gate_ref,
                  wr_ref, wa_ref, wo_ref, o_ref):
    a = _dot(yr_ref[...], wr_ref[...])
    b = _dot(ya_ref[...], wa_ref[...])
    mg_ret = jnp.concatenate([mr0_ref[...], mr1_ref[...]], axis=-1).astype(F32)
    mg_att = jnp.concatenate([ma0_ref[...], ma1_ref[...]], axis=-1).astype(F32)
    y = jax.nn.sigmoid(mg_ret) * a + jax.nn.sigmoid(mg_att) * b
    out = _dot(y.astype(BF16), wo_ref[...])
    o_ref[...] = x_ref[...] + gate_ref[0] * out


def _merge(x2d, y_ret, y_att, px, gate, w_o_ret, w_o_att, w_out, rows_per_mod, tm):
    rows = x2d.shape[0]
    half = D_MODEL // 2
    weight = lambda shape: pl.BlockSpec(shape, lambda i: (0, 0), pipeline_mode=pl.Buffered(1))
    mg = lambda off: pl.BlockSpec((tm, half), lambda i: (i, off // half))
    return pl.pallas_call(
        _merge_kernel,
        grid=(rows // tm,),
        in_specs=[pl.BlockSpec((tm, D_MODEL), lambda i: (i, 0)),
                  pl.BlockSpec((tm, RET_V_W), lambda i: (i, 0)),
                  pl.BlockSpec((tm, ATT_Q_W), lambda i: (i, 0)),
                  mg(OFF_MG_RET), mg(OFF_MG_RET + half), mg(OFF_MG_ATT), mg(OFF_MG_ATT + half),
                  pl.BlockSpec((1, 1, D_MODEL), lambda i: ((i * tm) // rows_per_mod, 0, 0)),
                  weight((RET_V_W, D_MODEL)), weight((ATT_Q_W, D_MODEL)), weight((D_MODEL, D_MODEL))],
        out_specs=pl.BlockSpec((tm, D_MODEL), lambda i: (i, 0)),
        out_shape=jax.ShapeDtypeStruct((rows, D_MODEL), F32),
        compiler_params=pltpu.CompilerParams(
            dimension_semantics=("parallel",), vmem_limit_bytes=VMEM_LIMIT),
        name="merge",
    )(x2d, y_ret, y_att, px, px, px, px, gate, w_o_ret, w_o_att, w_out)


def _rope_tables(s_len):
    rows = s_len // GRID_W
    row = jnp.repeat(jnp.arange(rows, dtype=F32), GRID_W)
    col = jnp.tile(jnp.arange(GRID_W, dtype=F32), rows)
    half = ATT_HEAD_DIM // 2
    freqs = ROPE_THETA ** (-jnp.arange(0, half, 2, dtype=F32) / half)
    ang = jnp.concatenate([row[:, None] * freqs, col[:, None] * freqs], axis=-1)
    cos, sin = jnp.cos(ang), jnp.sin(ang)
    return jnp.concatenate([cos, cos], axis=-1), jnp.concatenate([-sin, sin], axis=-1)


def _split_pairs(t):
    lead = t.shape[:-1]
    t = t.reshape(*lead, -1, ATT_HEAD_DIM // 2, 2)
    return jnp.swapaxes(t, -1, -2).reshape(*lead, -1)


def _layer(x, ctx, c, c_ctx, norm_w, w_ada, b_ada, w_in, ret_log2_decay,
           q_norm_w, k_norm_w, w_o_ret, w_o_att, w_out):
    b, s_len, _ = x.shape
    c_len = ctx.shape[1]

    mod = _adaln(jnp.concatenate([c, c_ctx[None]], axis=0), w_ada, b_ada)
    shift, scale, gate = (mod[:, i * D_MODEL:(i + 1) * D_MODEL] for i in range(3))
    lat = lambda t: t[:b].reshape(b, 1, D_MODEL)
    cx = lambda t: t[b:].reshape(1, 1, D_MODEL)

    w_bf = w_in.astype(BF16)
    for off, width in ((OFF_ATT_K, ATT_KV_W), (OFF_ATT_Q, ATT_Q_W)):
        w_bf = w_bf.at[:, off:off + width].set(_split_pairs(w_bf[:, off:off + width]))
    q_norm_w = _split_pairs(q_norm_w)
    k_norm_w = _split_pairs(k_norm_w)

    cos, sin_signed = _rope_tables(s_len)
    x2d = x.reshape(b * s_len, D_MODEL)
    px = _inproj_rows(x2d, norm_w, lat(scale), lat(shift), w_bf, q_norm_w, k_norm_w, cos, sin_signed,
                      IN_COLS, s_len, PROJ_TM, PROJ_TN, rope=True)
    pc = _inproj_rows(ctx.reshape(b * c_len, D_MODEL), norm_w, cx(scale), cx(shift), w_bf,
                      q_norm_w, k_norm_w, cos, sin_signed,
                      KV_COLS, b * c_len, PROJ_TM, PROJ_TN_CTX, rope=False)
    px3 = px.reshape(b, s_len, IN_COLS)
    pc3 = pc.reshape(b, c_len, KV_COLS)

    y_att = _attention(px3, pc3, _vt_prep(px3, pc3))

    decay3 = ret_log2_decay.astype(F32).T.reshape(RET_HEADS, 2, 1)
    y_ret = _retention(px3, pc3, decay3)

    x_new = _merge(x2d, y_ret.reshape(b * s_len, RET_V_W), y_att.reshape(b * s_len, ATT_Q_W), px,
                   lat(gate), w_o_ret.astype(BF16), w_o_att.astype(BF16), w_out.astype(BF16),
                   s_len, MERGE_TM)
    return x_new.reshape(b, s_len, D_MODEL)


def kernel(x, c, ctx, c_ctx, norm_w, w_ada, b_ada, w_in, ret_log2_decay, q_norm_w, k_norm_w,
           w_o_ret, w_o_att, w_out):
    depth = norm_w.shape[0]
    assert depth == 1, "context update between layers is not implemented"
    return _layer(x, ctx, c, c_ctx, norm_w[0], w_ada[0], b_ada[0], w_in[0], ret_log2_decay[0],
                  q_norm_w[0], k_norm_w[0], w_o_ret[0], w_o_att[0], w_out[0])
```

```python
import functools

import jax
import jax.numpy as jnp
from jax import lax
from jax.experimental import pallas as pl
from jax.experimental.pallas import tpu as pltpu

D_MODEL = 1024
GRID_W = 64
RET_HEADS = 4
RET_DK = 256
RET_DV = 512
RET_QK_W = RET_HEADS * RET_DK
RET_V_W = RET_HEADS * RET_DV
ATT_HEADS = 8
ATT_KV_HEADS = 2
ATT_HEAD_DIM = 128
ATT_Q_W = ATT_HEADS * ATT_HEAD_DIM
ATT_KV_W = ATT_KV_HEADS * ATT_HEAD_DIM
ATT_REP = ATT_HEADS // ATT_KV_HEADS
ROPE_THETA = 10000.0
NORM_EPS = 1e-6
KV_COLS = RET_QK_W + RET_V_W + 2 * ATT_KV_W
LOG2E = 1.4426950408889634

OFF_RET_K = 0
OFF_RET_V = OFF_RET_K + RET_QK_W
OFF_ATT_K = OFF_RET_V + RET_V_W
OFF_ATT_V = OFF_ATT_K + ATT_KV_W
OFF_RET_Q = OFF_ATT_V + ATT_KV_W
OFF_RET_G = OFF_RET_Q + RET_QK_W
OFF_ATT_Q = OFF_RET_G + RET_V_W
OFF_ATT_G = OFF_ATT_Q + ATT_Q_W
OFF_MG_RET = OFF_ATT_G + ATT_Q_W
OFF_MG_ATT = OFF_MG_RET + D_MODEL
IN_COLS = OFF_MG_ATT + D_MODEL
assert OFF_RET_Q == KV_COLS

RET_CHUNK = 256
PROJ_TM = 512
PROJ_TN = 1536
PROJ_TN_CTX = 1792
MERGE_TM = 1024
BF16_SUBLANES = 16
VT_ROWS = ATT_HEAD_DIM + BF16_SUBLANES
VMEM_LIMIT = 56 * 1024 * 1024

BF16 = jnp.bfloat16
F32 = jnp.float32


def _dot(a, b):
    return jnp.dot(a, b, preferred_element_type=F32)


def _dot_nt(a, b):
    return lax.dot_general(a, b, (((1,), (1,)), ((), ())), preferred_element_type=F32)


def _dot_tn(a, b):
    return lax.dot_general(a, b, (((0,), (0,)), ((), ())), preferred_element_type=F32)


def _silu(x):
    return x * jax.nn.sigmoid(x)


def _adaln_kernel(c_ref, w_ref, b_ref, o_ref):
    o_ref[...] = _dot(_silu(c_ref[...]), w_ref[...]) + b_ref[...]


def _adaln(cc, w_ada, b_ada):
    rows = cc.shape[0]
    tn = D_MODEL
    return pl.pallas_call(
        _adaln_kernel,
        grid=(3 * D_MODEL // tn,),
        in_specs=[pl.BlockSpec((rows, D_MODEL), lambda j: (0, 0)),
                  pl.BlockSpec((D_MODEL, tn), lambda j: (0, j)),
                  pl.BlockSpec((1, tn), lambda j: (0, j))],
        out_specs=pl.BlockSpec((rows, tn), lambda j: (0, j)),
        out_shape=jax.ShapeDtypeStruct((rows, 3 * D_MODEL), F32),
        compiler_params=pltpu.CompilerParams(vmem_limit_bytes=VMEM_LIMIT),
        name="adaln",
    )(cc, w_ada, b_ada.reshape(1, -1))


def _head_rms(t, w):
    ms = jnp.mean(t * t, axis=-1, keepdims=True)
    return t * lax.rsqrt(ms + NORM_EPS) * w


def _rope(t, cos, sin_signed):
    return t * cos + pltpu.roll(t, ATT_HEAD_DIM // 2, 1) * sin_signed


def _inproj_rows_kernel(x_ref, nw_ref, sc_ref, sh_ref, w_ref, qw_ref, kw_ref, cos_ref, sin_ref,
                        o_ref, vt_ref, *, tn, rope, prep_q):
    x = x_ref[...]
    ms = jnp.mean(x * x, axis=-1, keepdims=True)
    y = x * lax.rsqrt(ms + NORM_EPS) * nw_ref[...]
    h = (y * (1.0 + sc_ref[0]) + sh_ref[0]).astype(BF16)
    n_cols = o_ref.shape[1]
    heads = [(OFF_ATT_K + g * ATT_HEAD_DIM, kw_ref[...]) for g in range(ATT_KV_HEADS)]
    if prep_q:
        qw = qw_ref[...] * (ATT_HEAD_DIM ** -0.5 * LOG2E)
        heads += [(OFF_ATT_Q + r * ATT_HEAD_DIM, qw) for r in range(ATT_HEADS)]
    for c0 in range(0, n_cols, tn):
        res = _dot(h, w_ref[:, c0:c0 + tn])
        o_ref[:, c0:c0 + tn] = res.astype(BF16)
        for col, w in heads:
            if c0 <= col < c0 + tn:
                t = _head_rms(res[:, col - c0:col - c0 + ATT_HEAD_DIM], w)
                if rope:
                    t = _rope(t, cos_ref[...], sin_ref[...])
                o_ref[:, col:col + ATT_HEAD_DIM] = t.astype(BF16)
        for g in range(ATT_KV_HEADS):
            col = OFF_ATT_V + g * ATT_HEAD_DIM
            if c0 <= col < c0 + tn:
                n_b, _, _, n_tok = vt_ref.shape
                v = res[:, col - c0:col - c0 + ATT_HEAD_DIM]
                for bb in range(n_b):
                    vt_ref[bb, g, :ATT_HEAD_DIM, :] = v[bb * n_tok:(bb + 1) * n_tok].T.astype(BF16)
                    vt_ref[bb, g, ATT_HEAD_DIM:, :] = jnp.ones((VT_ROWS - ATT_HEAD_DIM, n_tok), BF16)


def _inproj_rows(x2d, norm_w, scale, shift, w_bf, q_norm_w, k_norm_w, cos, sin_signed,
                 n_cols, rows_per_mod, tokens, tm, tn, rope):
    rows = x2d.shape[0]
    if tokens >= tm:
        per_batch = tokens // tm
        vt_block = (1, ATT_KV_HEADS, VT_ROWS, tm)
        vt_map = lambda i: (i // per_batch, 0, 0, i % per_batch)
    else:
        vt_block = (tm // tokens, ATT_KV_HEADS, VT_ROWS, tokens)
        vt_map = lambda i: (i, 0, 0, 0)
    pos_tiles = cos.shape[0] // tm
    mod_map = lambda i: ((i * tm) // rows_per_mod, 0, 0)
    vec = lambda i: (0, 0)
    return pl.pallas_call(
        functools.partial(_inproj_rows_kernel, tn=tn, rope=rope, prep_q=n_cols > OFF_ATT_Q),
        grid=(rows // tm,),
        in_specs=[pl.BlockSpec((tm, D_MODEL), lambda i: (i, 0)),
                  pl.BlockSpec((1, D_MODEL), vec),
                  pl.BlockSpec((1, 1, D_MODEL), mod_map),
                  pl.BlockSpec((1, 1, D_MODEL), mod_map),
                  pl.BlockSpec((D_MODEL, n_cols), vec, pipeline_mode=pl.Buffered(1)),
                  pl.BlockSpec((1, ATT_HEAD_DIM), vec),
                  pl.BlockSpec((1, ATT_HEAD_DIM), vec),
                  pl.BlockSpec((tm, ATT_HEAD_DIM), lambda i: (i % pos_tiles, 0)),
                  pl.BlockSpec((tm, ATT_HEAD_DIM), lambda i: (i % pos_tiles, 0))],
        out_specs=[pl.BlockSpec((tm, n_cols), lambda i: (i, 0)), pl.BlockSpec(vt_block, vt_map)],
        out_shape=[jax.ShapeDtypeStruct((rows, n_cols), BF16),
                   jax.ShapeDtypeStruct((rows // tokens, ATT_KV_HEADS, VT_ROWS, tokens), BF16)],
        compiler_params=pltpu.CompilerParams(
            dimension_semantics=("parallel",), vmem_limit_bytes=VMEM_LIMIT),
        name="inproj_rows",
    )(x2d, norm_w.reshape(1, -1), scale, shift, w_bf, q_norm_w.reshape(1, -1),
      k_norm_w.reshape(1, -1), cos, sin_signed)


KEY_TILE = 256
ATT_UNIT = 512
ATT_TQ = 1024


def _attention_kernel(q_ref, kl_ref, kc_ref, vtl_ref, vtc_ref, g_ref, o_ref):
    k = jnp.concatenate([kl_ref[0], kc_ref[0]], axis=0)
    vt = jnp.concatenate([vtl_ref[0, 0], vtc_ref[0, 0]], axis=1)

    tq = q_ref.shape[1]
    units = [(r, c0) for r in range(ATT_REP) for c0 in range(0, tq, ATT_UNIT)]

    def logits(u):
        r, c0 = units[u]
        q = q_ref[0, c0:c0 + ATT_UNIT, r * ATT_HEAD_DIM:(r + 1) * ATT_HEAD_DIM]
        s_t = _dot_nt(k, q)
        return s_t, jnp.max(s_t, axis=0, keepdims=True)

    def finish(u, s_t, m, s_next):
        r, c0 = units[u]
        rows = slice(c0, c0 + ATT_UNIT)
        cols = slice(r * ATT_HEAD_DIM, (r + 1) * ATT_HEAD_DIM)
        if s_next is None:
            p_t = jnp.exp2(s_t - m).astype(BF16)
        else:
            tiles = []
            for j in range(0, s_t.shape[0], KEY_TILE):
                probe = pltpu.bitcast(s_next[j:j + 8, :], jnp.uint32)
                zero = ((probe >> 16) >> 16).astype(F32)[0:1, :]
                tiles.append(jnp.exp2(s_t[j:j + KEY_TILE, :] - (m + zero)).astype(BF16))
            p_t = jnp.concatenate(tiles, axis=0)
        o_t = _dot(vt, p_t)
        o = (o_t[:ATT_HEAD_DIM] * (1.0 / o_t[ATT_HEAD_DIM:ATT_HEAD_DIM + 1])).T
        o_ref[0, rows, cols] = (o * _silu(g_ref[0, rows, cols].astype(F32))).astype(BF16)

    pending = logits(0)
    for u in range(1, len(units)):
        nxt = logits(u)
        finish(u - 1, *pending, nxt[0])
        pending = nxt
    finish(len(units) - 1, *pending, None)


def _attention(px3, pc3, vt_lat, vt_ctx):
    b, s_len, _ = px3.shape
    c_len = pc3.shape[1]
    gw = ATT_REP * ATT_HEAD_DIM
    tq = ATT_TQ
    k_col = lambda i, g, t: (i, 0, OFF_ATT_K // ATT_HEAD_DIM + g)
    return pl.pallas_call(
        _attention_kernel,
        grid=(b, ATT_KV_HEADS, s_len // tq),
        in_specs=[pl.BlockSpec((1, tq, gw), lambda i, g, t: (i, t, OFF_ATT_Q // gw + g)),
                  pl.BlockSpec((1, s_len, ATT_HEAD_DIM), k_col),
                  pl.BlockSpec((1, c_len, ATT_HEAD_DIM), k_col),
                  pl.BlockSpec((1, 1, VT_ROWS, s_len), lambda i, g, t: (i, g, 0, 0)),
                  pl.BlockSpec((1, 1, VT_ROWS, c_len), lambda i, g, t: (i, g, 0, 0)),
                  pl.BlockSpec((1, tq, gw), lambda i, g, t: (i, t, OFF_ATT_G // gw + g))],
        out_specs=pl.BlockSpec((1, tq, gw), lambda i, g, t: (i, t, g)),
        out_shape=jax.ShapeDtypeStruct((b, s_len, ATT_Q_W), BF16),
        compiler_params=pltpu.CompilerParams(
            dimension_semantics=("parallel", "parallel", "arbitrary"),
            vmem_limit_bytes=VMEM_LIMIT),
        name="attention",
    )(px3, px3, pc3, vt_lat, vt_ctx, px3)


def _retention_kernel(d_ref, q_ref, k_ref, v_ref, g_ref, kc_ref, vc_ref, o_ref, sb_ref):
    s_len = q_ref.shape[1]
    c_len = kc_ref.shape[1]
    ch = RET_CHUNK
    n_chunks = s_len // ch

    d = d_ref[0]
    lg = jnp.log1p(-jnp.exp2(d))
    lg_f = lg[0:1, :]
    lg_b = lg[1:2, :]

    idx = lax.broadcasted_iota(jnp.int32, (ch, 1), 0).astype(F32)
    qd_f = (jnp.exp(lg_f * (idx + 1.0)) * (RET_DK ** -0.5)).astype(BF16)
    kd_f = jnp.exp(lg_f * (ch - 1.0 - idx)).astype(BF16)
    qd_b = (jnp.exp(lg_b * (ch - idx)) * (RET_DK ** -0.5)).astype(BF16)
    kd_b = jnp.exp(lg_b * idx).astype(BF16)
    cd_f = jnp.exp(lg_f * ch)
    cd_b = jnp.exp(lg_b * ch)

    row = lax.broadcasted_iota(jnp.int32, (ch, ch), 0)
    col = lax.broadcasted_iota(jnp.int32, (ch, ch), 1)
    rel = (row - col).astype(F32)
    mask = jnp.where(rel > 0, jnp.exp(lg_f * rel),
                     jnp.where(rel < 0, jnp.exp(-lg_b * rel), 2.0)) * (RET_DK ** -0.5)

    def chunk(n):
        return slice(n * ch, (n + 1) * ch)

    def masked_scores(n):
        return (_dot_nt(q_ref[0, chunk(n), :], k_ref[0, chunk(n), :]) * mask).astype(BF16)

    pos = lax.broadcasted_iota(jnp.int32, (c_len, 1), 0).astype(F32)
    kc = kc_ref[0].astype(F32)
    vc = vc_ref[0]
    s_f = _dot_tn((kc * jnp.exp(lg_f * (c_len - 1.0 - pos))).astype(BF16), vc)
    s_b = _dot_tn((kc * jnp.exp(lg_b * pos)).astype(BF16), vc)

    for n in reversed(range(n_chunks)):
        sb_ref[n] = s_b.astype(BF16)
        if n > 0:
            s_b = s_b * cd_b + _dot_tn(k_ref[0, chunk(n), :] * kd_b, v_ref[0, chunk(n), :])

    scores = masked_scores(0)
    for n in range(n_chunks):
        q = q_ref[0, chunk(n), :]
        v = v_ref[0, chunk(n), :]
        last = n + 1 == n_chunks
        next_scores = None if last else masked_scores(n + 1)
        kv_f = None if last else _dot_tn(k_ref[0, chunk(n), :] * kd_f, v)
        o = _dot(q * qd_b, sb_ref[n]) + _dot(q * qd_f, s_f.astype(BF16)) + _dot(scores, v)
        ms = jnp.mean(o * o, axis=-1, keepdims=True)
        o = (o * lax.rsqrt(ms + NORM_EPS)).astype(BF16)
        o_ref[0, chunk(n), :] = o * _silu(g_ref[0, chunk(n), :])
        if not last:
            s_f = s_f * cd_f + kv_f
            scores = next_scores


def _retention(px3, pc3, decay3):
    b, s_len, _ = px3.shape
    c_len = pc3.shape[1]
    return pl.pallas_call(
        _retention_kernel,
        grid=(b, RET_HEADS),
        in_specs=[pl.BlockSpec((1, 2, 1), lambda i, h: (h, 0, 0)),
                  pl.BlockSpec((1, s_len, RET_DK), lambda i, h: (i, 0, OFF_RET_Q // RET_DK + h)),
                  pl.BlockSpec((1, s_len, RET_DK), lambda i, h: (i, 0, OFF_RET_K // RET_DK + h)),
                  pl.BlockSpec((1, s_len, RET_DV), lambda i, h: (i, 0, OFF_RET_V // RET_DV + h)),
                  pl.BlockSpec((1, s_len, RET_DV), lambda i, h: (i, 0, OFF_RET_G // RET_DV + h)),
                  pl.BlockSpec((1, c_len, RET_DK), lambda i, h: (i, 0, h)),
                  pl.BlockSpec((1, c_len, RET_DV), lambda i, h: (i, 0, RET_QK_W // RET_DV + h))],
        out_specs=pl.BlockSpec((1, s_len, RET_DV), lambda i, h: (i, 0, h)),
        out_shape=jax.ShapeDtypeStruct((b, s_len, RET_V_W), BF16),
        scratch_shapes=[pltpu.VMEM((s_len // RET_CHUNK, RET_DK, RET_DV), BF16)],
        compiler_params=pltpu.CompilerParams(
            dimension_semantics=("parallel", "arbitrary"), vmem_limit_bytes=VMEM_LIMIT),
        name="retention",
    )(decay3, px3, px3, px3, px3, pc3, pc3)


def _merge_kernel(x_ref, yr_ref, ya_ref, mr0_ref, mr1_ref, ma0_ref, ma1_ref, gate_ref,
                  wr_ref, wa_ref, wo_ref, o_ref):
    a = _dot(yr_ref[...], wr_ref[...])
    b = _dot(ya_ref[...], wa_ref[...])
    mg_ret = jnp.concatenate([mr0_ref[...], mr1_ref[...]], axis=-1).astype(F32)
    mg_att = jnp.concatenate([ma0_ref[...], ma1_ref[...]], axis=-1).astype(F32)
    y = jax.nn.sigmoid(mg_ret) * a + jax.nn.sigmoid(mg_att) * b
    out = _dot(y.astype(BF16), wo_ref[...])
    o_ref[...] = x_ref[...] + gate_ref[0] * out


def _merge(x2d, y_ret, y_att, px, gate, w_o_ret, w_o_att, w_out, rows_per_mod, tm):
    rows = x2d.shape[0]
    half = D_MODEL // 2
    weight = lambda shape: pl.BlockSpec(shape, lambda i: (0, 0), pipeline_mode=pl.Buffered(1))
    mg = lambda off: pl.BlockSpec((tm, half), lambda i: (i, off // half))
    return pl.pallas_call(
        _merge_kernel,
        grid=(rows // tm,),
        in_specs=[pl.BlockSpec((tm, D_MODEL), lambda i: (i, 0)),
                  pl.BlockSpec((tm, RET_V_W), lambda i: (i, 0)),
                  pl.BlockSpec((tm, ATT_Q_W), lambda i: (i, 0)),
                  mg(OFF_MG_RET), mg(OFF_MG_RET + half), mg(OFF_MG_ATT), mg(OFF_MG_ATT + half),
                  pl.BlockSpec((1, 1, D_MODEL), lambda i: ((i * tm) // rows_per_mod, 0, 0)),
                  weight((RET_V_W, D_MODEL)), weight((ATT_Q_W, D_MODEL)), weight((D_MODEL, D_MODEL))],
        out_specs=pl.BlockSpec((tm, D_MODEL), lambda i: (i, 0)),
        out_shape=jax.ShapeDtypeStruct((rows, D_MODEL), F32),
        compiler_params=pltpu.CompilerParams(
            dimension_semantics=("parallel",), vmem_limit_bytes=VMEM_LIMIT),
        name="merge",
    )(x2d, y_ret, y_att, px, px, px, px, gate, w_o_ret, w_o_att, w_out)


def _rope_tables(s_len):
    rows = s_len // GRID_W
    row = jnp.repeat(jnp.arange(rows, dtype=F32), GRID_W)
    col = jnp.tile(jnp.arange(GRID_W, dtype=F32), rows)
    half = ATT_HEAD_DIM // 2
    freqs = ROPE_THETA ** (-jnp.arange(0, half, 2, dtype=F32) / half)
    ang = jnp.concatenate([row[:, None] * freqs, col[:, None] * freqs], axis=-1)
    cos, sin = jnp.cos(ang), jnp.sin(ang)
    return jnp.concatenate([cos, cos], axis=-1), jnp.concatenate([-sin, sin], axis=-1)


def _split_pairs(t):
    lead = t.shape[:-1]
    t = t.reshape(*lead, -1, ATT_HEAD_DIM // 2, 2)
    return jnp.swapaxes(t, -1, -2).reshape(*lead, -1)


def _layer(x, ctx, c, c_ctx, norm_w, w_ada, b_ada, w_in, ret_log2_decay,
           q_norm_w, k_norm_w, w_o_ret, w_o_att, w_out):
    b, s_len, _ = x.shape
    c_len = ctx.shape[1]

    mod = _adaln(jnp.concatenate([c, c_ctx[None]], axis=0), w_ada, b_ada)
    shift, scale, gate = (mod[:, i * D_MODEL:(i + 1) * D_MODEL] for i in range(3))
    lat = lambda t: t[:b].reshape(b, 1, D_MODEL)
    cx = lambda t: t[b:].reshape(1, 1, D_MODEL)

    w_bf = w_in.astype(BF16)
    for off, width in ((OFF_ATT_K, ATT_KV_W), (OFF_ATT_Q, ATT_Q_W)):
        w_bf = w_bf.at[:, off:off + width].set(_split_pairs(w_bf[:, off:off + width]))
    q_norm_w = _split_pairs(q_norm_w)
    k_norm_w = _split_pairs(k_norm_w)

    cos, sin_signed = _rope_tables(s_len)
    x2d = x.reshape(b * s_len, D_MODEL)
    px, vt_lat = _inproj_rows(x2d, norm_w, lat(scale), lat(shift), w_bf, q_norm_w, k_norm_w,
                              cos, sin_signed, IN_COLS, s_len, s_len, PROJ_TM, PROJ_TN, rope=True)
    pc, vt_ctx = _inproj_rows(ctx.reshape(b * c_len, D_MODEL), norm_w, cx(scale), cx(shift), w_bf,
                              q_norm_w, k_norm_w, cos, sin_signed,
                              KV_COLS, b * c_len, c_len, PROJ_TM, PROJ_TN_CTX, rope=False)
    px3 = px.reshape(b, s_len, IN_COLS)
    pc3 = pc.reshape(b, c_len, KV_COLS)

    y_att = _attention(px3, pc3, vt_lat, vt_ctx)

    decay3 = ret_log2_decay.astype(F32).T.reshape(RET_HEADS, 2, 1)
    y_ret = _retention(px3, pc3, decay3)

    x_new = _merge(x2d, y_ret.reshape(b * s_len, RET_V_W), y_att.reshape(b * s_len, ATT_Q_W), px,
                   lat(gate), w_o_ret.astype(BF16), w_o_att.astype(BF16), w_out.astype(BF16),
                   s_len, MERGE_TM)
    return x_new.reshape(b, s_len, D_MODEL)


def kernel(x, c, ctx, c_ctx, norm_w, w_ada, b_ada, w_in, ret_log2_decay, q_norm_w, k_norm_w,
           w_o_ret, w_o_att, w_out):
    depth = norm_w.shape[0]
    assert depth == 1, "context update between layers is not implemented"
    return _layer(x, ctx, c, c_ctx, norm_w[0], w_ada[0], b_ada[0], w_in[0], ret_log2_decay[0],
                  q_norm_w[0], k_norm_w[0], w_o_ret[0], w_o_att[0], w_out[0])
```

```python
import functools

import jax
import jax.numpy as jnp
from jax import lax
from jax.experimental import pallas as pl
from jax.experimental.pallas import tpu as pltpu

D_MODEL = 1024
GRID_W = 64
RET_HEADS = 4
RET_DK = 256
RET_DV = 512
RET_QK_W = RET_HEADS * RET_DK
RET_V_W = RET_HEADS * RET_DV
ATT_HEADS = 8
ATT_KV_HEADS = 2
ATT_HEAD_DIM = 128
ATT_Q_W = ATT_HEADS * ATT_HEAD_DIM
ATT_KV_W = ATT_KV_HEADS * ATT_HEAD_DIM
ATT_REP = ATT_HEADS // ATT_KV_HEADS
ROPE_THETA = 10000.0
NORM_EPS = 1e-6
KV_COLS = RET_QK_W + RET_V_W + 2 * ATT_KV_W
LOG2E = 1.4426950408889634

OFF_RET_K = 0
OFF_RET_V = OFF_RET_K + RET_QK_W
OFF_ATT_K = OFF_RET_V + RET_V_W
OFF_ATT_V = OFF_ATT_K + ATT_KV_W
OFF_RET_Q = OFF_ATT_V + ATT_KV_W
OFF_RET_G = OFF_RET_Q + RET_QK_W
OFF_ATT_Q = OFF_RET_G + RET_V_W
OFF_ATT_G = OFF_ATT_Q + ATT_Q_W
OFF_MG_RET = OFF_ATT_G + ATT_Q_W
OFF_MG_ATT = OFF_MG_RET + D_MODEL
IN_COLS = OFF_MG_ATT + D_MODEL
assert OFF_RET_Q == KV_COLS

RET_CHUNK = 256
PROJ_TM = 512
PROJ_TN = 1536
PROJ_TN_CTX = 1792
MERGE_TM = 1024
BF16_SUBLANES = 16
VT_ROWS = ATT_HEAD_DIM + BF16_SUBLANES
VMEM_LIMIT = 56 * 1024 * 1024

BF16 = jnp.bfloat16
F32 = jnp.float32


def _dot(a, b):
    return jnp.dot(a, b, preferred_element_type=F32)


def _dot_nt(a, b):
    return lax.dot_general(a, b, (((1,), (1,)), ((), ())), preferred_element_type=F32)


def _dot_tn(a, b):
    return lax.dot_general(a, b, (((0,), (0,)), ((), ())), preferred_element_type=F32)


def _silu(x):
    return x * jax.nn.sigmoid(x)


def _silu_tanh(x):
    h = 0.5 * x
    return h * jnp.tanh(h) + h


def _sigmoid_tanh(x):
    return 0.5 * jnp.tanh(0.5 * x) + 0.5


GATE_COLS = ((OFF_RET_G, OFF_RET_G + RET_V_W, _silu_tanh),
             (OFF_ATT_G, OFF_ATT_G + ATT_Q_W, _silu_tanh),
             (OFF_MG_RET, OFF_MG_ATT + D_MODEL, _sigmoid_tanh))


def _adaln_kernel(c_ref, w_ref, b_ref, o_ref):
    o_ref[...] = _dot(_silu(c_ref[...]), w_ref[...]) + b_ref[...]


def _adaln(cc, w_ada, b_ada):
    rows = cc.shape[0]
    tn = D_MODEL
    return pl.pallas_call(
        _adaln_kernel,
        grid=(3 * D_MODEL // tn,),
        in_specs=[pl.BlockSpec((rows, D_MODEL), lambda j: (0, 0)),
                  pl.BlockSpec((D_MODEL, tn), lambda j: (0, j)),
                  pl.BlockSpec((1, tn), lambda j: (0, j))],
        out_specs=pl.BlockSpec((rows, tn), lambda j: (0, j)),
        out_shape=jax.ShapeDtypeStruct((rows, 3 * D_MODEL), F32),
        compiler_params=pltpu.CompilerParams(vmem_limit_bytes=VMEM_LIMIT),
        name="adaln",
    )(cc, w_ada, b_ada.reshape(1, -1))


def _head_rms(t, w):
    ms = jnp.mean(t * t, axis=-1, keepdims=True)
    return t * lax.rsqrt(ms + NORM_EPS) * w


def _rope(t, cos, sin_signed):
    return t * cos + pltpu.roll(t, ATT_HEAD_DIM // 2, 1) * sin_signed


def _inproj_rows_kernel(x_ref, nw_ref, sc_ref, sh_ref, w_ref, qw_ref, kw_ref, cos_ref, sin_ref,
                        o_ref, vt_ref, *, tn, rope, prep_q):
    x = x_ref[...]
    ms = jnp.mean(x * x, axis=-1, keepdims=True)
    y = x * lax.rsqrt(ms + NORM_EPS) * nw_ref[...]
    h = (y * (1.0 + sc_ref[0]) + sh_ref[0]).astype(BF16)
    n_cols = o_ref.shape[1]
    heads = [(OFF_ATT_K + g * ATT_HEAD_DIM, kw_ref[...]) for g in range(ATT_KV_HEADS)]
    if prep_q:
        qw = qw_ref[...] * (ATT_HEAD_DIM ** -0.5 * LOG2E)
        heads += [(OFF_ATT_Q + r * ATT_HEAD_DIM, qw) for r in range(ATT_HEADS)]
    for c0 in range(0, n_cols, tn):
        res = _dot(h, w_ref[:, c0:c0 + tn])
        edges = sorted({c0, c0 + tn} | {e for lo, hi, _ in GATE_COLS for e in (lo, hi) if c0 < e < c0 + tn})
        for lo, hi in zip(edges[:-1], edges[1:]):
            piece = res[:, lo - c0:hi - c0]
            for g_lo, g_hi, act in GATE_COLS:
                if g_lo <= lo < g_hi:
                    piece = act(piece)
            o_ref[:, lo:hi] = piece.astype(BF16)
        for col, w in heads:
            if c0 <= col < c0 + tn:
                t = _head_rms(res[:, col - c0:col - c0 + ATT_HEAD_DIM], w)
                if rope:
                    t = _rope(t, cos_ref[...], sin_ref[...])
                o_ref[:, col:col + ATT_HEAD_DIM] = t.astype(BF16)
        for g in range(ATT_KV_HEADS):
            col = OFF_ATT_V + g * ATT_HEAD_DIM
            if c0 <= col < c0 + tn:
                n_b, _, _, n_tok = vt_ref.shape
                v = res[:, col - c0:col - c0 + ATT_HEAD_DIM]
                for bb in range(n_b):
                    vt_ref[bb, g, :ATT_HEAD_DIM, :] = v[bb * n_tok:(bb + 1) * n_tok].T.astype(BF16)
                    vt_ref[bb, g, ATT_HEAD_DIM:, :] = jnp.ones((VT_ROWS - ATT_HEAD_DIM, n_tok), BF16)


def _inproj_rows(x2d, norm_w, scale, shift, w_bf, q_norm_w, k_norm_w, cos, sin_signed,
                 n_cols, rows_per_mod, tokens, tm, tn, rope):
    rows = x2d.shape[0]
    if tokens >= tm:
        per_batch = tokens // tm
        vt_block = (1, ATT_KV_HEADS, VT_ROWS, tm)
        vt_map = lambda i: (i // per_batch, 0, 0, i % per_batch)
    else:
        vt_block = (tm // tokens, ATT_KV_HEADS, VT_ROWS, tokens)
        vt_map = lambda i: (i, 0, 0, 0)
    pos_tiles = cos.shape[0] // tm
    mod_map = lambda i: ((i * tm) // rows_per_mod, 0, 0)
    vec = lambda i: (0, 0)
    return pl.pallas_call(
        functools.partial(_inproj_rows_kernel, tn=tn, rope=rope, prep_q=n_cols > OFF_ATT_Q),
        grid=(rows // tm,),
        in_specs=[pl.BlockSpec((tm, D_MODEL), lambda i: (i, 0)),
                  pl.BlockSpec((1, D_MODEL), vec),
                  pl.BlockSpec((1, 1, D_MODEL), mod_map),
                  pl.BlockSpec((1, 1, D_MODEL), mod_map),
                  pl.BlockSpec((D_MODEL, n_cols), vec, pipeline_mode=pl.Buffered(1)),
                  pl.BlockSpec((1, ATT_HEAD_DIM), vec),
                  pl.BlockSpec((1, ATT_HEAD_DIM), vec),
                  pl.BlockSpec((tm, ATT_HEAD_DIM), lambda i: (i % pos_tiles, 0)),
                  pl.BlockSpec((tm, ATT_HEAD_DIM), lambda i: (i % pos_tiles, 0))],
        out_specs=[pl.BlockSpec((tm, n_cols), lambda i: (i, 0)), pl.BlockSpec(vt_block, vt_map)],
        out_shape=[jax.ShapeDtypeStruct((rows, n_cols), BF16),
                   jax.ShapeDtypeStruct((rows // tokens, ATT_KV_HEADS, VT_ROWS, tokens), BF16)],
        compiler_params=pltpu.CompilerParams(
            dimension_semantics=("parallel",), vmem_limit_bytes=VMEM_LIMIT),
        name="inproj_rows",
    )(x2d, norm_w.reshape(1, -1), scale, shift, w_bf, q_norm_w.reshape(1, -1),
      k_norm_w.reshape(1, -1), cos, sin_signed)


KEY_TILE = 256
ATT_UNIT = 512
ATT_TQ = 1024


def _attention_kernel(q_ref, kl_ref, kc_ref, vtl_ref, vtc_ref, g_ref, o_ref):
    k = jnp.concatenate([kl_ref[0], kc_ref[0]], axis=0)
    vt = jnp.concatenate([vtl_ref[0, 0], vtc_ref[0, 0]], axis=1)

    tq = q_ref.shape[1]
    units = [(r, c0) for r in range(ATT_REP) for c0 in range(0, tq, ATT_UNIT)]

    def logits(u):
        r, c0 = units[u]
        q = q_ref[0, c0:c0 + ATT_UNIT, r * ATT_HEAD_DIM:(r + 1) * ATT_HEAD_DIM]
        s_t = _dot_nt(k, q)
        return s_t, jnp.max(s_t, axis=0, keepdims=True)

    def finish(u, s_t, m, s_next):
        r, c0 = units[u]
        rows = slice(c0, c0 + ATT_UNIT)
        cols = slice(r * ATT_HEAD_DIM, (r + 1) * ATT_HEAD_DIM)
        if s_next is None:
            p_t = jnp.exp2(s_t - m).astype(BF16)
        else:
            tiles = []
            for j in range(0, s_t.shape[0], KEY_TILE):
                probe = pltpu.bitcast(s_next[j:j + 8, :], jnp.uint32)
                zero = ((probe >> 16) >> 16).astype(F32)[0:1, :]
                tiles.append(jnp.exp2(s_t[j:j + KEY_TILE, :] - (m + zero)).astype(BF16))
            p_t = jnp.concatenate(tiles, axis=0)
        o_t = _dot(vt, p_t)
        o = (o_t[:ATT_HEAD_DIM] * (1.0 / o_t[ATT_HEAD_DIM:ATT_HEAD_DIM + 1])).T
        o_ref[0, rows, cols] = (o * g_ref[0, rows, cols].astype(F32)).astype(BF16)

    pending = logits(0)
    for u in range(1, len(units)):
        nxt = logits(u)
        finish(u - 1, *pending, nxt[0])
        pending = nxt
    finish(len(units) - 1, *pending, None)


def _attention(px3, pc3, vt_lat, vt_ctx):
    b, s_len, _ = px3.shape
    c_len = pc3.shape[1]
    gw = ATT_REP * ATT_HEAD_DIM
    tq = ATT_TQ
    k_col = lambda i, g, t: (i, 0, OFF_ATT_K // ATT_HEAD_DIM + g)
    return pl.pallas_call(
        _attention_kernel,
        grid=(b, ATT_KV_HEADS, s_len // tq),
        in_specs=[pl.BlockSpec((1, tq, gw), lambda i, g, t: (i, t, OFF_ATT_Q // gw + g)),
                  pl.BlockSpec((1, s_len, ATT_HEAD_DIM), k_col),
                  pl.BlockSpec((1, c_len, ATT_HEAD_DIM), k_col),
                  pl.BlockSpec((1, 1, VT_ROWS, s_len), lambda i, g, t: (i, g, 0, 0)),
                  pl.BlockSpec((1, 1, VT_ROWS, c_len), lambda i, g, t: (i, g, 0, 0)),
                  pl.BlockSpec((1, tq, gw), lambda i, g, t: (i, t, OFF_ATT_G // gw + g))],
        out_specs=pl.BlockSpec((1, tq, gw), lambda i, g, t: (i, t, g)),
        out_shape=jax.ShapeDtypeStruct((b, s_len, ATT_Q_W), BF16),
        compiler_params=pltpu.CompilerParams(
            dimension_semantics=("parallel", "parallel", "arbitrary"),
            vmem_limit_bytes=VMEM_LIMIT),
        name="attention",
    )(px3, px3, pc3, vt_lat, vt_ctx, px3)


def _retention_kernel(d_ref, q_ref, k_ref, v_ref, g_ref, kc_ref, vc_ref, o_ref, sb_ref):
    s_len = q_ref.shape[1]
    c_len = kc_ref.shape[1]
    ch = RET_CHUNK
    n_chunks = s_len // ch

    d = d_ref[0]
    lg = jnp.log1p(-jnp.exp2(d))
    lg_f = lg[0:1, :]
    lg_b = lg[1:2, :]

    idx = lax.broadcasted_iota(jnp.int32, (ch, 1), 0).astype(F32)
    qd_f = (jnp.exp(lg_f * (idx + 1.0)) * (RET_DK ** -0.5)).astype(BF16)
    kd_f = jnp.exp(lg_f * (ch - 1.0 - idx)).astype(BF16)
    qd_b = (jnp.exp(lg_b * (ch - idx)) * (RET_DK ** -0.5)).astype(BF16)
    kd_b = jnp.exp(lg_b * idx).astype(BF16)
    cd_f = jnp.exp(lg_f * ch)
    cd_b = jnp.exp(lg_b * ch)

    row = lax.broadcasted_iota(jnp.int32, (ch, ch), 0)
    col = lax.broadcasted_iota(jnp.int32, (ch, ch), 1)
    rel = (row - col).astype(F32)
    mask = jnp.where(rel > 0, jnp.exp(lg_f * rel),
                     jnp.where(rel < 0, jnp.exp(-lg_b * rel), 2.0)) * (RET_DK ** -0.5)

    def chunk(n):
        return slice(n * ch, (n + 1) * ch)

    def masked_scores(n):
        return (_dot_nt(q_ref[0, chunk(n), :], k_ref[0, chunk(n), :]) * mask).astype(BF16)

    pos = lax.broadcasted_iota(jnp.int32, (c_len, 1), 0).astype(F32)
    kc = kc_ref[0].astype(F32)
    vc = vc_ref[0]
    s_f = _dot_tn((kc * jnp.exp(lg_f * (c_len - 1.0 - pos))).astype(BF16), vc)
    s_b = _dot_tn((kc * jnp.exp(lg_b * pos)).astype(BF16), vc)

    for n in reversed(range(n_chunks)):
        sb_ref[n] = s_b.astype(BF16)
        if n > 0:
            s_b = s_b * cd_b + _dot_tn(k_ref[0, chunk(n), :] * kd_b, v_ref[0, chunk(n), :])

    scores = masked_scores(0)
    for n in range(n_chunks):
        q = q_ref[0, chunk(n), :]
        v = v_ref[0, chunk(n), :]
        last = n + 1 == n_chunks
        next_scores = None if last else masked_scores(n + 1)
        kv_f = None if last else _dot_tn(k_ref[0, chunk(n), :] * kd_f, v)
        o = _dot(q * qd_b, sb_ref[n]) + _dot(q * qd_f, s_f.astype(BF16)) + _dot(scores, v)
        ms = jnp.mean(o * o, axis=-1, keepdims=True)
        o = (o * lax.rsqrt(ms + NORM_EPS)).astype(BF16)
        o_ref[0, chunk(n), :] = o * g_ref[0, chunk(n), :]
        if not last:
            s_f = s_f * cd_f + kv_f
            scores = next_scores


def _retention(px3, pc3, decay3):
    b, s_len, _ = px3.shape
    c_len = pc3.shape[1]
    return pl.pallas_call(
        _retention_kernel,
        grid=(b, RET_HEADS),
        in_specs=[pl.BlockSpec((1, 2, 1), lambda i, h: (h, 0, 0)),
                  pl.BlockSpec((1, s_len, RET_DK), lambda i, h: (i, 0, OFF_RET_Q // RET_DK + h)),
                  pl.BlockSpec((1, s_len, RET_DK), lambda i, h: (i, 0, OFF_RET_K // RET_DK + h)),
                  pl.BlockSpec((1, s_len, RET_DV), lambda i, h: (i, 0, OFF_RET_V // RET_DV + h)),
                  pl.BlockSpec((1, s_len, RET_DV), lambda i, h: (i, 0, OFF_RET_G // RET_DV + h)),
                  pl.BlockSpec((1, c_len, RET_DK), lambda i, h: (i, 0, h)),
                  pl.BlockSpec((1, c_len, RET_DV), lambda i, h: (i, 0, RET_QK_W // RET_DV + h))],
        out_specs=pl.BlockSpec((1, s_len, RET_DV), lambda i, h: (i, 0, h)),
        out_shape=jax.ShapeDtypeStruct((b, s_len, RET_V_W), BF16),
        scratch_shapes=[pltpu.VMEM((s_len // RET_CHUNK, RET_DK, RET_DV), BF16)],
        compiler_params=pltpu.CompilerParams(
            dimension_semantics=("parallel", "arbitrary"), vmem_limit_bytes=VMEM_LIMIT),
        name="retention",
    )(decay3, px3, px3, px3, px3, pc3, pc3)


def _merge_kernel(x_ref, yr_ref, ya_ref, mr0_ref, mr1_ref, ma0_ref, ma1_ref, gate_ref,
                  wr_ref, wa_ref, wo_ref, o_ref):
    a = _dot(yr_ref[...], wr_ref[...])
    b = _dot(ya_ref[...], wa_ref[...])
    mg_ret = jnp.concatenate([mr0_ref[...], mr1_ref[...]], axis=-1).astype(F32)
    mg_att = jnp.concatenate([ma0_ref[...], ma1_ref[...]], axis=-1).astype(F32)
    y = mg_ret * a + mg_att * b
    out = _dot(y.astype(BF16), wo_ref[...])
    o_ref[...] = x_ref[...] + gate_ref[0] * out


def _merge(x2d, y_ret, y_att, px, gate, w_o_ret, w_o_att, w_out, rows_per_mod, tm):
    rows = x2d.shape[0]
    half = D_MODEL // 2
    weight = lambda shape: pl.BlockSpec(shape, lambda i: (0, 0), pipeline_mode=pl.Buffered(1))
    mg = lambda off: pl.BlockSpec((tm, half), lambda i: (i, off // half))
    return pl.pallas_call(
        _merge_kernel,
        grid=(rows // tm,),
        in_specs=[pl.BlockSpec((tm, D_MODEL), lambda i: (i, 0)),
                  pl.BlockSpec((tm, RET_V_W), lambda i: (i, 0)),
                  pl.BlockSpec((tm, ATT_Q_W), lambda i: (i, 0)),
                  mg(OFF_MG_RET), mg(OFF_MG_RET + half), mg(OFF_MG_ATT), mg(OFF_MG_ATT + half),
                  pl.BlockSpec((1, 1, D_MODEL), lambda i: ((i * tm) // rows_per_mod, 0, 0)),
                  weight((RET_V_W, D_MODEL)), weight((ATT_Q_W, D_MODEL)), weight((D_MODEL, D_MODEL))],
        out_specs=pl.BlockSpec((tm, D_MODEL), lambda i: (i, 0)),
        out_shape=jax.ShapeDtypeStruct((rows, D_MODEL), F32),
        compiler_params=pltpu.CompilerParams(
            dimension_semantics=("parallel",), vmem_limit_bytes=VMEM_LIMIT),
        name="merge",
    )(x2d, y_ret, y_att, px, px, px, px, gate, w_o_ret, w_o_att, w_out)


def _rope_tables(s_len):
    rows = s_len // GRID_W
    row = jnp.repeat(jnp.arange(rows, dtype=F32), GRID_W)
    col = jnp.tile(jnp.arange(GRID_W, dtype=F32), rows)
    half = ATT_HEAD_DIM // 2
    freqs = ROPE_THETA ** (-jnp.arange(0, half, 2, dtype=F32) / half)
    ang = jnp.concatenate([row[:, None] * freqs, col[:, None] * freqs], axis=-1)
    cos, sin = jnp.cos(ang), jnp.sin(ang)
    return jnp.concatenate([cos, cos], axis=-1), jnp.concatenate([-sin, sin], axis=-1)


def _split_pairs(t):
    lead = t.shape[:-1]
    t = t.reshape(*lead, -1, ATT_HEAD_DIM // 2, 2)
    return jnp.swapaxes(t, -1, -2).reshape(*lead, -1)


def _layer(x, ctx, c, c_ctx, norm_w, w_ada, b_ada, w_in, ret_log2_decay,
           q_norm_w, k_norm_w, w_o_ret, w_o_att, w_out):
    b, s_len, _ = x.shape
    c_len = ctx.shape[1]

    mod = _adaln(jnp.concatenate([c, c_ctx[None]], axis=0), w_ada, b_ada)
    shift, scale, gate = (mod[:, i * D_MODEL:(i + 1) * D_MODEL] for i in range(3))
    lat = lambda t: t[:b].reshape(b, 1, D_MODEL)
    cx = lambda t: t[b:].reshape(1, 1, D_MODEL)

    w_bf = w_in.astype(BF16)
    for off, width in ((OFF_ATT_K, ATT_KV_W), (OFF_ATT_Q, ATT_Q_W)):
        w_bf = w_bf.at[:, off:off + width].set(_split_pairs(w_bf[:, off:off + width]))
    q_norm_w = _split_pairs(q_norm_w)
    k_norm_w = _split_pairs(k_norm_w)

    cos, sin_signed = _rope_tables(s_len)
    x2d = x.reshape(b * s_len, D_MODEL)
    px, vt_lat = _inproj_rows(x2d, norm_w, lat(scale), lat(shift), w_bf, q_norm_w, k_norm_w,
                              cos, sin_signed, IN_COLS, s_len, s_len, PROJ_TM, PROJ_TN, rope=True)
    pc, vt_ctx = _inproj_rows(ctx.reshape(b * c_len, D_MODEL), norm_w, cx(scale), cx(shift), w_bf,
                              q_norm_w, k_norm_w, cos, sin_signed,
                              KV_COLS, b * c_len, c_len, PROJ_TM, PROJ_TN_CTX, rope=False)
    px3 = px.reshape(b, s_len, IN_COLS)
    pc3 = pc.reshape(b, c_len, KV_COLS)

    y_att = _attention(px3, pc3, vt_lat, vt_ctx)

    decay3 = ret_log2_decay.astype(F32).T.reshape(RET_HEADS, 2, 1)
    y_ret = _retention(px3, pc3, decay3)

    x_new = _merge(x2d, y_ret.reshape(b * s_len, RET_V_W), y_att.reshape(b * s_len, ATT_Q_W), px,
                   lat(gate), w_o_ret.astype(BF16), w_o_att.astype(BF16), w_out.astype(BF16),
                   s_len, MERGE_TM)
    return x_new.reshape(b, s_len, D_MODEL)


def kernel(x, c, ctx, c_ctx, norm_w, w_ada, b_ada, w_in, ret_log2_decay, q_norm_w, k_norm_w,
           w_o_ret, w_o_att, w_out):
    depth = norm_w.shape[0]
    assert depth == 1, "context update between layers is not implemented"
    return _layer(x, ctx, c, c_ctx, norm_w[0], w_ada[0], b_ada[0], w_in[0], ret_log2_decay[0],
                  q_norm_w[0], k_norm_w[0], w_o_ret[0], w_o_att[0], w_out[0])
```

```python
import functools

import jax
import jax.numpy as jnp
from jax import lax
from jax.experimental import pallas as pl
from jax.experimental.pallas import tpu as pltpu

D_MODEL = 1024
GRID_W = 64
RET_HEADS = 4
RET_DK = 256
RET_DV = 512
RET_QK_W = RET_HEADS * RET_DK
RET_V_W = RET_HEADS * RET_DV
ATT_HEADS = 8
ATT_KV_HEADS = 2
ATT_HEAD_DIM = 128
ATT_Q_W = ATT_HEADS * ATT_HEAD_DIM
ATT_KV_W = ATT_KV_HEADS * ATT_HEAD_DIM
ATT_REP = ATT_HEADS // ATT_KV_HEADS
ROPE_THETA = 10000.0
NORM_EPS = 1e-6
KV_COLS = RET_QK_W + RET_V_W + 2 * ATT_KV_W
LOG2E = 1.4426950408889634

OFF_RET_K = 0
OFF_RET_V = OFF_RET_K + RET_QK_W
OFF_ATT_K = OFF_RET_V + RET_V_W
OFF_ATT_V = OFF_ATT_K + ATT_KV_W
OFF_RET_Q = OFF_ATT_V + ATT_KV_W
OFF_RET_G = OFF_RET_Q + RET_QK_W
OFF_ATT_Q = OFF_RET_G + RET_V_W
OFF_ATT_G = OFF_ATT_Q + ATT_Q_W
OFF_MG_RET = OFF_ATT_G + ATT_Q_W
OFF_MG_ATT = OFF_MG_RET + D_MODEL
IN_COLS = OFF_MG_ATT + D_MODEL
assert OFF_RET_Q == KV_COLS

RET_CHUNK = 256
PROJ_TM = 512
PROJ_TN = 1536
PROJ_TN_CTX = 1792
MERGE_TM = 1024
BF16_SUBLANES = 16
VT_ROWS = ATT_HEAD_DIM + BF16_SUBLANES
VMEM_LIMIT = 56 * 1024 * 1024

BF16 = jnp.bfloat16
F32 = jnp.float32


def _dot(a, b):
    return jnp.dot(a, b, preferred_element_type=F32)


def _dot_nt(a, b):
    return lax.dot_general(a, b, (((1,), (1,)), ((), ())), preferred_element_type=F32)


def _dot_tn(a, b):
    return lax.dot_general(a, b, (((0,), (0,)), ((), ())), preferred_element_type=F32)


def _silu(x):
    return x * jax.nn.sigmoid(x)


def _silu_tanh(x):
    h = 0.5 * x
    return h * jnp.tanh(h) + h


def _sigmoid_tanh(x):
    return 0.5 * jnp.tanh(0.5 * x) + 0.5


GATE_COLS = ((OFF_RET_G, OFF_RET_G + RET_V_W, _silu_tanh),
             (OFF_ATT_G, OFF_ATT_G + ATT_Q_W, _silu_tanh),
             (OFF_MG_RET, OFF_MG_ATT + D_MODEL, _sigmoid_tanh))


def _adaln_kernel(c_ref, w_ref, b_ref, o_ref):
    o_ref[...] = _dot(_silu(c_ref[...]), w_ref[...]) + b_ref[...]


def _adaln(cc, w_ada, b_ada):
    rows = cc.shape[0]
    tn = D_MODEL
    return pl.pallas_call(
        _adaln_kernel,
        grid=(3 * D_MODEL // tn,),
        in_specs=[pl.BlockSpec((rows, D_MODEL), lambda j: (0, 0)),
                  pl.BlockSpec((D_MODEL, tn), lambda j: (0, j)),
                  pl.BlockSpec((1, tn), lambda j: (0, j))],
        out_specs=pl.BlockSpec((rows, tn), lambda j: (0, j)),
        out_shape=jax.ShapeDtypeStruct((rows, 3 * D_MODEL), F32),
        compiler_params=pltpu.CompilerParams(vmem_limit_bytes=VMEM_LIMIT),
        name="adaln",
    )(cc, w_ada, b_ada.reshape(1, -1))


def _head_rms(t, w):
    ms = jnp.mean(t * t, axis=-1, keepdims=True)
    return t * lax.rsqrt(ms + NORM_EPS) * w


def _rope(t, cos, sin_signed):
    return t * cos + pltpu.roll(t, ATT_HEAD_DIM // 2, 1) * sin_signed


def _inproj_rows_kernel(x_ref, nw_ref, sc_ref, sh_ref, w_ref, qw_ref, kw_ref, cos_ref, sin_ref,
                        o_ref, vt_ref, *, tn, rope, prep_q):
    x = x_ref[...]
    ms = jnp.mean(x * x, axis=-1, keepdims=True)
    y = x * lax.rsqrt(ms + NORM_EPS) * nw_ref[...]
    h = (y * (1.0 + sc_ref[0]) + sh_ref[0]).astype(BF16)
    n_cols = o_ref.shape[1]
    heads = [(OFF_ATT_K + g * ATT_HEAD_DIM, kw_ref[...]) for g in range(ATT_KV_HEADS)]
    if prep_q:
        qw = qw_ref[...] * (ATT_HEAD_DIM ** -0.5 * LOG2E)
        heads += [(OFF_ATT_Q + r * ATT_HEAD_DIM, qw) for r in range(ATT_HEADS)]
    for c0 in range(0, n_cols, tn):
        res = _dot(h, w_ref[:, c0:c0 + tn])
        edges = sorted({c0, c0 + tn} | {e for lo, hi, _ in GATE_COLS for e in (lo, hi) if c0 < e < c0 + tn})
        for lo, hi in zip(edges[:-1], edges[1:]):
            piece = res[:, lo - c0:hi - c0]
            for g_lo, g_hi, act in GATE_COLS:
                if g_lo <= lo < g_hi:
                    piece = act(piece)
            o_ref[:, lo:hi] = piece.astype(BF16)
        for col, w in heads:
            if c0 <= col < c0 + tn:
                t = _head_rms(res[:, col - c0:col - c0 + ATT_HEAD_DIM], w)
                if rope:
                    t = _rope(t, cos_ref[...], sin_ref[...])
                o_ref[:, col:col + ATT_HEAD_DIM] = t.astype(BF16)
        for g in range(ATT_KV_HEADS):
            col = OFF_ATT_V + g * ATT_HEAD_DIM
            if c0 <= col < c0 + tn:
                n_b, _, _, n_tok = vt_ref.shape
                v = res[:, col - c0:col - c0 + ATT_HEAD_DIM]
                for bb in range(n_b):
                    vt_ref[bb, g, :ATT_HEAD_DIM, :] = v[bb * n_tok:(bb + 1) * n_tok].T.astype(BF16)
                    vt_ref[bb, g, ATT_HEAD_DIM:, :] = jnp.ones((VT_ROWS - ATT_HEAD_DIM, n_tok), BF16)


def _inproj_rows(x2d, norm_w, scale, shift, w_bf, q_norm_w, k_norm_w, cos, sin_signed,
                 n_cols, rows_per_mod, tokens, tm, tn, rope):
    rows = x2d.shape[0]
    if tokens >= tm:
        per_batch = tokens // tm
        vt_block = (1, ATT_KV_HEADS, VT_ROWS, tm)
        vt_map = lambda i: (i // per_batch, 0, 0, i % per_batch)
    else:
        vt_block = (tm // tokens, ATT_KV_HEADS, VT_ROWS, tokens)
        vt_map = lambda i: (i, 0, 0, 0)
    pos_tiles = cos.shape[0] // tm
    mod_map = lambda i: ((i * tm) // rows_per_mod, 0, 0)
    vec = lambda i: (0, 0)
    return pl.pallas_call(
        functools.partial(_inproj_rows_kernel, tn=tn, rope=rope, prep_q=n_cols > OFF_ATT_Q),
        grid=(rows // tm,),
        in_specs=[pl.BlockSpec((tm, D_MODEL), lambda i: (i, 0)),
                  pl.BlockSpec((1, D_MODEL), vec),
                  pl.BlockSpec((1, 1, D_MODEL), mod_map),
                  pl.BlockSpec((1, 1, D_MODEL), mod_map),
                  pl.BlockSpec((D_MODEL, n_cols), vec, pipeline_mode=pl.Buffered(1)),
                  pl.BlockSpec((1, ATT_HEAD_DIM), vec),
                  pl.BlockSpec((1, ATT_HEAD_DIM), vec),
                  pl.BlockSpec((tm, ATT_HEAD_DIM), lambda i: (i % pos_tiles, 0)),
                  pl.BlockSpec((tm, ATT_HEAD_DIM), lambda i: (i % pos_tiles, 0))],
        out_specs=[pl.BlockSpec((tm, n_cols), lambda i: (i, 0)), pl.BlockSpec(vt_block, vt_map)],
        out_shape=[jax.ShapeDtypeStruct((rows, n_cols), BF16),
                   jax.ShapeDtypeStruct((rows // tokens, ATT_KV_HEADS, VT_ROWS, tokens), BF16)],
        compiler_params=pltpu.CompilerParams(
            dimension_semantics=("parallel",), vmem_limit_bytes=VMEM_LIMIT),
        name="inproj_rows",
    )(x2d, norm_w.reshape(1, -1), scale, shift, w_bf, q_norm_w.reshape(1, -1),
      k_norm_w.reshape(1, -1), cos, sin_signed)


KEY_TILE = 128
ATT_UNIT = 512
ATT_TQ = 1024


def _attention_kernel(q_ref, kl_ref, kc_ref, vtl_ref, vtc_ref, g_ref, o_ref):
    k = jnp.concatenate([kl_ref[0], kc_ref[0]], axis=0)
    vt = jnp.concatenate([vtl_ref[0, 0], vtc_ref[0, 0]], axis=1)

    tq = q_ref.shape[1]
    units = [(r, c0) for r in range(ATT_REP) for c0 in range(0, tq, ATT_UNIT)]

    def logits(u):
        r, c0 = units[u]
        q = q_ref[0, c0:c0 + ATT_UNIT, r * ATT_HEAD_DIM:(r + 1) * ATT_HEAD_DIM]
        s_t = _dot_nt(k, q)
        return s_t, jnp.max(s_t, axis=0, keepdims=True)

    def finish(u, s_t, m, s_next):
        r, c0 = units[u]
        rows = slice(c0, c0 + ATT_UNIT)
        cols = slice(r * ATT_HEAD_DIM, (r + 1) * ATT_HEAD_DIM)
        if s_next is None:
            p_t = jnp.exp2(s_t - m).astype(BF16)
        else:
            tiles = []
            for j in range(0, s_t.shape[0], KEY_TILE):
                probe = pltpu.bitcast(s_next[j + KEY_TILE - 8:j + KEY_TILE, :], jnp.uint32)
                zero = ((probe >> 16) >> 16).astype(F32)[0:1, :]
                tiles.append(jnp.exp2(s_t[j:j + KEY_TILE, :] - (m + zero)).astype(BF16))
            p_t = jnp.concatenate(tiles, axis=0)
        o_t = _dot(vt, p_t)
        o = (o_t[:ATT_HEAD_DIM] * (1.0 / o_t[ATT_HEAD_DIM:ATT_HEAD_DIM + 1])).T
        o_ref[0, rows, cols] = (o * g_ref[0, rows, cols].astype(F32)).astype(BF16)

    pending = logits(0)
    for u in range(1, len(units)):
        nxt = logits(u)
        finish(u - 1, *pending, nxt[0])
        pending = nxt
    finish(len(units) - 1, *pending, None)


def _attention(px3, pc3, vt_lat, vt_ctx):
    b, s_len, _ = px3.shape
    c_len = pc3.shape[1]
    gw = ATT_REP * ATT_HEAD_DIM
    tq = ATT_TQ
    k_col = lambda i, g, t: (i, 0, OFF_ATT_K // ATT_HEAD_DIM + g)
    return pl.pallas_call(
        _attention_kernel,
        grid=(b, ATT_KV_HEADS, s_len // tq),
        in_specs=[pl.BlockSpec((1, tq, gw), lambda i, g, t: (i, t, OFF_ATT_Q // gw + g)),
                  pl.BlockSpec((1, s_len, ATT_HEAD_DIM), k_col),
                  pl.BlockSpec((1, c_len, ATT_HEAD_DIM), k_col),
                  pl.BlockSpec((1, 1, VT_ROWS, s_len), lambda i, g, t: (i, g, 0, 0)),
                  pl.BlockSpec((1, 1, VT_ROWS, c_len), lambda i, g, t: (i, g, 0, 0)),
                  pl.BlockSpec((1, tq, gw), lambda i, g, t: (i, t, OFF_ATT_G // gw + g))],
        out_specs=pl.BlockSpec((1, tq, gw), lambda i, g, t: (i, t, g)),
        out_shape=jax.ShapeDtypeStruct((b, s_len, ATT_Q_W), BF16),
        compiler_params=pltpu.CompilerParams(
            dimension_semantics=("parallel", "parallel", "arbitrary"),
            vmem_limit_bytes=VMEM_LIMIT),
        name="attention",
    )(px3, px3, pc3, vt_lat, vt_ctx, px3)


def _retention_kernel(d_ref, q_ref, k_ref, v_ref, g_ref, kc_ref, vc_ref, o_ref, sb_ref):
    s_len = q_ref.shape[1]
    c_len = kc_ref.shape[1]
    ch = RET_CHUNK
    n_chunks = s_len // ch

    d = d_ref[0]
    lg = jnp.log1p(-jnp.exp2(d))
    lg_f = lg[0:1, :]
    lg_b = lg[1:2, :]

    idx = lax.broadcasted_iota(jnp.int32, (ch, 1), 0).astype(F32)
    qd_f = (jnp.exp(lg_f * (idx + 1.0)) * (RET_DK ** -0.5)).astype(BF16)
    kd_f = jnp.exp(lg_f * (ch - 1.0 - idx)).astype(BF16)
    qd_b = (jnp.exp(lg_b * (ch - idx)) * (RET_DK ** -0.5)).astype(BF16)
    kd_b = jnp.exp(lg_b * idx).astype(BF16)
    cd_f = jnp.exp(lg_f * ch)
    cd_b = jnp.exp(lg_b * ch)

    row = lax.broadcasted_iota(jnp.int32, (ch, ch), 0)
    col = lax.broadcasted_iota(jnp.int32, (ch, ch), 1)
    rel = (row - col).astype(F32)
    mask = jnp.where(rel > 0, jnp.exp(lg_f * rel),
                     jnp.where(rel < 0, jnp.exp(-lg_b * rel), 2.0)) * (RET_DK ** -0.5)

    def chunk(n):
        return slice(n * ch, (n + 1) * ch)

    def masked_scores(n):
        return (_dot_nt(q_ref[0, chunk(n), :], k_ref[0, chunk(n), :]) * mask).astype(BF16)

    pos = lax.broadcasted_iota(jnp.int32, (c_len, 1), 0).astype(F32)
    kc = kc_ref[0].astype(F32)
    vc = vc_ref[0]
    s_f = _dot_tn((kc * jnp.exp(lg_f * (c_len - 1.0 - pos))).astype(BF16), vc)
    s_b = _dot_tn((kc * jnp.exp(lg_b * pos)).astype(BF16), vc)

    for n in reversed(range(n_chunks)):
        sb_ref[n] = s_b.astype(BF16)
        if n > 0:
            s_b = s_b * cd_b + _dot_tn(k_ref[0, chunk(n), :] * kd_b, v_ref[0, chunk(n), :])

    scores = masked_scores(0)
    for n in range(n_chunks):
        q = q_ref[0, chunk(n), :]
        v = v_ref[0, chunk(n), :]
        last = n + 1 == n_chunks
        next_scores = None if last else masked_scores(n + 1)
        kv_f = None if last else _dot_tn(k_ref[0, chunk(n), :] * kd_f, v)
        o = _dot(q * qd_b, sb_ref[n]) + _dot(q * qd_f, s_f.astype(BF16)) + _dot(scores, v)
        ms = jnp.mean(o * o, axis=-1, keepdims=True)
        o = (o * lax.rsqrt(ms + NORM_EPS)).astype(BF16)
        o_ref[0, chunk(n), :] = o * g_ref[0, chunk(n), :]
        if not last:
            s_f = s_f * cd_f + kv_f
            scores = next_scores


def _retention(px3, pc3, decay3):
    b, s_len, _ = px3.shape
    c_len = pc3.shape[1]
    return pl.pallas_call(
        _retention_kernel,
        grid=(b, RET_HEADS),
        in_specs=[pl.BlockSpec((1, 2, 1), lambda i, h: (h, 0, 0)),
                  pl.BlockSpec((1, s_len, RET_DK), lambda i, h: (i, 0, OFF_RET_Q // RET_DK + h)),
                  pl.BlockSpec((1, s_len, RET_DK), lambda i, h: (i, 0, OFF_RET_K // RET_DK + h)),
                  pl.BlockSpec((1, s_len, RET_DV), lambda i, h: (i, 0, OFF_RET_V // RET_DV + h)),
                  pl.BlockSpec((1, s_len, RET_DV), lambda i, h: (i, 0, OFF_RET_G // RET_DV + h)),
                  pl.BlockSpec((1, c_len, RET_DK), lambda i, h: (i, 0, h)),
                  pl.BlockSpec((1, c_len, RET_DV), lambda i, h: (i, 0, RET_QK_W // RET_DV + h))],
        out_specs=pl.BlockSpec((1, s_len, RET_DV), lambda i, h: (i, 0, h)),
        out_shape=jax.ShapeDtypeStruct((b, s_len, RET_V_W), BF16),
        scratch_shapes=[pltpu.VMEM((s_len // RET_CHUNK, RET_DK, RET_DV), BF16)],
        compiler_params=pltpu.CompilerParams(
            dimension_semantics=("parallel", "arbitrary"), vmem_limit_bytes=VMEM_LIMIT),
        name="retention",
    )(decay3, px3, px3, px3, px3, pc3, pc3)


def _merge_kernel(x_ref, yr_ref, ya_ref, mr0_ref, mr1_ref, ma0_ref, ma1_ref, gate_ref,
                  wr_ref, wa_ref, wo_ref, o_ref):
    a = _dot(yr_ref[...], wr_ref[...])
    b = _dot(ya_ref[...], wa_ref[...])
    mg_ret = jnp.concatenate([mr0_ref[...], mr1_ref[...]], axis=-1).astype(F32)
    mg_att = jnp.concatenate([ma0_ref[...], ma1_ref[...]], axis=-1).astype(F32)
    y = mg_ret * a + mg_att * b
    out = _dot(y.astype(BF16), wo_ref[...])
    o_ref[...] = x_ref[...] + gate_ref[0] * out


def _merge(x2d, y_ret, y_att, px, gate, w_o_ret, w_o_att, w_out, rows_per_mod, tm):
    rows = x2d.shape[0]
    half = D_MODEL // 2
    weight = lambda shape: pl.BlockSpec(shape, lambda i: (0, 0), pipeline_mode=pl.Buffered(1))
    mg = lambda off: pl.BlockSpec((tm, half), lambda i: (i, off // half))
    return pl.pallas_call(
        _merge_kernel,
        grid=(rows // tm,),
        in_specs=[pl.BlockSpec((tm, D_MODEL), lambda i: (i, 0)),
                  pl.BlockSpec((tm, RET_V_W), lambda i: (i, 0)),
                  pl.BlockSpec((tm, ATT_Q_W), lambda i: (i, 0)),
                  mg(OFF_MG_RET), mg(OFF_MG_RET + half), mg(OFF_MG_ATT), mg(OFF_MG_ATT + half),
                  pl.BlockSpec((1, 1, D_MODEL), lambda i: ((i * tm) // rows_per_mod, 0, 0)),
                  weight((RET_V_W, D_MODEL)), weight((ATT_Q_W, D_MODEL)), weight((D_MODEL, D_MODEL))],
        out_specs=pl.BlockSpec((tm, D_MODEL), lambda i: (i, 0)),
        out_shape=jax.ShapeDtypeStruct((rows, D_MODEL), F32),
        compiler_params=pltpu.CompilerParams(
            dimension_semantics=("parallel",), vmem_limit_bytes=VMEM_LIMIT),
        name="merge",
    )(x2d, y_ret, y_att, px, px, px, px, gate, w_o_ret, w_o_att, w_out)


def _rope_tables(s_len):
    rows = s_len // GRID_W
    row = jnp.repeat(jnp.arange(rows, dtype=F32), GRID_W)
    col = jnp.tile(jnp.arange(GRID_W, dtype=F32), rows)
    half = ATT_HEAD_DIM // 2
    freqs = ROPE_THETA ** (-jnp.arange(0, half, 2, dtype=F32) / half)
    ang = jnp.concatenate([row[:, None] * freqs, col[:, None] * freqs], axis=-1)
    cos, sin = jnp.cos(ang), jnp.sin(ang)
    return jnp.concatenate([cos, cos], axis=-1), jnp.concatenate([-sin, sin], axis=-1)


def _split_pairs(t):
    lead = t.shape[:-1]
    t = t.reshape(*lead, -1, ATT_HEAD_DIM // 2, 2)
    return jnp.swapaxes(t, -1, -2).reshape(*lead, -1)


def _layer(x, ctx, c, c_ctx, norm_w, w_ada, b_ada, w_in, ret_log2_decay,
           q_norm_w, k_norm_w, w_o_ret, w_o_att, w_out):
    b, s_len, _ = x.shape
    c_len = ctx.shape[1]

    mod = _adaln(jnp.concatenate([c, c_ctx[None]], axis=0), w_ada, b_ada)
    shift, scale, gate = (mod[:, i * D_MODEL:(i + 1) * D_MODEL] for i in range(3))
    lat = lambda t: t[:b].reshape(b, 1, D_MODEL)
    cx = lambda t: t[b:].reshape(1, 1, D_MODEL)

    w_bf = w_in.astype(BF16)
    for off, width in ((OFF_ATT_K, ATT_KV_W), (OFF_ATT_Q, ATT_Q_W)):
        w_bf = w_bf.at[:, off:off + width].set(_split_pairs(w_bf[:, off:off + width]))
    q_norm_w = _split_pairs(q_norm_w)
    k_norm_w = _split_pairs(k_norm_w)

    cos, sin_signed = _rope_tables(s_len)
    x2d = x.reshape(b * s_len, D_MODEL)
    px, vt_lat = _inproj_rows(x2d, norm_w, lat(scale), lat(shift), w_bf, q_norm_w, k_norm_w,
                              cos, sin_signed, IN_COLS, s_len, s_len, PROJ_TM, PROJ_TN, rope=True)
    pc, vt_ctx = _inproj_rows(ctx.reshape(b * c_len, D_MODEL), norm_w, cx(scale), cx(shift), w_bf,
                              q_norm_w, k_norm_w, cos, sin_signed,
                              KV_COLS, b * c_len, c_len, PROJ_TM, PROJ_TN_CTX, rope=False)
    px3 = px.reshape(b, s_len, IN_COLS)
    pc3 = pc.reshape(b, c_len, KV_COLS)

    y_att = _attention(px3, pc3, vt_lat, vt_ctx)

    decay3 = ret_log2_decay.astype(F32).T.reshape(RET_HEADS, 2, 1)
    y_ret = _retention(px3, pc3, decay3)

    x_new = _merge(x2d, y_ret.reshape(b * s_len, RET_V_W), y_att.reshape(b * s_len, ATT_Q_W), px,
                   lat(gate), w_o_ret.astype(BF16), w_o_att.astype(BF16), w_out.astype(BF16),
                   s_len, MERGE_TM)
    return x_new.reshape(b, s_len, D_MODEL)


def kernel(x, c, ctx, c_ctx, norm_w, w_ada, b_ada, w_in, ret_log2_decay, q_norm_w, k_norm_w,
           w_o_ret, w_o_att, w_out):
    depth = norm_w.shape[0]
    assert depth == 1, "context update between layers is not implemented"
    return _layer(x, ctx, c, c_ctx, norm_w[0], w_ada[0], b_ada[0], w_in[0], ret_log2_decay[0],
                  q_norm_w[0], k_norm_w[0], w_o_ret[0], w_o_att[0], w_out[0])
```

```python
import functools

import jax
import jax.numpy as jnp
from jax import lax
from jax.experimental import pallas as pl
from jax.experimental.pallas import tpu as pltpu

D_MODEL = 1024
GRID_W = 64
RET_HEADS = 4
RET_DK = 256
RET_DV = 512
RET_QK_W = RET_HEADS * RET_DK
RET_V_W = RET_HEADS * RET_DV
ATT_HEADS = 8
ATT_KV_HEADS = 2
ATT_HEAD_DIM = 128
ATT_Q_W = ATT_HEADS * ATT_HEAD_DIM
ATT_KV_W = ATT_KV_HEADS * ATT_HEAD_DIM
ATT_REP = ATT_HEADS // ATT_KV_HEADS
ROPE_THETA = 10000.0
NORM_EPS = 1e-6
KV_COLS = RET_QK_W + RET_V_W + 2 * ATT_KV_W
LOG2E = 1.4426950408889634

OFF_RET_K = 0
OFF_RET_V = OFF_RET_K + RET_QK_W
OFF_ATT_K = OFF_RET_V + RET_V_W
OFF_ATT_V = OFF_ATT_K + ATT_KV_W
OFF_RET_Q = OFF_ATT_V + ATT_KV_W
OFF_RET_G = OFF_RET_Q + RET_QK_W
OFF_ATT_Q = OFF_RET_G + RET_V_W
OFF_ATT_G = OFF_ATT_Q + ATT_Q_W
OFF_MG_RET = OFF_ATT_G + ATT_Q_W
OFF_MG_ATT = OFF_MG_RET + D_MODEL
IN_COLS = OFF_MG_ATT + D_MODEL
assert OFF_RET_Q == KV_COLS

RET_CHUNK = 256
PROJ_TM = 512
PROJ_TN = 1536
PROJ_TN_CTX = 1792
MERGE_TM = 1024
BF16_SUBLANES = 16
VT_ROWS = ATT_HEAD_DIM + BF16_SUBLANES
VMEM_LIMIT = 56 * 1024 * 1024

BF16 = jnp.bfloat16
F32 = jnp.float32


def _dot(a, b):
    return jnp.dot(a, b, preferred_element_type=F32)


def _dot_nt(a, b):
    return lax.dot_general(a, b, (((1,), (1,)), ((), ())), preferred_element_type=F32)


def _dot_tn(a, b):
    return lax.dot_general(a, b, (((0,), (0,)), ((), ())), preferred_element_type=F32)


def _silu(x):
    return x * jax.nn.sigmoid(x)


def _silu_tanh(x):
    h = 0.5 * x
    return h * jnp.tanh(h) + h


def _sigmoid_tanh(x):
    return 0.5 * jnp.tanh(0.5 * x) + 0.5


GATE_COLS = ((OFF_RET_G, OFF_RET_G + RET_V_W, _silu_tanh),
             (OFF_ATT_G, OFF_ATT_G + ATT_Q_W, _silu_tanh),
             (OFF_MG_RET, OFF_MG_ATT + D_MODEL, _sigmoid_tanh))


def _adaln_kernel(c_ref, w_ref, b_ref, o_ref):
    o_ref[...] = _dot(_silu(c_ref[...]), w_ref[...]) + b_ref[...]


def _adaln(cc, w_ada, b_ada):
    rows = cc.shape[0]
    tn = D_MODEL
    return pl.pallas_call(
        _adaln_kernel,
        grid=(3 * D_MODEL // tn,),
        in_specs=[pl.BlockSpec((rows, D_MODEL), lambda j: (0, 0)),
                  pl.BlockSpec((D_MODEL, tn), lambda j: (0, j)),
                  pl.BlockSpec((1, tn), lambda j: (0, j))],
        out_specs=pl.BlockSpec((rows, tn), lambda j: (0, j)),
        out_shape=jax.ShapeDtypeStruct((rows, 3 * D_MODEL), F32),
        compiler_params=pltpu.CompilerParams(vmem_limit_bytes=VMEM_LIMIT),
        name="adaln",
    )(cc, w_ada, b_ada.reshape(1, -1))


def _head_rms(t, w):
    ms = jnp.mean(t * t, axis=-1, keepdims=True)
    return t * lax.rsqrt(ms + NORM_EPS) * w


def _rope(t, cos, sin_signed):
    return t * cos + pltpu.roll(t, ATT_HEAD_DIM // 2, 1) * sin_signed


def _inproj_rows_kernel(x_ref, nw_ref, sc_ref, sh_ref, w_ref, qw_ref, kw_ref, cos_ref, sin_ref,
                        o_ref, vt_ref, *, tn, rope, prep_q):
    x = x_ref[...]
    ms = jnp.mean(x * x, axis=-1, keepdims=True)
    y = x * lax.rsqrt(ms + NORM_EPS) * nw_ref[...]
    h = (y * (1.0 + sc_ref[0]) + sh_ref[0]).astype(BF16)
    n_cols = o_ref.shape[1]
    heads = [(OFF_ATT_K + g * ATT_HEAD_DIM, kw_ref[...]) for g in range(ATT_KV_HEADS)]
    if prep_q:
        qw = qw_ref[...] * (ATT_HEAD_DIM ** -0.5 * LOG2E)
        heads += [(OFF_ATT_Q + r * ATT_HEAD_DIM, qw) for r in range(ATT_HEADS)]
    for c0 in range(0, n_cols, tn):
        res = _dot(h, w_ref[:, c0:c0 + tn])
        edges = sorted({c0, c0 + tn} | {e for lo, hi, _ in GATE_COLS for e in (lo, hi) if c0 < e < c0 + tn})
        for lo, hi in zip(edges[:-1], edges[1:]):
            piece = res[:, lo - c0:hi - c0]
            for g_lo, g_hi, act in GATE_COLS:
                if g_lo <= lo < g_hi:
                    piece = act(piece)
            o_ref[:, lo:hi] = piece.astype(BF16)
        for col, w in heads:
            if c0 <= col < c0 + tn:
                t = _head_rms(res[:, col - c0:col - c0 + ATT_HEAD_DIM], w)
                if rope:
                    t = _rope(t, cos_ref[...], sin_ref[...])
                o_ref[:, col:col + ATT_HEAD_DIM] = t.astype(BF16)
        for g in range(ATT_KV_HEADS):
            col = OFF_ATT_V + g * ATT_HEAD_DIM
            if c0 <= col < c0 + tn:
                n_b, _, _, n_tok = vt_ref.shape
                v = res[:, col - c0:col - c0 + ATT_HEAD_DIM]
                for bb in range(n_b):
                    vt_ref[bb, g, :ATT_HEAD_DIM, :] = v[bb * n_tok:(bb + 1) * n_tok].T.astype(BF16)
                    vt_ref[bb, g, ATT_HEAD_DIM:, :] = jnp.ones((VT_ROWS - ATT_HEAD_DIM, n_tok), BF16)


def _inproj_rows(x2d, norm_w, scale, shift, w_bf, q_norm_w, k_norm_w, cos, sin_signed,
                 n_cols, rows_per_mod, tokens, tm, tn, rope):
    rows = x2d.shape[0]
    if tokens >= tm:
        per_batch = tokens // tm
        vt_block = (1, ATT_KV_HEADS, VT_ROWS, tm)
        vt_map = lambda i: (i // per_batch, 0, 0, i % per_batch)
    else:
        vt_block = (tm // tokens, ATT_KV_HEADS, VT_ROWS, tokens)
        vt_map = lambda i: (i, 0, 0, 0)
    pos_tiles = cos.shape[0] // tm
    mod_map = lambda i: ((i * tm) // rows_per_mod, 0, 0)
    vec = lambda i: (0, 0)
    return pl.pallas_call(
        functools.partial(_inproj_rows_kernel, tn=tn, rope=rope, prep_q=n_cols > OFF_ATT_Q),
        grid=(rows // tm,),
        in_specs=[pl.BlockSpec((tm, D_MODEL), lambda i: (i, 0)),
                  pl.BlockSpec((1, D_MODEL), vec),
                  pl.BlockSpec((1, 1, D_MODEL), mod_map),
                  pl.BlockSpec((1, 1, D_MODEL), mod_map),
                  pl.BlockSpec((D_MODEL, n_cols), vec, pipeline_mode=pl.Buffered(1)),
                  pl.BlockSpec((1, ATT_HEAD_DIM), vec),
                  pl.BlockSpec((1, ATT_HEAD_DIM), vec),
                  pl.BlockSpec((tm, ATT_HEAD_DIM), lambda i: (i % pos_tiles, 0)),
                  pl.BlockSpec((tm, ATT_HEAD_DIM), lambda i: (i % pos_tiles, 0))],
        out_specs=[pl.BlockSpec((tm, n_cols), lambda i: (i, 0)), pl.BlockSpec(vt_block, vt_map)],
        out_shape=[jax.ShapeDtypeStruct((rows, n_cols), BF16),
                   jax.ShapeDtypeStruct((rows // tokens, ATT_KV_HEADS, VT_ROWS, tokens), BF16)],
        compiler_params=pltpu.CompilerParams(
            dimension_semantics=("parallel",), vmem_limit_bytes=VMEM_LIMIT),
        name="inproj_rows",
    )(x2d, norm_w.reshape(1, -1), scale, shift, w_bf, q_norm_w.reshape(1, -1),
      k_norm_w.reshape(1, -1), cos, sin_signed)


KEY_TILE = 128
ATT_UNIT = 512
ATT_TQ = 2048


def _attention_kernel(q_ref, kl_ref, kc_ref, vtl_ref, vtc_ref, g_ref, o_ref):
    k = jnp.concatenate([kl_ref[0], kc_ref[0]], axis=0)
    vt = jnp.concatenate([vtl_ref[0, 0], vtc_ref[0, 0]], axis=1)

    tq = q_ref.shape[1]
    units = [(r, c0) for r in range(ATT_REP) for c0 in range(0, tq, ATT_UNIT)]

    def logits(u):
        r, c0 = units[u]
        q = q_ref[0, c0:c0 + ATT_UNIT, r * ATT_HEAD_DIM:(r + 1) * ATT_HEAD_DIM]
        s_t = _dot_nt(k, q)
        return s_t, jnp.max(s_t, axis=0, keepdims=True)

    def finish(u, s_t, m, s_next):
        r, c0 = units[u]
        rows = slice(c0, c0 + ATT_UNIT)
        cols = slice(r * ATT_HEAD_DIM, (r + 1) * ATT_HEAD_DIM)
        if s_next is None:
            p_t = jnp.exp2(s_t - m).astype(BF16)
        else:
            tiles = []
            for j in range(0, s_t.shape[0], KEY_TILE):
                probe = pltpu.bitcast(s_next[j + KEY_TILE - 8:j + KEY_TILE, :], jnp.uint32)
                zero = ((probe >> 16) >> 16).astype(F32)[0:1, :]
                tiles.append(jnp.exp2(s_t[j:j + KEY_TILE, :] - (m + zero)).astype(BF16))
            p_t = jnp.concatenate(tiles, axis=0)
        o_t = _dot(vt, p_t)
        o = (o_t[:ATT_HEAD_DIM] * (1.0 / o_t[ATT_HEAD_DIM:ATT_HEAD_DIM + 1])).T
        o_ref[0, rows, cols] = (o * g_ref[0, rows, cols].astype(F32)).astype(BF16)

    pending = logits(0)
    for u in range(1, len(units)):
        nxt = logits(u)
        finish(u - 1, *pending, nxt[0])
        pending = nxt
    finish(len(units) - 1, *pending, None)


def _attention(px3, pc3, vt_lat, vt_ctx):
    b, s_len, _ = px3.shape
    c_len = pc3.shape[1]
    gw = ATT_REP * ATT_HEAD_DIM
    tq = ATT_TQ
    k_col = lambda i, g, t: (i, 0, OFF_ATT_K // ATT_HEAD_DIM + g)
    return pl.pallas_call(
        _attention_kernel,
        grid=(b, ATT_KV_HEADS, s_len // tq),
        in_specs=[pl.BlockSpec((1, tq, gw), lambda i, g, t: (i, t, OFF_ATT_Q // gw + g)),
                  pl.BlockSpec((1, s_len, ATT_HEAD_DIM), k_col),
                  pl.BlockSpec((1, c_len, ATT_HEAD_DIM), k_col),
                  pl.BlockSpec((1, 1, VT_ROWS, s_len), lambda i, g, t: (i, g, 0, 0)),
                  pl.BlockSpec((1, 1, VT_ROWS, c_len), lambda i, g, t: (i, g, 0, 0)),
                  pl.BlockSpec((1, tq, gw), lambda i, g, t: (i, t, OFF_ATT_G // gw + g))],
        out_specs=pl.BlockSpec((1, tq, gw), lambda i, g, t: (i, t, g)),
        out_shape=jax.ShapeDtypeStruct((b, s_len, ATT_Q_W), BF16),
        compiler_params=pltpu.CompilerParams(
            dimension_semantics=("parallel", "parallel", "arbitrary"),
            vmem_limit_bytes=VMEM_LIMIT),
        name="attention",
    )(px3, px3, pc3, vt_lat, vt_ctx, px3)


def _retention_kernel(d_ref, q_ref, k_ref, v_ref, g_ref, kc_ref, vc_ref, o_ref, sb_ref):
    s_len = q_ref.shape[1]
    c_len = kc_ref.shape[1]
    ch = RET_CHUNK
    n_chunks = s_len // ch

    d = d_ref[0]
    lg = jnp.log1p(-jnp.exp2(d))
    lg_f = lg[0:1, :]
    lg_b = lg[1:2, :]

    idx = lax.broadcasted_iota(jnp.int32, (ch, 1), 0).astype(F32)
    qd_f = (jnp.exp(lg_f * (idx + 1.0)) * (RET_DK ** -0.5)).astype(BF16)
    kd_f = jnp.exp(lg_f * (ch - 1.0 - idx)).astype(BF16)
    qd_b = (jnp.exp(lg_b * (ch - idx)) * (RET_DK ** -0.5)).astype(BF16)
    kd_b = jnp.exp(lg_b * idx).astype(BF16)
    cd_f = jnp.exp(lg_f * ch)
    cd_b = jnp.exp(lg_b * ch)

    row = lax.broadcasted_iota(jnp.int32, (ch, ch), 0)
    col = lax.broadcasted_iota(jnp.int32, (ch, ch), 1)
    rel = (row - col).astype(F32)
    mask = jnp.where(rel > 0, jnp.exp(lg_f * rel),
                     jnp.where(rel < 0, jnp.exp(-lg_b * rel), 2.0)) * (RET_DK ** -0.5)

    def chunk(n):
        return slice(n * ch, (n + 1) * ch)

    def masked_scores(n):
        return (_dot_nt(q_ref[0, chunk(n), :], k_ref[0, chunk(n), :]) * mask).astype(BF16)

    pos = lax.broadcasted_iota(jnp.int32, (c_len, 1), 0).astype(F32)
    kc = kc_ref[0].astype(F32)
    vc = vc_ref[0]
    s_f = _dot_tn((kc * jnp.exp(lg_f * (c_len - 1.0 - pos))).astype(BF16), vc)
    s_b = _dot_tn((kc * jnp.exp(lg_b * pos)).astype(BF16), vc)

    for n in reversed(range(n_chunks)):
        sb_ref[n] = s_b.astype(BF16)
        if n > 0:
            s_b = s_b * cd_b + _dot_tn(k_ref[0, chunk(n), :] * kd_b, v_ref[0, chunk(n), :])

    scores = masked_scores(0)
    for n in range(n_chunks):
        q = q_ref[0, chunk(n), :]
        v = v_ref[0, chunk(n), :]
        last = n + 1 == n_chunks
        next_scores = None if last else masked_scores(n + 1)
        kv_f = None if last else _dot_tn(k_ref[0, chunk(n), :] * kd_f, v)
        o = _dot(q * qd_b, sb_ref[n]) + _dot(q * qd_f, s_f.astype(BF16)) + _dot(scores, v)
        ms = jnp.mean(o * o, axis=-1, keepdims=True)
        o = (o * lax.rsqrt(ms + NORM_EPS)).astype(BF16)
        o_ref[0, chunk(n), :] = o * g_ref[0, chunk(n), :]
        if not last:
            s_f = s_f * cd_f + kv_f
            scores = next_scores


def _retention(px3, pc3, decay3):
    b, s_len, _ = px3.shape
    c_len = pc3.shape[1]
    return pl.pallas_call(
        _retention_kernel,
        grid=(b, RET_HEADS),
        in_specs=[pl.BlockSpec((1, 2, 1), lambda i, h: (h, 0, 0)),
                  pl.BlockSpec((1, s_len, RET_DK), lambda i, h: (i, 0, OFF_RET_Q // RET_DK + h)),
                  pl.BlockSpec((1, s_len, RET_DK), lambda i, h: (i, 0, OFF_RET_K // RET_DK + h)),
                  pl.BlockSpec((1, s_len, RET_DV), lambda i, h: (i, 0, OFF_RET_V // RET_DV + h)),
                  pl.BlockSpec((1, s_len, RET_DV), lambda i, h: (i, 0, OFF_RET_G // RET_DV + h)),
                  pl.BlockSpec((1, c_len, RET_DK), lambda i, h: (i, 0, h)),
                  pl.BlockSpec((1, c_len, RET_DV), lambda i, h: (i, 0, RET_QK_W // RET_DV + h))],
        out_specs=pl.BlockSpec((1, s_len, RET_DV), lambda i, h: (i, 0, h)),
        out_shape=jax.ShapeDtypeStruct((b, s_len, RET_V_W), BF16),
        scratch_shapes=[pltpu.VMEM((s_len // RET_CHUNK, RET_DK, RET_DV), BF16)],
        compiler_params=pltpu.CompilerParams(
            dimension_semantics=("parallel", "arbitrary"), vmem_limit_bytes=VMEM_LIMIT),
        name="retention",
    )(decay3, px3, px3, px3, px3, pc3, pc3)


def _merge_kernel(x_ref, yr_ref, ya_ref, mr0_ref, mr1_ref, ma0_ref, ma1_ref, gate_ref,
                  wr_ref, wa_ref, wo_ref, o_ref):
    a = _dot(yr_ref[...], wr_ref[...])
    b = _dot(ya_ref[...], wa_ref[...])
    mg_ret = jnp.concatenate([mr0_ref[...], mr1_ref[...]], axis=-1).astype(F32)
    mg_att = jnp.concatenate([ma0_ref[...], ma1_ref[...]], axis=-1).astype(F32)
    y = mg_ret * a + mg_att * b
    out = _dot(y.astype(BF16), wo_ref[...])
    o_ref[...] = x_ref[...] + gate_ref[0] * out


def _merge(x2d, y_ret, y_att, px, gate, w_o_ret, w_o_att, w_out, rows_per_mod, tm):
    rows = x2d.shape[0]
    half = D_MODEL // 2
    weight = lambda shape: pl.BlockSpec(shape, lambda i: (0, 0), pipeline_mode=pl.Buffered(1))
    mg = lambda off: pl.BlockSpec((tm, half), lambda i: (i, off // half))
    return pl.pallas_call(
        _merge_kernel,
        grid=(rows // tm,),
        in_specs=[pl.BlockSpec((tm, D_MODEL), lambda i: (i, 0)),
                  pl.BlockSpec((tm, RET_V_W), lambda i: (i, 0)),
                  pl.BlockSpec((tm, ATT_Q_W), lambda i: (i, 0)),
                  mg(OFF_MG_RET), mg(OFF_MG_RET + half), mg(OFF_MG_ATT), mg(OFF_MG_ATT + half),
                  pl.BlockSpec((1, 1, D_MODEL), lambda i: ((i * tm) // rows_per_mod, 0, 0)),
                  weight((RET_V_W, D_MODEL)), weight((ATT_Q_W, D_MODEL)), weight((D_MODEL, D_MODEL))],
        out_specs=pl.BlockSpec((tm, D_MODEL), lambda i: (i, 0)),
        out_shape=jax.ShapeDtypeStruct((rows, D_MODEL), F32),
        compiler_params=pltpu.CompilerParams(
            dimension_semantics=("parallel",), vmem_limit_bytes=VMEM_LIMIT),
        name="merge",
    )(x2d, y_ret, y_att, px, px, px, px, gate, w_o_ret, w_o_att, w_out)


def _rope_tables(s_len):
    rows = s_len // GRID_W
    row = jnp.repeat(jnp.arange(rows, dtype=F32), GRID_W)
    col = jnp.tile(jnp.arange(GRID_W, dtype=F32), rows)
    half = ATT_HEAD_DIM // 2
    freqs = ROPE_THETA ** (-jnp.arange(0, half, 2, dtype=F32) / half)
    ang = jnp.concatenate([row[:, None] * freqs, col[:, None] * freqs], axis=-1)
    cos, sin = jnp.cos(ang), jnp.sin(ang)
    return jnp.concatenate([cos, cos], axis=-1), jnp.concatenate([-sin, sin], axis=-1)


def _split_pairs(t):
    lead = t.shape[:-1]
    t = t.reshape(*lead, -1, ATT_HEAD_DIM // 2, 2)
    return jnp.swapaxes(t, -1, -2).reshape(*lead, -1)


def _layer(x, ctx, c, c_ctx, norm_w, w_ada, b_ada, w_in, ret_log2_decay,
           q_norm_w, k_norm_w, w_o_ret, w_o_att, w_out):
    b, s_len, _ = x.shape
    c_len = ctx.shape[1]

    mod = _adaln(jnp.concatenate([c, c_ctx[None]], axis=0), w_ada, b_ada)
    shift, scale, gate = (mod[:, i * D_MODEL:(i + 1) * D_MODEL] for i in range(3))
    lat = lambda t: t[:b].reshape(b, 1, D_MODEL)
    cx = lambda t: t[b:].reshape(1, 1, D_MODEL)

    w_bf = w_in.astype(BF16)
    for off, width in ((OFF_ATT_K, ATT_KV_W), (OFF_ATT_Q, ATT_Q_W)):
        w_bf = w_bf.at[:, off:off + width].set(_split_pairs(w_bf[:, off:off + width]))
    q_norm_w = _split_pairs(q_norm_w)
    k_norm_w = _split_pairs(k_norm_w)

    cos, sin_signed = _rope_tables(s_len)
    x2d = x.reshape(b * s_len, D_MODEL)
    px, vt_lat = _inproj_rows(x2d, norm_w, lat(scale), lat(shift), w_bf, q_norm_w, k_norm_w,
                              cos, sin_signed, IN_COLS, s_len, s_len, PROJ_TM, PROJ_TN, rope=True)
    pc, vt_ctx = _inproj_rows(ctx.reshape(b * c_len, D_MODEL), norm_w, cx(scale), cx(shift), w_bf,
                              q_norm_w, k_norm_w, cos, sin_signed,
                              KV_COLS, b * c_len, c_len, PROJ_TM, PROJ_TN_CTX, rope=False)
    px3 = px.reshape(b, s_len, IN_COLS)
    pc3 = pc.reshape(b, c_len, KV_COLS)

    y_att = _attention(px3, pc3, vt_lat, vt_ctx)

    decay3 = ret_log2_decay.astype(F32).T.reshape(RET_HEADS, 2, 1)
    y_ret = _retention(px3, pc3, decay3)

    x_new = _merge(x2d, y_ret.reshape(b * s_len, RET_V_W), y_att.reshape(b * s_len, ATT_Q_W), px,
                   lat(gate), w_o_ret.astype(BF16), w_o_att.astype(BF16), w_out.astype(BF16),
                   s_len, MERGE_TM)
    return x_new.reshape(b, s_len, D_MODEL)


def kernel(x, c, ctx, c_ctx, norm_w, w_ada, b_ada, w_in, ret_log2_decay, q_norm_w, k_norm_w,
           w_o_ret, w_o_att, w_out):
    depth = norm_w.shape[0]
    assert depth == 1, "context update between layers is not implemented"
    return _layer(x, ctx, c, c_ctx, norm_w[0], w_ada[0], b_ada[0], w_in[0], ret_log2_decay[0],
                  q_norm_w[0], k_norm_w[0], w_o_ret[0], w_o_att[0], w_out[0])
```

```python
import functools

import jax
import jax.numpy as jnp
from jax import lax
from jax.experimental import pallas as pl
from jax.experimental.pallas import tpu as pltpu

D_MODEL = 1024
GRID_W = 64
RET_HEADS = 4
RET_DK = 256
RET_DV = 512
RET_QK_W = RET_HEADS * RET_DK
RET_V_W = RET_HEADS * RET_DV
ATT_HEADS = 8
ATT_KV_HEADS = 2
ATT_HEAD_DIM = 128
ATT_Q_W = ATT_HEADS * ATT_HEAD_DIM
ATT_KV_W = ATT_KV_HEADS * ATT_HEAD_DIM
ATT_REP = ATT_HEADS // ATT_KV_HEADS
ROPE_THETA = 10000.0
NORM_EPS = 1e-6
KV_COLS = RET_QK_W + RET_V_W + 2 * ATT_KV_W
LOG2E = 1.4426950408889634

OFF_RET_K = 0
OFF_RET_V = OFF_RET_K + RET_QK_W
OFF_ATT_K = OFF_RET_V + RET_V_W
OFF_ATT_V = OFF_ATT_K + ATT_KV_W
OFF_RET_Q = OFF_ATT_V + ATT_KV_W
OFF_RET_G = OFF_RET_Q + RET_QK_W
OFF_ATT_Q = OFF_RET_G + RET_V_W
OFF_ATT_G = OFF_ATT_Q + ATT_Q_W
OFF_MG_RET = OFF_ATT_G + ATT_Q_W
OFF_MG_ATT = OFF_MG_RET + D_MODEL
IN_COLS = OFF_MG_ATT + D_MODEL
assert OFF_RET_Q == KV_COLS

RET_CHUNK = 256
PROJ_TM = 512
PROJ_TN = 1536
PROJ_TN_CTX = 1792
MERGE_TM = 1024
BF16_SUBLANES = 16
VT_ROWS = ATT_HEAD_DIM + BF16_SUBLANES
VMEM_LIMIT = 56 * 1024 * 1024

BF16 = jnp.bfloat16
F32 = jnp.float32


def _dot(a, b):
    return jnp.dot(a, b, preferred_element_type=F32)


def _dot_nt(a, b):
    return lax.dot_general(a, b, (((1,), (1,)), ((), ())), preferred_element_type=F32)


def _dot_tn(a, b):
    return lax.dot_general(a, b, (((0,), (0,)), ((), ())), preferred_element_type=F32)


def _silu(x):
    return x * jax.nn.sigmoid(x)


def _silu_tanh(x):
    h = 0.5 * x
    return h * jnp.tanh(h) + h


def _sigmoid_tanh(x):
    return 0.5 * jnp.tanh(0.5 * x) + 0.5


GATE_COLS = ((OFF_RET_G, OFF_RET_G + RET_V_W, _silu_tanh),
             (OFF_ATT_G, OFF_ATT_G + ATT_Q_W, _silu_tanh),
             (OFF_MG_RET, OFF_MG_ATT + D_MODEL, _sigmoid_tanh))


def _adaln_kernel(c_ref, w_ref, b_ref, o_ref):
    o_ref[...] = _dot(_silu(c_ref[...]), w_ref[...]) + b_ref[...]


def _adaln(cc, w_ada, b_ada):
    rows = cc.shape[0]
    tn = D_MODEL
    return pl.pallas_call(
        _adaln_kernel,
        grid=(3 * D_MODEL // tn,),
        in_specs=[pl.BlockSpec((rows, D_MODEL), lambda j: (0, 0)),
                  pl.BlockSpec((D_MODEL, tn), lambda j: (0, j)),
                  pl.BlockSpec((1, tn), lambda j: (0, j))],
        out_specs=pl.BlockSpec((rows, tn), lambda j: (0, j)),
        out_shape=jax.ShapeDtypeStruct((rows, 3 * D_MODEL), F32),
        compiler_params=pltpu.CompilerParams(vmem_limit_bytes=VMEM_LIMIT),
        name="adaln",
    )(cc, w_ada, b_ada.reshape(1, -1))


def _head_rms(t, w):
    ms = jnp.mean(t * t, axis=-1, keepdims=True)
    return t * lax.rsqrt(ms + NORM_EPS) * w


def _rope(t, cos, sin_signed):
    return t * cos + pltpu.roll(t, ATT_HEAD_DIM // 2, 1) * sin_signed


def _inproj_rows_kernel(x_ref, nw_ref, sc_ref, sh_ref, w_ref, qw_ref, kw_ref, cos_ref, sin_ref,
                        o_ref, vt_ref, *, tn, rope, prep_q):
    x = x_ref[...]
    ms = jnp.mean(x * x, axis=-1, keepdims=True)
    y = x * lax.rsqrt(ms + NORM_EPS) * nw_ref[...]
    h = (y * (1.0 + sc_ref[0]) + sh_ref[0]).astype(BF16)
    n_cols = o_ref.shape[1]
    heads = [(OFF_ATT_K + g * ATT_HEAD_DIM, kw_ref[...]) for g in range(ATT_KV_HEADS)]
    if prep_q:
        qw = qw_ref[...] * (ATT_HEAD_DIM ** -0.5 * LOG2E)
        heads += [(OFF_ATT_Q + r * ATT_HEAD_DIM, qw) for r in range(ATT_HEADS)]
    for c0 in range(0, n_cols, tn):
        res = _dot(h, w_ref[:, c0:c0 + tn])
        edges = sorted({c0, c0 + tn} | {e for lo, hi, _ in GATE_COLS for e in (lo, hi) if c0 < e < c0 + tn})
        for lo, hi in zip(edges[:-1], edges[1:]):
            piece = res[:, lo - c0:hi - c0]
            for g_lo, g_hi, act in GATE_COLS:
                if g_lo <= lo < g_hi:
                    piece = act(piece)
            o_ref[:, lo:hi] = piece.astype(BF16)
        for col, w in heads:
            if c0 <= col < c0 + tn:
                t = _head_rms(res[:, col - c0:col - c0 + ATT_HEAD_DIM], w)
                if rope:
                    t = _rope(t, cos_ref[...], sin_ref[...])
                o_ref[:, col:col + ATT_HEAD_DIM] = t.astype(BF16)
        for g in range(ATT_KV_HEADS):
            col = OFF_ATT_V + g * ATT_HEAD_DIM
            if c0 <= col < c0 + tn:
                n_b, _, _, n_tok = vt_ref.shape
                v = res[:, col - c0:col - c0 + ATT_HEAD_DIM]
                for bb in range(n_b):
                    vt_ref[bb, g, :ATT_HEAD_DIM, :] = v[bb * n_tok:(bb + 1) * n_tok].T.astype(BF16)
                    vt_ref[bb, g, ATT_HEAD_DIM:, :] = jnp.ones((VT_ROWS - ATT_HEAD_DIM, n_tok), BF16)


def _inproj_rows(x2d, norm_w, scale, shift, w_bf, q_norm_w, k_norm_w, cos, sin_signed,
                 n_cols, rows_per_mod, tokens, tm, tn, rope):
    rows = x2d.shape[0]
    if tokens >= tm:
        per_batch = tokens // tm
        vt_block = (1, ATT_KV_HEADS, VT_ROWS, tm)
        vt_map = lambda i: (i // per_batch, 0, 0, i % per_batch)
    else:
        vt_block = (tm // tokens, ATT_KV_HEADS, VT_ROWS, tokens)
        vt_map = lambda i: (i, 0, 0, 0)
    pos_tiles = cos.shape[0] // tm
    mod_map = lambda i: ((i * tm) // rows_per_mod, 0, 0)
    vec = lambda i: (0, 0)
    return pl.pallas_call(
        functools.partial(_inproj_rows_kernel, tn=tn, rope=rope, prep_q=n_cols > OFF_ATT_Q),
        grid=(rows // tm,),
        in_specs=[pl.BlockSpec((tm, D_MODEL), lambda i: (i, 0)),
                  pl.BlockSpec((1, D_MODEL), vec),
                  pl.BlockSpec((1, 1, D_MODEL), mod_map),
                  pl.BlockSpec((1, 1, D_MODEL), mod_map),
                  pl.BlockSpec((D_MODEL, n_cols), vec, pipeline_mode=pl.Buffered(1)),
                  pl.BlockSpec((1, ATT_HEAD_DIM), vec),
                  pl.BlockSpec((1, ATT_HEAD_DIM), vec),
                  pl.BlockSpec((tm, ATT_HEAD_DIM), lambda i: (i % pos_tiles, 0)),
                  pl.BlockSpec((tm, ATT_HEAD_DIM), lambda i: (i % pos_tiles, 0))],
        out_specs=[pl.BlockSpec((tm, n_cols), lambda i: (i, 0)), pl.BlockSpec(vt_block, vt_map)],
        out_shape=[jax.ShapeDtypeStruct((rows, n_cols), BF16),
                   jax.ShapeDtypeStruct((rows // tokens, ATT_KV_HEADS, VT_ROWS, tokens), BF16)],
        compiler_params=pltpu.CompilerParams(
            dimension_semantics=("parallel",), vmem_limit_bytes=VMEM_LIMIT),
        name="inproj_rows",
    )(x2d, norm_w.reshape(1, -1), scale, shift, w_bf, q_norm_w.reshape(1, -1),
      k_norm_w.reshape(1, -1), cos, sin_signed)


KEY_TILE = 128
ATT_UNIT = 512
ATT_TQ = 2048


def _attention_kernel(q_ref, kl_ref, kc_ref, vtl_ref, vtc_ref, g_ref, o_ref):
    k = jnp.concatenate([kl_ref[0], kc_ref[0]], axis=0)
    vt = jnp.concatenate([vtl_ref[0, 0], vtc_ref[0, 0]], axis=1)

    tq = q_ref.shape[1]
    units = [(r, c0) for r in range(ATT_REP) for c0 in range(0, tq, ATT_UNIT)]

    def logits(u):
        r, c0 = units[u]
        q = q_ref[0, c0:c0 + ATT_UNIT, r * ATT_HEAD_DIM:(r + 1) * ATT_HEAD_DIM]
        s_t = _dot_nt(k, q)
        return s_t, jnp.max(s_t, axis=0, keepdims=True)

    def finish(u, s_t, m, s_next):
        r, c0 = units[u]
        rows = slice(c0, c0 + ATT_UNIT)
        cols = slice(r * ATT_HEAD_DIM, (r + 1) * ATT_HEAD_DIM)
        if s_next is None:
            p_t = jnp.exp2(s_t - m).astype(BF16)
        else:
            tiles = []
            for j in range(0, s_t.shape[0], KEY_TILE):
                probe = pltpu.bitcast(s_next[j + KEY_TILE - 8:j + KEY_TILE, :], jnp.uint32)
                zero = ((probe >> 16) >> 16).astype(F32)[0:1, :]
                tiles.append(jnp.exp2(s_t[j:j + KEY_TILE, :] - (m + zero)).astype(BF16))
            p_t = jnp.concatenate(tiles, axis=0)
        o_t = _dot(vt, p_t)
        o = (o_t[:ATT_HEAD_DIM] * (1.0 / o_t[ATT_HEAD_DIM:ATT_HEAD_DIM + 1])).T
        o_ref[0, rows, cols] = (o * g_ref[0, rows, cols].astype(F32)).astype(BF16)

    pending = logits(0)
    for u in range(1, len(units)):
        nxt = logits(u)
        finish(u - 1, *pending, nxt[0])
        pending = nxt
    finish(len(units) - 1, *pending, None)


def _attention(px3, pc3, vt_lat, vt_ctx):
    b, s_len, _ = px3.shape
    c_len = pc3.shape[1]
    gw = ATT_REP * ATT_HEAD_DIM
    tq = ATT_TQ
    k_col = lambda i, g, t: (i, 0, OFF_ATT_K // ATT_HEAD_DIM + g)
    return pl.pallas_call(
        _attention_kernel,
        grid=(b, ATT_KV_HEADS, s_len // tq),
        in_specs=[pl.BlockSpec((1, tq, gw), lambda i, g, t: (i, t, OFF_ATT_Q // gw + g)),
                  pl.BlockSpec((1, s_len, ATT_HEAD_DIM), k_col),
                  pl.BlockSpec((1, c_len, ATT_HEAD_DIM), k_col),
                  pl.BlockSpec((1, 1, VT_ROWS, s_len), lambda i, g, t: (i, g, 0, 0)),
                  pl.BlockSpec((1, 1, VT_ROWS, c_len), lambda i, g, t: (i, g, 0, 0)),
                  pl.BlockSpec((1, tq, gw), lambda i, g, t: (i, t, OFF_ATT_G // gw + g))],
        out_specs=pl.BlockSpec((1, tq, gw), lambda i, g, t: (i, t, g)),
        out_shape=jax.ShapeDtypeStruct((b, s_len, ATT_Q_W), BF16),
        compiler_params=pltpu.CompilerParams(
            dimension_semantics=("parallel", "parallel", "arbitrary"),
            vmem_limit_bytes=VMEM_LIMIT),
        name="attention",
    )(px3, px3, pc3, vt_lat, vt_ctx, px3)


def _retention_kernel(d_ref, q_ref, k_ref, v_ref, g_ref, kc_ref, vc_ref, o_ref, sb_ref):
    s_len = q_ref.shape[1]
    c_len = kc_ref.shape[1]
    ch = RET_CHUNK
    n_chunks = s_len // ch

    d = d_ref[0]
    lg = jnp.log1p(-jnp.exp2(d))
    lg_f = lg[0:1, :]
    lg_b = lg[1:2, :]

    idx = lax.broadcasted_iota(jnp.int32, (ch, 1), 0).astype(F32)
    qd_f = (jnp.exp(lg_f * (idx + 1.0)) * (RET_DK ** -0.5)).astype(BF16)
    kd_f = jnp.exp(lg_f * (ch - 1.0 - idx)).astype(BF16)
    qd_b = (jnp.exp(lg_b * (ch - idx)) * (RET_DK ** -0.5)).astype(BF16)
    kd_b = jnp.exp(lg_b * idx).astype(BF16)
    cd_f = jnp.exp(lg_f * ch)
    cd_b = jnp.exp(lg_b * ch)

    row = lax.broadcasted_iota(jnp.int32, (ch, ch), 0)
    col = lax.broadcasted_iota(jnp.int32, (ch, ch), 1)
    rel = (row - col).astype(F32)
    mask = jnp.where(rel > 0, jnp.exp(lg_f * rel),
                     jnp.where(rel < 0, jnp.exp(-lg_b * rel), 2.0)) * (RET_DK ** -0.5)

    def chunk(n):
        return slice(n * ch, (n + 1) * ch)

    def masked_scores(n):
        return (_dot_nt(q_ref[0, chunk(n), :], k_ref[0, chunk(n), :]) * mask).astype(BF16)

    pos = lax.broadcasted_iota(jnp.int32, (c_len, 1), 0).astype(F32)
    kc = kc_ref[0].astype(F32)
    vc = vc_ref[0]
    s_f = _dot_tn((kc * jnp.exp(lg_f * (c_len - 1.0 - pos))).astype(BF16), vc)
    s_b = _dot_tn((kc * jnp.exp(lg_b * pos)).astype(BF16), vc)

    for n in reversed(range(n_chunks)):
        sb_ref[n] = s_b.astype(BF16)
        if n > 0:
            s_b = s_b * cd_b + _dot_tn(k_ref[0, chunk(n), :] * kd_b, v_ref[0, chunk(n), :])

    scores = masked_scores(0)
    for n in range(n_chunks):
        q = q_ref[0, chunk(n), :]
        v = v_ref[0, chunk(n), :]
        last = n + 1 == n_chunks
        next_scores = None if last else masked_scores(n + 1)
        kv_f = None if last else _dot_tn(k_ref[0, chunk(n), :] * kd_f, v)
        o = _dot(q * qd_b, sb_ref[n]) + _dot(q * qd_f, s_f.astype(BF16)) + _dot(scores, v)
        ms = jnp.mean(o * o, axis=-1, keepdims=True)
        if not last:
            probe = pltpu.bitcast(next_scores[0:16, 0:128], jnp.uint32)
            ms = ms + ((probe >> 16) >> 16).astype(F32)[0:1, 0:1]
        o = (o * lax.rsqrt(ms + NORM_EPS)).astype(BF16)
        o_ref[0, chunk(n), :] = o * g_ref[0, chunk(n), :]
        if not last:
            s_f = s_f * cd_f + kv_f
            scores = next_scores


def _retention(px3, pc3, decay3):
    b, s_len, _ = px3.shape
    c_len = pc3.shape[1]
    return pl.pallas_call(
        _retention_kernel,
        grid=(b, RET_HEADS),
        in_specs=[pl.BlockSpec((1, 2, 1), lambda i, h: (h, 0, 0)),
                  pl.BlockSpec((1, s_len, RET_DK), lambda i, h: (i, 0, OFF_RET_Q // RET_DK + h)),
                  pl.BlockSpec((1, s_len, RET_DK), lambda i, h: (i, 0, OFF_RET_K // RET_DK + h)),
                  pl.BlockSpec((1, s_len, RET_DV), lambda i, h: (i, 0, OFF_RET_V // RET_DV + h)),
                  pl.BlockSpec((1, s_len, RET_DV), lambda i, h: (i, 0, OFF_RET_G // RET_DV + h)),
                  pl.BlockSpec((1, c_len, RET_DK), lambda i, h: (i, 0, h)),
                  pl.BlockSpec((1, c_len, RET_DV), lambda i, h: (i, 0, RET_QK_W // RET_DV + h))],
        out_specs=pl.BlockSpec((1, s_len, RET_DV), lambda i, h: (i, 0, h)),
        out_shape=jax.ShapeDtypeStruct((b, s_len, RET_V_W), BF16),
        scratch_shapes=[pltpu.VMEM((s_len // RET_CHUNK, RET_DK, RET_DV), BF16)],
        compiler_params=pltpu.CompilerParams(
            dimension_semantics=("parallel", "arbitrary"), vmem_limit_bytes=VMEM_LIMIT),
        name="retention",
    )(decay3, px3, px3, px3, px3, pc3, pc3)


def _merge_kernel(x_ref, yr_ref, ya_ref, mr0_ref, mr1_ref, ma0_ref, ma1_ref, gate_ref,
                  wr_ref, wa_ref, wo_ref, o_ref):
    a = _dot(yr_ref[...], wr_ref[...])
    b = _dot(ya_ref[...], wa_ref[...])
    mg_ret = jnp.concatenate([mr0_ref[...], mr1_ref[...]], axis=-1).astype(F32)
    mg_att = jnp.concatenate([ma0_ref[...], ma1_ref[...]], axis=-1).astype(F32)
    y = mg_ret * a + mg_att * b
    out = _dot(y.astype(BF16), wo_ref[...])
    o_ref[...] = x_ref[...] + gate_ref[0] * out


def _merge(x2d, y_ret, y_att, px, gate, w_o_ret, w_o_att, w_out, rows_per_mod, tm):
    rows = x2d.shape[0]
    half = D_MODEL // 2
    weight = lambda shape: pl.BlockSpec(shape, lambda i: (0, 0), pipeline_mode=pl.Buffered(1))
    mg = lambda off: pl.BlockSpec((tm, half), lambda i: (i, off // half))
    return pl.pallas_call(
        _merge_kernel,
        grid=(rows // tm,),
        in_specs=[pl.BlockSpec((tm, D_MODEL), lambda i: (i, 0)),
                  pl.BlockSpec((tm, RET_V_W), lambda i: (i, 0)),
                  pl.BlockSpec((tm, ATT_Q_W), lambda i: (i, 0)),
                  mg(OFF_MG_RET), mg(OFF_MG_RET + half), mg(OFF_MG_ATT), mg(OFF_MG_ATT + half),
                  pl.BlockSpec((1, 1, D_MODEL), lambda i: ((i * tm) // rows_per_mod, 0, 0)),
                  weight((RET_V_W, D_MODEL)), weight((ATT_Q_W, D_MODEL)), weight((D_MODEL, D_MODEL))],
        out_specs=pl.BlockSpec((tm, D_MODEL), lambda i: (i, 0)),
        out_shape=jax.ShapeDtypeStruct((rows, D_MODEL), F32),
        compiler_params=pltpu.CompilerParams(
            dimension_semantics=("parallel",), vmem_limit_bytes=VMEM_LIMIT),
        name="merge",
    )(x2d, y_ret, y_att, px, px, px, px, gate, w_o_ret, w_o_att, w_out)


def _rope_tables(s_len):
    rows = s_len // GRID_W
    row = jnp.repeat(jnp.arange(rows, dtype=F32), GRID_W)
    col = jnp.tile(jnp.arange(GRID_W, dtype=F32), rows)
    half = ATT_HEAD_DIM // 2
    freqs = ROPE_THETA ** (-jnp.arange(0, half, 2, dtype=F32) / half)
    ang = jnp.concatenate([row[:, None] * freqs, col[:, None] * freqs], axis=-1)
    cos, sin = jnp.cos(ang), jnp.sin(ang)
    return jnp.concatenate([cos, cos], axis=-1), jnp.concatenate([-sin, sin], axis=-1)


def _split_pairs(t):
    lead = t.shape[:-1]
    t = t.reshape(*lead, -1, ATT_HEAD_DIM // 2, 2)
    return jnp.swapaxes(t, -1, -2).reshape(*lead, -1)


def _layer(x, ctx, c, c_ctx, norm_w, w_ada, b_ada, w_in, ret_log2_decay,
           q_norm_w, k_norm_w, w_o_ret, w_o_att, w_out):
    b, s_len, _ = x.shape
    c_len = ctx.shape[1]

    mod = _adaln(jnp.concatenate([c, c_ctx[None]], axis=0), w_ada, b_ada)
    shift, scale, gate = (mod[:, i * D_MODEL:(i + 1) * D_MODEL] for i in range(3))
    lat = lambda t: t[:b].reshape(b, 1, D_MODEL)
    cx = lambda t: t[b:].reshape(1, 1, D_MODEL)

    w_bf = w_in.astype(BF16)
    for off, width in ((OFF_ATT_K, ATT_KV_W), (OFF_ATT_Q, ATT_Q_W)):
        w_bf = w_bf.at[:, off:off + width].set(_split_pairs(w_bf[:, off:off + width]))
    q_norm_w = _split_pairs(q_norm_w)
    k_norm_w = _split_pairs(k_norm_w)

    cos, sin_signed = _rope_tables(s_len)
    x2d = x.reshape(b * s_len, D_MODEL)
    px, vt_lat = _inproj_rows(x2d, norm_w, lat(scale), lat(shift), w_bf, q_norm_w, k_norm_w,
                              cos, sin_signed, IN_COLS, s_len, s_len, PROJ_TM, PROJ_TN, rope=True)
    pc, vt_ctx = _inproj_rows(ctx.reshape(b * c_len, D_MODEL), norm_w, cx(scale), cx(shift), w_bf,
                              q_norm_w, k_norm_w, cos, sin_signed,
                              KV_COLS, b * c_len, c_len, PROJ_TM, PROJ_TN_CTX, rope=False)
    px3 = px.reshape(b, s_len, IN_COLS)
    pc3 = pc.reshape(b, c_len, KV_COLS)

    y_att = _attention(px3, pc3, vt_lat, vt_ctx)

    decay3 = ret_log2_decay.astype(F32).T.reshape(RET_HEADS, 2, 1)
    y_ret = _retention(px3, pc3, decay3)

    x_new = _merge(x2d, y_ret.reshape(b * s_len, RET_V_W), y_att.reshape(b * s_len, ATT_Q_W), px,
                   lat(gate), w_o_ret.astype(BF16), w_o_att.astype(BF16), w_out.astype(BF16),
                   s_len, MERGE_TM)
    return x_new.reshape(b, s_len, D_MODEL)


def kernel(x, c, ctx, c_ctx, norm_w, w_ada, b_ada, w_in, ret_log2_decay, q_norm_w, k_norm_w,
           w_o_ret, w_o_att, w_out):
    depth = norm_w.shape[0]
    assert depth == 1, "context update between layers is not implemented"
    return _layer(x, ctx, c, c_ctx, norm_w[0], w_ada[0], b_ada[0], w_in[0], ret_log2_decay[0],
                  q_norm_w[0], k_norm_w[0], w_o_ret[0], w_o_att[0], w_out[0])
```

```python
import functools

import jax
import jax.numpy as jnp
from jax import lax
from jax.experimental import pallas as pl
from jax.experimental.pallas import tpu as pltpu

D_MODEL = 1024
GRID_W = 64
RET_HEADS = 4
RET_DK = 256
RET_DV = 512
RET_QK_W = RET_HEADS * RET_DK
RET_V_W = RET_HEADS * RET_DV
ATT_HEADS = 8
ATT_KV_HEADS = 2
ATT_HEAD_DIM = 128
ATT_Q_W = ATT_HEADS * ATT_HEAD_DIM
ATT_KV_W = ATT_KV_HEADS * ATT_HEAD_DIM
ATT_REP = ATT_HEADS // ATT_KV_HEADS
ROPE_THETA = 10000.0
NORM_EPS = 1e-6
KV_COLS = RET_QK_W + RET_V_W + 2 * ATT_KV_W
LOG2E = 1.4426950408889634

OFF_RET_K = 0
OFF_RET_V = OFF_RET_K + RET_QK_W
OFF_ATT_K = OFF_RET_V + RET_V_W
OFF_ATT_V = OFF_ATT_K + ATT_KV_W
OFF_RET_Q = OFF_ATT_V + ATT_KV_W
OFF_RET_G = OFF_RET_Q + RET_QK_W
OFF_ATT_Q = OFF_RET_G + RET_V_W
OFF_ATT_G = OFF_ATT_Q + ATT_Q_W
OFF_MG_RET = OFF_ATT_G + ATT_Q_W
OFF_MG_ATT = OFF_MG_RET + D_MODEL
IN_COLS = OFF_MG_ATT + D_MODEL
assert OFF_RET_Q == KV_COLS

RET_CHUNK = 256
PROJ_TM = 512
PROJ_TN = 1536
PROJ_TN_CTX = 1792
MERGE_TM = 1024
BF16_SUBLANES = 16
VT_ROWS = ATT_HEAD_DIM + BF16_SUBLANES
VMEM_LIMIT = 56 * 1024 * 1024

BF16 = jnp.bfloat16
F32 = jnp.float32


def _dot(a, b):
    return jnp.dot(a, b, preferred_element_type=F32)


def _dot_nt(a, b):
    return lax.dot_general(a, b, (((1,), (1,)), ((), ())), preferred_element_type=F32)


def _dot_tn(a, b):
    return lax.dot_general(a, b, (((0,), (0,)), ((), ())), preferred_element_type=F32)


def _silu(x):
    return x * jax.nn.sigmoid(x)


def _silu_tanh(x):
    h = 0.5 * x
    return h * jnp.tanh(h) + h


def _sigmoid_tanh(x):
    return 0.5 * jnp.tanh(0.5 * x) + 0.5


GATE_COLS = ((OFF_RET_G, OFF_RET_G + RET_V_W, _silu_tanh),
             (OFF_ATT_G, OFF_ATT_G + ATT_Q_W, _silu_tanh),
             (OFF_MG_RET, OFF_MG_ATT + D_MODEL, _sigmoid_tanh))


def _adaln_kernel(c_ref, w_ref, b_ref, o_ref):
    o_ref[...] = _dot(_silu(c_ref[...]), w_ref[...]) + b_ref[...]


def _adaln(cc, w_ada, b_ada):
    rows = cc.shape[0]
    tn = D_MODEL
    return pl.pallas_call(
        _adaln_kernel,
        grid=(3 * D_MODEL // tn,),
        in_specs=[pl.BlockSpec((rows, D_MODEL), lambda j: (0, 0)),
                  pl.BlockSpec((D_MODEL, tn), lambda j: (0, j)),
                  pl.BlockSpec((1, tn), lambda j: (0, j))],
        out_specs=pl.BlockSpec((rows, tn), lambda j: (0, j)),
        out_shape=jax.ShapeDtypeStruct((rows, 3 * D_MODEL), F32),
        compiler_params=pltpu.CompilerParams(vmem_limit_bytes=VMEM_LIMIT),
        name="adaln",
    )(cc, w_ada, b_ada.reshape(1, -1))


def _head_rms(t, w):
    ms = jnp.mean(t * t, axis=-1, keepdims=True)
    return t * lax.rsqrt(ms + NORM_EPS) * w


def _rope(t, cos, sin_signed):
    return t * cos + pltpu.roll(t, ATT_HEAD_DIM // 2, 1) * sin_signed


def _inproj_rows_kernel(x_ref, nw_ref, sc_ref, sh_ref, w_ref, qw_ref, kw_ref, cos_ref, sin_ref,
                        o_ref, vt_ref, *, tn, rope, prep_q):
    x = x_ref[...]
    ms = jnp.mean(x * x, axis=-1, keepdims=True)
    y = x * lax.rsqrt(ms + NORM_EPS) * nw_ref[...]
    h = (y * (1.0 + sc_ref[0]) + sh_ref[0]).astype(BF16)
    n_cols = o_ref.shape[1]
    heads = [(OFF_ATT_K + g * ATT_HEAD_DIM, kw_ref[...]) for g in range(ATT_KV_HEADS)]
    if prep_q:
        qw = qw_ref[...] * (ATT_HEAD_DIM ** -0.5 * LOG2E)
        heads += [(OFF_ATT_Q + r * ATT_HEAD_DIM, qw) for r in range(ATT_HEADS)]
    for c0 in range(0, n_cols, tn):
        res = _dot(h, w_ref[:, c0:c0 + tn])
        edges = sorted({c0, c0 + tn} | {e for lo, hi, _ in GATE_COLS for e in (lo, hi) if c0 < e < c0 + tn})
        for lo, hi in zip(edges[:-1], edges[1:]):
            piece = res[:, lo - c0:hi - c0]
            for g_lo, g_hi, act in GATE_COLS:
                if g_lo <= lo < g_hi:
                    piece = act(piece)
            o_ref[:, lo:hi] = piece.astype(BF16)
        for col, w in heads:
            if c0 <= col < c0 + tn:
                t = _head_rms(res[:, col - c0:col - c0 + ATT_HEAD_DIM], w)
                if rope:
                    t = _rope(t, cos_ref[...], sin_ref[...])
                o_ref[:, col:col + ATT_HEAD_DIM] = t.astype(BF16)
        for g in range(ATT_KV_HEADS):
            col = OFF_ATT_V + g * ATT_HEAD_DIM
            if c0 <= col < c0 + tn:
                n_b, _, _, n_tok = vt_ref.shape
                v = res[:, col - c0:col - c0 + ATT_HEAD_DIM]
                for bb in range(n_b):
                    vt_ref[bb, g, :ATT_HEAD_DIM, :] = v[bb * n_tok:(bb + 1) * n_tok].T.astype(BF16)
                    vt_ref[bb, g, ATT_HEAD_DIM:, :] = jnp.ones((VT_ROWS - ATT_HEAD_DIM, n_tok), BF16)


def _inproj_rows(x2d, norm_w, scale, shift, w_bf, q_norm_w, k_norm_w, cos, sin_signed,
                 n_cols, rows_per_mod, tokens, tm, tn, rope):
    rows = x2d.shape[0]
    if tokens >= tm:
        per_batch = tokens // tm
        vt_block = (1, ATT_KV_HEADS, VT_ROWS, tm)
        vt_map = lambda i: (i // per_batch, 0, 0, i % per_batch)
    else:
        vt_block = (tm // tokens, ATT_KV_HEADS, VT_ROWS, tokens)
        vt_map = lambda i: (i, 0, 0, 0)
    pos_tiles = cos.shape[0] // tm
    mod_map = lambda i: ((i * tm) // rows_per_mod, 0, 0)
    vec = lambda i: (0, 0)
    return pl.pallas_call(
        functools.partial(_inproj_rows_kernel, tn=tn, rope=rope, prep_q=n_cols > OFF_ATT_Q),
        grid=(rows // tm,),
        in_specs=[pl.BlockSpec((tm, D_MODEL), lambda i: (i, 0)),
                  pl.BlockSpec((1, D_MODEL), vec),
                  pl.BlockSpec((1, 1, D_MODEL), mod_map),
                  pl.BlockSpec((1, 1, D_MODEL), mod_map),
                  pl.BlockSpec((D_MODEL, n_cols), vec, pipeline_mode=pl.Buffered(1)),
                  pl.BlockSpec((1, ATT_HEAD_DIM), vec),
                  pl.BlockSpec((1, ATT_HEAD_DIM), vec),
                  pl.BlockSpec((tm, ATT_HEAD_DIM), lambda i: (i % pos_tiles, 0)),
                  pl.BlockSpec((tm, ATT_HEAD_DIM), lambda i: (i % pos_tiles, 0))],
        out_specs=[pl.BlockSpec((tm, n_cols), lambda i: (i, 0)), pl.BlockSpec(vt_block, vt_map)],
        out_shape=[jax.ShapeDtypeStruct((rows, n_cols), BF16),
                   jax.ShapeDtypeStruct((rows // tokens, ATT_KV_HEADS, VT_ROWS, tokens), BF16)],
        compiler_params=pltpu.CompilerParams(
            dimension_semantics=("parallel",), vmem_limit_bytes=VMEM_LIMIT),
        name="inproj_rows",
    )(x2d, norm_w.reshape(1, -1), scale, shift, w_bf, q_norm_w.reshape(1, -1),
      k_norm_w.reshape(1, -1), cos, sin_signed)


KEY_TILE = 128
ATT_UNIT = 512
ATT_TQ = 2048


def _attention_kernel(q_ref, kl_ref, kc_ref, vtl_ref, vtc_ref, g_ref, o_ref):
    k = jnp.concatenate([kl_ref[0], kc_ref[0]], axis=0)
    vt = jnp.concatenate([vtl_ref[0, 0], vtc_ref[0, 0]], axis=1)

    tq = q_ref.shape[1]
    units = [(r, c0) for r in range(ATT_REP) for c0 in range(0, tq, ATT_UNIT)]

    def logits(u):
        r, c0 = units[u]
        q = q_ref[0, c0:c0 + ATT_UNIT, r * ATT_HEAD_DIM:(r + 1) * ATT_HEAD_DIM]
        s_t = _dot_nt(k, q)
        return s_t, jnp.max(s_t, axis=0, keepdims=True)

    def finish(u, s_t, m, s_next):
        r, c0 = units[u]
        rows = slice(c0, c0 + ATT_UNIT)
        cols = slice(r * ATT_HEAD_DIM, (r + 1) * ATT_HEAD_DIM)
        if s_next is None:
            p_t = jnp.exp2(s_t - m).astype(BF16)
        else:
            tiles = []
            for j in range(0, s_t.shape[0], KEY_TILE):
                probe = pltpu.bitcast(s_next[j + KEY_TILE - 8:j + KEY_TILE, :], jnp.uint32)
                zero = ((probe >> 16) >> 16).astype(F32)[0:1, :]
                tiles.append(jnp.exp2(s_t[j:j + KEY_TILE, :] - (m + zero)).astype(BF16))
            p_t = jnp.concatenate(tiles, axis=0)
        o_t = _dot(vt, p_t)
        o = (o_t[:ATT_HEAD_DIM] * (1.0 / o_t[ATT_HEAD_DIM:ATT_HEAD_DIM + 1])).T
        o_ref[0, rows, cols] = (o * g_ref[0, rows, cols].astype(F32)).astype(BF16)

    pending = logits(0)
    for u in range(1, len(units)):
        nxt = logits(u)
        finish(u - 1, *pending, nxt[0])
        pending = nxt
    finish(len(units) - 1, *pending, None)


def _attention(px3, pc3, vt_lat, vt_ctx):
    b, s_len, _ = px3.shape
    c_len = pc3.shape[1]
    gw = ATT_REP * ATT_HEAD_DIM
    tq = ATT_TQ
    k_col = lambda i, g, t: (i, 0, OFF_ATT_K // ATT_HEAD_DIM + g)
    return pl.pallas_call(
        _attention_kernel,
        grid=(b, ATT_KV_HEADS, s_len // tq),
        in_specs=[pl.BlockSpec((1, tq, gw), lambda i, g, t: (i, t, OFF_ATT_Q // gw + g)),
                  pl.BlockSpec((1, s_len, ATT_HEAD_DIM), k_col),
                  pl.BlockSpec((1, c_len, ATT_HEAD_DIM), k_col),
                  pl.BlockSpec((1, 1, VT_ROWS, s_len), lambda i, g, t: (i, g, 0, 0)),
                  pl.BlockSpec((1, 1, VT_ROWS, c_len), lambda i, g, t: (i, g, 0, 0)),
                  pl.BlockSpec((1, tq, gw), lambda i, g, t: (i, t, OFF_ATT_G // gw + g))],
        out_specs=pl.BlockSpec((1, tq, gw), lambda i, g, t: (i, t, g)),
        out_shape=jax.ShapeDtypeStruct((b, s_len, ATT_Q_W), BF16),
        compiler_params=pltpu.CompilerParams(
            dimension_semantics=("parallel", "parallel", "arbitrary"),
            vmem_limit_bytes=VMEM_LIMIT),
        name="attention",
    )(px3, px3, pc3, vt_lat, vt_ctx, px3)


def _retention_kernel(d_ref, q_ref, k_ref, v_ref, g_ref, kc_ref, vc_ref, o_ref, sb_ref):
    s_len = q_ref.shape[1]
    c_len = kc_ref.shape[1]
    ch = RET_CHUNK
    n_chunks = s_len // ch

    d = d_ref[0]
    lg = jnp.log1p(-jnp.exp2(d))
    lg_f = lg[0:1, :]
    lg_b = lg[1:2, :]

    idx = lax.broadcasted_iota(jnp.int32, (ch, 1), 0).astype(F32)
    qd_f = (jnp.exp(lg_f * (idx + 1.0)) * (RET_DK ** -0.5)).astype(BF16)
    kd_f = jnp.exp(lg_f * (ch - 1.0 - idx)).astype(BF16)
    qd_b = (jnp.exp(lg_b * (ch - idx)) * (RET_DK ** -0.5)).astype(BF16)
    kd_b = jnp.exp(lg_b * idx).astype(BF16)
    cd_f = jnp.exp(lg_f * ch)
    cd_b = jnp.exp(lg_b * ch)

    row = lax.broadcasted_iota(jnp.int32, (ch, ch), 0)
    col = lax.broadcasted_iota(jnp.int32, (ch, ch), 1)
    rel = (row - col).astype(F32)
    mask = jnp.where(rel > 0, jnp.exp(lg_f * rel),
                     jnp.where(rel < 0, jnp.exp(-lg_b * rel), 2.0)) * (RET_DK ** -0.5)

    def chunk(n):
        return slice(n * ch, (n + 1) * ch)

    def masked_scores(n):
        return (_dot_nt(q_ref[0, chunk(n), :], k_ref[0, chunk(n), :]) * mask).astype(BF16)

    pos = lax.broadcasted_iota(jnp.int32, (c_len, 1), 0).astype(F32)
    kc = kc_ref[0].astype(F32)
    vc = vc_ref[0]
    s_f = _dot_tn((kc * jnp.exp(lg_f * (c_len - 1.0 - pos))).astype(BF16), vc)
    s_b = _dot_tn((kc * jnp.exp(lg_b * pos)).astype(BF16), vc)

    for n in reversed(range(n_chunks)):
        sb_ref[n] = s_b.astype(BF16)
        if n > 0:
            s_b = s_b * cd_b + _dot_tn(k_ref[0, chunk(n), :] * kd_b, v_ref[0, chunk(n), :])

    scores = masked_scores(0)
    for n in range(n_chunks):
        q = q_ref[0, chunk(n), :]
        v = v_ref[0, chunk(n), :]
        last = n + 1 == n_chunks
        next_scores = None if last else masked_scores(n + 1)
        kv_f = None if last else _dot_tn(k_ref[0, chunk(n), :] * kd_f, v)
        o = _dot(q * qd_b, sb_ref[n]) + _dot(q * qd_f, s_f.astype(BF16)) + _dot(scores, v)
        ms = jnp.mean(o * o, axis=-1, keepdims=True)
        if not last:
            probe = pltpu.bitcast(next_scores[ch - 16:ch, ch - 128:ch], jnp.uint32)
            ms = ms + ((probe >> 16) >> 16).astype(F32)[0:1, 0:1]
        o = (o * lax.rsqrt(ms + NORM_EPS)).astype(BF16)
        o_ref[0, chunk(n), :] = o * g_ref[0, chunk(n), :]
        if not last:
            s_f = s_f * cd_f + kv_f
            scores = next_scores


def _retention(px3, pc3, decay3):
    b, s_len, _ = px3.shape
    c_len = pc3.shape[1]
    return pl.pallas_call(
        _retention_kernel,
        grid=(b, RET_HEADS),
        in_specs=[pl.BlockSpec((1, 2, 1), lambda i, h: (h, 0, 0)),
                  pl.BlockSpec((1, s_len, RET_DK), lambda i, h: (i, 0, OFF_RET_Q // RET_DK + h)),
                  pl.BlockSpec((1, s_len, RET_DK), lambda i, h: (i, 0, OFF_RET_K // RET_DK + h)),
                  pl.BlockSpec((1, s_len, RET_DV), lambda i, h: (i, 0, OFF_RET_V // RET_DV + h)),
                  pl.BlockSpec((1, s_len, RET_DV), lambda i, h: (i, 0, OFF_RET_G // RET_DV + h)),
                  pl.BlockSpec((1, c_len, RET_DK), lambda i, h: (i, 0, h)),
                  pl.BlockSpec((1, c_len, RET_DV), lambda i, h: (i, 0, RET_QK_W // RET_DV + h))],
        out_specs=pl.BlockSpec((1, s_len, RET_DV), lambda i, h: (i, 0, h)),
        out_shape=jax.ShapeDtypeStruct((b, s_len, RET_V_W), BF16),
        scratch_shapes=[pltpu.VMEM((s_len // RET_CHUNK, RET_DK, RET_DV), BF16)],
        compiler_params=pltpu.CompilerParams(
            dimension_semantics=("parallel", "arbitrary"), vmem_limit_bytes=VMEM_LIMIT),
        name="retention",
    )(decay3, px3, px3, px3, px3, pc3, pc3)


def _merge_kernel(x_ref, yr_ref, ya_ref, mr0_ref, mr1_ref, ma0_ref, ma1_ref, gate_ref,
                  wr_ref, wa_ref, wo_ref, o_ref):
    a = _dot(yr_ref[...], wr_ref[...])
    b = _dot(ya_ref[...], wa_ref[...])
    mg_ret = jnp.concatenate([mr0_ref[...], mr1_ref[...]], axis=-1).astype(F32)
    mg_att = jnp.concatenate([ma0_ref[...], ma1_ref[...]], axis=-1).astype(F32)
    y = mg_ret * a + mg_att * b
    out = _dot(y.astype(BF16), wo_ref[...])
    o_ref[...] = x_ref[...] + gate_ref[0] * out


def _merge(x2d, y_ret, y_att, px, gate, w_o_ret, w_o_att, w_out, rows_per_mod, tm):
    rows = x2d.shape[0]
    half = D_MODEL // 2
    weight = lambda shape: pl.BlockSpec(shape, lambda i: (0, 0), pipeline_mode=pl.Buffered(1))
    mg = lambda off: pl.BlockSpec((tm, half), lambda i: (i, off // half))
    return pl.pallas_call(
        _merge_kernel,
        grid=(rows // tm,),
        in_specs=[pl.BlockSpec((tm, D_MODEL), lambda i: (i, 0)),
                  pl.BlockSpec((tm, RET_V_W), lambda i: (i, 0)),
                  pl.BlockSpec((tm, ATT_Q_W), lambda i: (i, 0)),
                  mg(OFF_MG_RET), mg(OFF_MG_RET + half), mg(OFF_MG_ATT), mg(OFF_MG_ATT + half),
                  pl.BlockSpec((1, 1, D_MODEL), lambda i: ((i * tm) // rows_per_mod, 0, 0)),
                  weight((RET_V_W, D_MODEL)), weight((ATT_Q_W, D_MODEL)), weight((D_MODEL, D_MODEL))],
        out_specs=pl.BlockSpec((tm, D_MODEL), lambda i: (i, 0)),
        out_shape=jax.ShapeDtypeStruct((rows, D_MODEL), F32),
        compiler_params=pltpu.CompilerParams(
            dimension_semantics=("parallel",), vmem_limit_bytes=VMEM_LIMIT),
        name="merge",
    )(x2d, y_ret, y_att, px, px, px, px, gate, w_o_ret, w_o_att, w_out)


def _rope_tables(s_len):
    rows = s_len // GRID_W
    row = jnp.repeat(jnp.arange(rows, dtype=F32), GRID_W)
    col = jnp.tile(jnp.arange(GRID_W, dtype=F32), rows)
    half = ATT_HEAD_DIM // 2
    freqs = ROPE_THETA ** (-jnp.arange(0, half, 2, dtype=F32) / half)
    ang = jnp.concatenate([row[:, None] * freqs, col[:, None] * freqs], axis=-1)
    cos, sin = jnp.cos(ang), jnp.sin(ang)
    return jnp.concatenate([cos, cos], axis=-1), jnp.concatenate([-sin, sin], axis=-1)


def _split_pairs(t):
    lead = t.shape[:-1]
    t = t.reshape(*lead, -1, ATT_HEAD_DIM // 2, 2)
    return jnp.swapaxes(t, -1, -2).reshape(*lead, -1)


def _layer(x, ctx, c, c_ctx, norm_w, w_ada, b_ada, w_in, ret_log2_decay,
           q_norm_w, k_norm_w, w_o_ret, w_o_att, w_out):
    b, s_len, _ = x.shape
    c_len = ctx.shape[1]

    mod = _adaln(jnp.concatenate([c, c_ctx[None]], axis=0), w_ada, b_ada)
    shift, scale, gate = (mod[:, i * D_MODEL:(i + 1) * D_MODEL] for i in range(3))
    lat = lambda t: t[:b].reshape(b, 1, D_MODEL)
    cx = lambda t: t[b:].reshape(1, 1, D_MODEL)

    w_bf = w_in.astype(BF16)
    for off, width in ((OFF_ATT_K, ATT_KV_W), (OFF_ATT_Q, ATT_Q_W)):
        w_bf = w_bf.at[:, off:off + width].set(_split_pairs(w_bf[:, off:off + width]))
    q_norm_w = _split_pairs(q_norm_w)
    k_norm_w = _split_pairs(k_norm_w)

    cos, sin_signed = _rope_tables(s_len)
    x2d = x.reshape(b * s_len, D_MODEL)
    px, vt_lat = _inproj_rows(x2d, norm_w, lat(scale), lat(shift), w_bf, q_norm_w, k_norm_w,
                              cos, sin_signed, IN_COLS, s_len, s_len, PROJ_TM, PROJ_TN, rope=True)
    pc, vt_ctx = _inproj_rows(ctx.reshape(b * c_len, D_MODEL), norm_w, cx(scale), cx(shift), w_bf,
                              q_norm_w, k_norm_w, cos, sin_signed,
                              KV_COLS, b * c_len, c_len, PROJ_TM, PROJ_TN_CTX, rope=False)
    px3 = px.reshape(b, s_len, IN_COLS)
    pc3 = pc.reshape(b, c_len, KV_COLS)

    y_att = _attention(px3, pc3, vt_lat, vt_ctx)

    decay3 = ret_log2_decay.astype(F32).T.reshape(RET_HEADS, 2, 1)
    y_ret = _retention(px3, pc3, decay3)

    x_new = _merge(x2d, y_ret.reshape(b * s_len, RET_V_W), y_att.reshape(b * s_len, ATT_Q_W), px,
                   lat(gate), w_o_ret.astype(BF16), w_o_att.astype(BF16), w_out.astype(BF16),
                   s_len, MERGE_TM)
    return x_new.reshape(b, s_len, D_MODEL)


def kernel(x, c, ctx, c_ctx, norm_w, w_ada, b_ada, w_in, ret_log2_decay, q_norm_w, k_norm_w,
           w_o_ret, w_o_att, w_out):
    depth = norm_w.shape[0]
    assert depth == 1, "context update between layers is not implemented"
    return _layer(x, ctx, c, c_ctx, norm_w[0], w_ada[0], b_ada[0], w_in[0], ret_log2_decay[0],
                  q_norm_w[0], k_norm_w[0], w_o_ret[0], w_o_att[0], w_out[0])
```

```python
import functools

import jax
import jax.numpy as jnp
from jax import lax
from jax.experimental import pallas as pl
from jax.experimental.pallas import tpu as pltpu

D_MODEL = 1024
GRID_W = 64
RET_HEADS = 4
RET_DK = 256
RET_DV = 512
RET_QK_W = RET_HEADS * RET_DK
RET_V_W = RET_HEADS * RET_DV
ATT_HEADS = 8
ATT_KV_HEADS = 2
ATT_HEAD_DIM = 128
ATT_Q_W = ATT_HEADS * ATT_HEAD_DIM
ATT_KV_W = ATT_KV_HEADS * ATT_HEAD_DIM
ATT_REP = ATT_HEADS // ATT_KV_HEADS
ROPE_THETA = 10000.0
NORM_EPS = 1e-6
KV_COLS = RET_QK_W + RET_V_W + 2 * ATT_KV_W
LOG2E = 1.4426950408889634

OFF_RET_K = 0
OFF_RET_V = OFF_RET_K + RET_QK_W
OFF_ATT_K = OFF_RET_V + RET_V_W
OFF_ATT_V = OFF_ATT_K + ATT_KV_W
OFF_RET_Q = OFF_ATT_V + ATT_KV_W
OFF_RET_G = OFF_RET_Q + RET_QK_W
OFF_ATT_Q = OFF_RET_G + RET_V_W
OFF_ATT_G = OFF_ATT_Q + ATT_Q_W
OFF_MG_RET = OFF_ATT_G + ATT_Q_W
OFF_MG_ATT = OFF_MG_RET + D_MODEL
IN_COLS = OFF_MG_ATT + D_MODEL
assert OFF_RET_Q == KV_COLS

RET_CHUNK = 256
PROJ_TM = 512
PROJ_TN = 1536
PROJ_TN_CTX = 1792
MERGE_TM = 1024
BF16_SUBLANES = 16
VT_ROWS = ATT_HEAD_DIM + BF16_SUBLANES
VMEM_LIMIT = 56 * 1024 * 1024

BF16 = jnp.bfloat16
F32 = jnp.float32


def _dot(a, b):
    return jnp.dot(a, b, preferred_element_type=F32)


def _dot_nt(a, b):
    return lax.dot_general(a, b, (((1,), (1,)), ((), ())), preferred_element_type=F32)


def _dot_tn(a, b):
    return lax.dot_general(a, b, (((0,), (0,)), ((), ())), preferred_element_type=F32)


def _silu(x):
    return x * jax.nn.sigmoid(x)


def _silu_tanh(x, half=0.5):
    h = half * x
    return h * jnp.tanh(h) + h


def _sigmoid_tanh(x, half=0.5):
    return half * jnp.tanh(half * x) + half


GATE_COLS = ((OFF_RET_G, OFF_RET_G + RET_V_W, _silu_tanh),
             (OFF_ATT_G, OFF_ATT_G + ATT_Q_W, _silu_tanh),
             (OFF_MG_RET, OFF_MG_ATT + D_MODEL, _sigmoid_tanh))


def _adaln_kernel(c_ref, w_ref, b_ref, o_ref):
    o_ref[...] = _dot(_silu(c_ref[...]), w_ref[...]) + b_ref[...]


def _adaln(cc, w_ada, b_ada):
    rows = cc.shape[0]
    tn = D_MODEL
    return pl.pallas_call(
        _adaln_kernel,
        grid=(3 * D_MODEL // tn,),
        in_specs=[pl.BlockSpec((rows, D_MODEL), lambda j: (0, 0)),
                  pl.BlockSpec((D_MODEL, tn), lambda j: (0, j)),
                  pl.BlockSpec((1, tn), lambda j: (0, j))],
        out_specs=pl.BlockSpec((rows, tn), lambda j: (0, j)),
        out_shape=jax.ShapeDtypeStruct((rows, 3 * D_MODEL), F32),
        compiler_params=pltpu.CompilerParams(vmem_limit_bytes=VMEM_LIMIT),
        name="adaln",
    )(cc, w_ada, b_ada.reshape(1, -1))


def _head_rms(t, w):
    ms = jnp.mean(t * t, axis=-1, keepdims=True)
    return t * lax.rsqrt(ms + NORM_EPS) * w


def _rope(t, cos, sin_signed):
    return t * cos + pltpu.roll(t, ATT_HEAD_DIM // 2, 1) * sin_signed


def _inproj_rows_kernel(x_ref, nw_ref, sc_ref, sh_ref, w_ref, qw_ref, kw_ref, cos_ref, sin_ref,
                        o_ref, vt_ref, *, tn, rope, prep_q):
    x = x_ref[...]
    ms = jnp.mean(x * x, axis=-1, keepdims=True)
    y = x * lax.rsqrt(ms + NORM_EPS) * nw_ref[...]
    h = (y * (1.0 + sc_ref[0]) + sh_ref[0]).astype(BF16)
    n_cols = o_ref.shape[1]
    heads = [(OFF_ATT_K + g * ATT_HEAD_DIM, kw_ref[...]) for g in range(ATT_KV_HEADS)]
    if prep_q:
        qw = qw_ref[...] * (ATT_HEAD_DIM ** -0.5 * LOG2E)
        heads += [(OFF_ATT_Q + r * ATT_HEAD_DIM, qw) for r in range(ATT_HEADS)]
    chunks = list(range(0, n_cols, tn))
    res_next = _dot(h, w_ref[:, 0:tn])
    for ci, c0 in enumerate(chunks):
        res = res_next
        half = 0.5
        if ci + 1 < len(chunks):
            res_next = _dot(h, w_ref[:, chunks[ci + 1]:chunks[ci + 1] + tn])
            probe = pltpu.bitcast(res_next[0:8, 0:128], jnp.uint32)
            half = 0.5 + ((probe >> 16) >> 16).astype(F32)[0:1, 0:1]
        edges = sorted({c0, c0 + tn} | {e for lo, hi, _ in GATE_COLS for e in (lo, hi) if c0 < e < c0 + tn})
        for lo, hi in zip(edges[:-1], edges[1:]):
            piece = res[:, lo - c0:hi - c0]
            for g_lo, g_hi, act in GATE_COLS:
                if g_lo <= lo < g_hi:
                    piece = act(piece, half)
            o_ref[:, lo:hi] = piece.astype(BF16)
        for col, w in heads:
            if c0 <= col < c0 + tn:
                t = _head_rms(res[:, col - c0:col - c0 + ATT_HEAD_DIM], w)
                if rope:
                    t = _rope(t, cos_ref[...], sin_ref[...])
                o_ref[:, col:col + ATT_HEAD_DIM] = t.astype(BF16)
        for g in range(ATT_KV_HEADS):
            col = OFF_ATT_V + g * ATT_HEAD_DIM
            if c0 <= col < c0 + tn:
                n_b, _, _, n_tok = vt_ref.shape
                v = res[:, col - c0:col - c0 + ATT_HEAD_DIM]
                for bb in range(n_b):
                    vt_ref[bb, g, :ATT_HEAD_DIM, :] = v[bb * n_tok:(bb + 1) * n_tok].T.astype(BF16)
                    vt_ref[bb, g, ATT_HEAD_DIM:, :] = jnp.ones((VT_ROWS - ATT_HEAD_DIM, n_tok), BF16)


def _inproj_rows(x2d, norm_w, scale, shift, w_bf, q_norm_w, k_norm_w, cos, sin_signed,
                 n_cols, rows_per_mod, tokens, tm, tn, rope):
    rows = x2d.shape[0]
    if tokens >= tm:
        per_batch = tokens // tm
        vt_block = (1, ATT_KV_HEADS, VT_ROWS, tm)
        vt_map = lambda i: (i // per_batch, 0, 0, i % per_batch)
    else:
        vt_block = (tm // tokens, ATT_KV_HEADS, VT_ROWS, tokens)
        vt_map = lambda i: (i, 0, 0, 0)
    pos_tiles = cos.shape[0] // tm
    mod_map = lambda i: ((i * tm) // rows_per_mod, 0, 0)
    vec = lambda i: (0, 0)
    return pl.pallas_call(
        functools.partial(_inproj_rows_kernel, tn=tn, rope=rope, prep_q=n_cols > OFF_ATT_Q),
        grid=(rows // tm,),
        in_specs=[pl.BlockSpec((tm, D_MODEL), lambda i: (i, 0)),
                  pl.BlockSpec((1, D_MODEL), vec),
                  pl.BlockSpec((1, 1, D_MODEL), mod_map),
                  pl.BlockSpec((1, 1, D_MODEL), mod_map),
                  pl.BlockSpec((D_MODEL, n_cols), vec, pipeline_mode=pl.Buffered(1)),
                  pl.BlockSpec((1, ATT_HEAD_DIM), vec),
                  pl.BlockSpec((1, ATT_HEAD_DIM), vec),
                  pl.BlockSpec((tm, ATT_HEAD_DIM), lambda i: (i % pos_tiles, 0)),
                  pl.BlockSpec((tm, ATT_HEAD_DIM), lambda i: (i % pos_tiles, 0))],
        out_specs=[pl.BlockSpec((tm, n_cols), lambda i: (i, 0)), pl.BlockSpec(vt_block, vt_map)],
        out_shape=[jax.ShapeDtypeStruct((rows, n_cols), BF16),
                   jax.ShapeDtypeStruct((rows // tokens, ATT_KV_HEADS, VT_ROWS, tokens), BF16)],
        compiler_params=pltpu.CompilerParams(
            dimension_semantics=("parallel",), vmem_limit_bytes=VMEM_LIMIT),
        name="inproj_rows",
    )(x2d, norm_w.reshape(1, -1), scale, shift, w_bf, q_norm_w.reshape(1, -1),
      k_norm_w.reshape(1, -1), cos, sin_signed)


KEY_TILE = 128
ATT_UNIT = 512
ATT_TQ = 2048


def _attention_kernel(q_ref, kl_ref, kc_ref, vtl_ref, vtc_ref, g_ref, o_ref):
    k = jnp.concatenate([kl_ref[0], kc_ref[0]], axis=0)
    vt = jnp.concatenate([vtl_ref[0, 0], vtc_ref[0, 0]], axis=1)

    tq = q_ref.shape[1]
    units = [(r, c0) for r in range(ATT_REP) for c0 in range(0, tq, ATT_UNIT)]

    def logits(u):
        r, c0 = units[u]
        q = q_ref[0, c0:c0 + ATT_UNIT, r * ATT_HEAD_DIM:(r + 1) * ATT_HEAD_DIM]
        s_t = _dot_nt(k, q)
        return s_t, jnp.max(s_t, axis=0, keepdims=True)

    def finish(u, s_t, m, s_next):
        r, c0 = units[u]
        rows = slice(c0, c0 + ATT_UNIT)
        cols = slice(r * ATT_HEAD_DIM, (r + 1) * ATT_HEAD_DIM)
        if s_next is None:
            p_t = jnp.exp2(s_t - m).astype(BF16)
        else:
            tiles = []
            for j in range(0, s_t.shape[0], KEY_TILE):
                probe = pltpu.bitcast(s_next[j + KEY_TILE - 8:j + KEY_TILE, :], jnp.uint32)
                zero = ((probe >> 16) >> 16).astype(F32)[0:1, :]
                tiles.append(jnp.exp2(s_t[j:j + KEY_TILE, :] - (m + zero)).astype(BF16))
            p_t = jnp.concatenate(tiles, axis=0)
        o_t = _dot(vt, p_t)
        o = (o_t[:ATT_HEAD_DIM] * (1.0 / o_t[ATT_HEAD_DIM:ATT_HEAD_DIM + 1])).T
        o_ref[0, rows, cols] = (o * g_ref[0, rows, cols].astype(F32)).astype(BF16)

    pending = logits(0)
    for u in range(1, len(units)):
        nxt = logits(u)
        finish(u - 1, *pending, nxt[0])
        pending = nxt
    finish(len(units) - 1, *pending, None)


def _attention(px3, pc3, vt_lat, vt_ctx):
    b, s_len, _ = px3.shape
    c_len = pc3.shape[1]
    gw = ATT_REP * ATT_HEAD_DIM
    tq = ATT_TQ
    k_col = lambda i, g, t: (i, 0, OFF_ATT_K // ATT_HEAD_DIM + g)
    return pl.pallas_call(
        _attention_kernel,
        grid=(b, ATT_KV_HEADS, s_len // tq),
        in_specs=[pl.BlockSpec((1, tq, gw), lambda i, g, t: (i, t, OFF_ATT_Q // gw + g)),
                  pl.BlockSpec((1, s_len, ATT_HEAD_DIM), k_col),
                  pl.BlockSpec((1, c_len, ATT_HEAD_DIM), k_col),
                  pl.BlockSpec((1, 1, VT_ROWS, s_len), lambda i, g, t: (i, g, 0, 0)),
                  pl.BlockSpec((1, 1, VT_ROWS, c_len), lambda i, g, t: (i, g, 0, 0)),
                  pl.BlockSpec((1, tq, gw), lambda i, g, t: (i, t, OFF_ATT_G // gw + g))],
        out_specs=pl.BlockSpec((1, tq, gw), lambda i, g, t: (i, t, g)),
        out_shape=jax.ShapeDtypeStruct((b, s_len, ATT_Q_W), BF16),
        compiler_params=pltpu.CompilerParams(
            dimension_semantics=("parallel", "parallel", "arbitrary"),
            vmem_limit_bytes=VMEM_LIMIT),
        name="attention",
    )(px3, px3, pc3, vt_lat, vt_ctx, px3)


def _retention_kernel(d_ref, q_ref, k_ref, v_ref, g_ref, kc_ref, vc_ref, o_ref, sb_ref):
    s_len = q_ref.shape[1]
    c_len = kc_ref.shape[1]
    ch = RET_CHUNK
    n_chunks = s_len // ch

    d = d_ref[0]
    lg = jnp.log1p(-jnp.exp2(d))
    lg_f = lg[0:1, :]
    lg_b = lg[1:2, :]

    idx = lax.broadcasted_iota(jnp.int32, (ch, 1), 0).astype(F32)
    qd_f = (jnp.exp(lg_f * (idx + 1.0)) * (RET_DK ** -0.5)).astype(BF16)
    kd_f = jnp.exp(lg_f * (ch - 1.0 - idx)).astype(BF16)
    qd_b = (jnp.exp(lg_b * (ch - idx)) * (RET_DK ** -0.5)).astype(BF16)
    kd_b = jnp.exp(lg_b * idx).astype(BF16)
    cd_f = jnp.exp(lg_f * ch)
    cd_b = jnp.exp(lg_b * ch)

    row = lax.broadcasted_iota(jnp.int32, (ch, ch), 0)
    col = lax.broadcasted_iota(jnp.int32, (ch, ch), 1)
    rel = (row - col).astype(F32)
    mask = jnp.where(rel > 0, jnp.exp(lg_f * rel),
                     jnp.where(rel < 0, jnp.exp(-lg_b * rel), 2.0)) * (RET_DK ** -0.5)

    def chunk(n):
        return slice(n * ch, (n + 1) * ch)

    def masked_scores(n):
        return (_dot_nt(q_ref[0, chunk(n), :], k_ref[0, chunk(n), :]) * mask).astype(BF16)

    pos = lax.broadcasted_iota(jnp.int32, (c_len, 1), 0).astype(F32)
    kc = kc_ref[0].astype(F32)
    vc = vc_ref[0]
    s_f = _dot_tn((kc * jnp.exp(lg_f * (c_len - 1.0 - pos))).astype(BF16), vc)
    s_b = _dot_tn((kc * jnp.exp(lg_b * pos)).astype(BF16), vc)

    for n in reversed(range(n_chunks)):
        sb_ref[n] = s_b.astype(BF16)
        if n > 0:
            s_b = s_b * cd_b + _dot_tn(k_ref[0, chunk(n), :] * kd_b, v_ref[0, chunk(n), :])

    scores = masked_scores(0)
    for n in range(n_chunks):
        q = q_ref[0, chunk(n), :]
        v = v_ref[0, chunk(n), :]
        last = n + 1 == n_chunks
        next_scores = None if last else masked_scores(n + 1)
        kv_f = None if last else _dot_tn(k_ref[0, chunk(n), :] * kd_f, v)
        o = _dot(q * qd_b, sb_ref[n]) + _dot(q * qd_f, s_f.astype(BF16)) + _dot(scores, v)
        ms = jnp.mean(o * o, axis=-1, keepdims=True)
        if not last:
            probe = pltpu.bitcast(next_scores[0:16, 0:128], jnp.uint32)
            ms = ms + ((probe >> 16) >> 16).astype(F32)[0:1, 0:1]
        o = (o * lax.rsqrt(ms + NORM_EPS)).astype(BF16)
        o_ref[0, chunk(n), :] = o * g_ref[0, chunk(n), :]
        if not last:
            s_f = s_f * cd_f + kv_f
            scores = next_scores


def _retention(px3, pc3, decay3):
    b, s_len, _ = px3.shape
    c_len = pc3.shape[1]
    return pl.pallas_call(
        _retention_kernel,
        grid=(b, RET_HEADS),
        in_specs=[pl.BlockSpec((1, 2, 1), lambda i, h: (h, 0, 0)),
                  pl.BlockSpec((1, s_len, RET_DK), lambda i, h: (i, 0, OFF_RET_Q // RET_DK + h)),
                  pl.BlockSpec((1, s_len, RET_DK), lambda i, h: (i, 0, OFF_RET_K // RET_DK + h)),
                  pl.BlockSpec((1, s_len, RET_DV), lambda i, h: (i, 0, OFF_RET_V // RET_DV + h)),
                  pl.BlockSpec((1, s_len, RET_DV), lambda i, h: (i, 0, OFF_RET_G // RET_DV + h)),
                  pl.BlockSpec((1, c_len, RET_DK), lambda i, h: (i, 0, h)),
                  pl.BlockSpec((1, c_len, RET_DV), lambda i, h: (i, 0, RET_QK_W // RET_DV + h))],
        out_specs=pl.BlockSpec((1, s_len, RET_DV), lambda i, h: (i, 0, h)),
        out_shape=jax.ShapeDtypeStruct((b, s_len, RET_V_W), BF16),
        scratch_shapes=[pltpu.VMEM((s_len // RET_CHUNK, RET_DK, RET_DV), BF16)],
        compiler_params=pltpu.CompilerParams(
            dimension_semantics=("parallel", "arbitrary"), vmem_limit_bytes=VMEM_LIMIT),
        name="retention",
    )(decay3, px3, px3, px3, px3, pc3, pc3)


def _merge_kernel(x_ref, yr_ref, ya_ref, mr0_ref, mr1_ref, ma0_ref, ma1_ref, gate_ref,
                  wr_ref, wa_ref, wo_ref, o_ref):
    a = _dot(yr_ref[...], wr_ref[...])
    b = _dot(ya_ref[...], wa_ref[...])
    mg_ret = jnp.concatenate([mr0_ref[...], mr1_ref[...]], axis=-1).astype(F32)
    mg_att = jnp.concatenate([ma0_ref[...], ma1_ref[...]], axis=-1).astype(F32)
    y = mg_ret * a + mg_att * b
    out = _dot(y.astype(BF16), wo_ref[...])
    o_ref[...] = x_ref[...] + gate_ref[0] * out


def _merge(x2d, y_ret, y_att, px, gate, w_o_ret, w_o_att, w_out, rows_per_mod, tm):
    rows = x2d.shape[0]
    half = D_MODEL // 2
    weight = lambda shape: pl.BlockSpec(shape, lambda i: (0, 0), pipeline_mode=pl.Buffered(1))
    mg = lambda off: pl.BlockSpec((tm, half), lambda i: (i, off // half))
    return pl.pallas_call(
        _merge_kernel,
        grid=(rows // tm,),
        in_specs=[pl.BlockSpec((tm, D_MODEL), lambda i: (i, 0)),
                  pl.BlockSpec((tm, RET_V_W), lambda i: (i, 0)),
                  pl.BlockSpec((tm, ATT_Q_W), lambda i: (i, 0)),
                  mg(OFF_MG_RET), mg(OFF_MG_RET + half), mg(OFF_MG_ATT), mg(OFF_MG_ATT + half),
                  pl.BlockSpec((1, 1, D_MODEL), lambda i: ((i * tm) // rows_per_mod, 0, 0)),
                  weight((RET_V_W, D_MODEL)), weight((ATT_Q_W, D_MODEL)), weight((D_MODEL, D_MODEL))],
        out_specs=pl.BlockSpec((tm, D_MODEL), lambda i: (i, 0)),
        out_shape=jax.ShapeDtypeStruct((rows, D_MODEL), F32),
        compiler_params=pltpu.CompilerParams(
            dimension_semantics=("parallel",), vmem_limit_bytes=VMEM_LIMIT),
        name="merge",
    )(x2d, y_ret, y_att, px, px, px, px, gate, w_o_ret, w_o_att, w_out)


def _rope_tables(s_len):
    rows = s_len // GRID_W
    row = jnp.repeat(jnp.arange(rows, dtype=F32), GRID_W)
    col = jnp.tile(jnp.arange(GRID_W, dtype=F32), rows)
    half = ATT_HEAD_DIM // 2
    freqs = ROPE_THETA ** (-jnp.arange(0, half, 2, dtype=F32) / half)
    ang = jnp.concatenate([row[:, None] * freqs, col[:, None] * freqs], axis=-1)
    cos, sin = jnp.cos(ang), jnp.sin(ang)
    return jnp.concatenate([cos, cos], axis=-1), jnp.concatenate([-sin, sin], axis=-1)


def _split_pairs(t):
    lead = t.shape[:-1]
    t = t.reshape(*lead, -1, ATT_HEAD_DIM // 2, 2)
    return jnp.swapaxes(t, -1, -2).reshape(*lead, -1)


def _layer(x, ctx, c, c_ctx, norm_w, w_ada, b_ada, w_in, ret_log2_decay,
           q_norm_w, k_norm_w, w_o_ret, w_o_att, w_out):
    b, s_len, _ = x.shape
    c_len = ctx.shape[1]

    mod = _adaln(jnp.concatenate([c, c_ctx[None]], axis=0), w_ada, b_ada)
    shift, scale, gate = (mod[:, i * D_MODEL:(i + 1) * D_MODEL] for i in range(3))
    lat = lambda t: t[:b].reshape(b, 1, D_MODEL)
    cx = lambda t: t[b:].reshape(1, 1, D_MODEL)

    w_bf = w_in.astype(BF16)
    for off, width in ((OFF_ATT_K, ATT_KV_W), (OFF_ATT_Q, ATT_Q_W)):
        w_bf = w_bf.at[:, off:off + width].set(_split_pairs(w_bf[:, off:off + width]))
    q_norm_w = _split_pairs(q_norm_w)
    k_norm_w = _split_pairs(k_norm_w)

    cos, sin_signed = _rope_tables(s_len)
    x2d = x.reshape(b * s_len, D_MODEL)
    px, vt_lat = _inproj_rows(x2d, norm_w, lat(scale), lat(shift), w_bf, q_norm_w, k_norm_w,
                              cos, sin_signed, IN_COLS, s_len, s_len, PROJ_TM, PROJ_TN, rope=True)
    pc, vt_ctx = _inproj_rows(ctx.reshape(b * c_len, D_MODEL), norm_w, cx(scale), cx(shift), w_bf,
                              q_norm_w, k_norm_w, cos, sin_signed,
                              KV_COLS, b * c_len, c_len, PROJ_TM, PROJ_TN_CTX, rope=False)
    px3 = px.reshape(b, s_len, IN_COLS)
    pc3 = pc.reshape(b, c_len, KV_COLS)

    y_att = _attention(px3, pc3, vt_lat, vt_ctx)

    decay3 = ret_log2_decay.astype(F32).T.reshape(RET_HEADS, 2, 1)
    y_ret = _retention(px3, pc3, decay3)

    x_new = _merge(x2d, y_ret.reshape(b * s_len, RET_V_W), y_att.reshape(b * s_len, ATT_Q_W), px,
                   lat(gate), w_o_ret.astype(BF16), w_o_att.astype(BF16), w_out.astype(BF16),
                   s_len, MERGE_TM)
    return x_new.reshape(b, s_len, D_MODEL)


def kernel(x, c, ctx, c_ctx, norm_w, w_ada, b_ada, w_in, ret_log2_decay, q_norm_w, k_norm_w,
           w_o_ret, w_o_att, w_out):
    depth = norm_w.shape[0]
    assert depth == 1, "context update between layers is not implemented"
    return _layer(x, ctx, c, c_ctx, norm_w[0], w_ada[0], b_ada[0], w_in[0], ret_log2_decay[0],
                  q_norm_w[0], k_norm_w[0], w_o_ret[0], w_o_att[0], w_out[0])
```

```python
import functools

import jax
import jax.numpy as jnp
from jax import lax
from jax.experimental import pallas as pl
from jax.experimental.pallas import tpu as pltpu

D_MODEL = 1024
GRID_W = 64
RET_HEADS = 4
RET_DK = 256
RET_DV = 512
RET_QK_W = RET_HEADS * RET_DK
RET_V_W = RET_HEADS * RET_DV
ATT_HEADS = 8
ATT_KV_HEADS = 2
ATT_HEAD_DIM = 128
ATT_Q_W = ATT_HEADS * ATT_HEAD_DIM
ATT_KV_W = ATT_KV_HEADS * ATT_HEAD_DIM
ATT_REP = ATT_HEADS // ATT_KV_HEADS
ROPE_THETA = 10000.0
NORM_EPS = 1e-6
KV_COLS = RET_QK_W + RET_V_W + 2 * ATT_KV_W
LOG2E = 1.4426950408889634

OFF_RET_K = 0
OFF_RET_V = OFF_RET_K + RET_QK_W
OFF_ATT_K = OFF_RET_V + RET_V_W
OFF_ATT_V = OFF_ATT_K + ATT_KV_W
OFF_RET_Q = OFF_ATT_V + ATT_KV_W
OFF_RET_G = OFF_RET_Q + RET_QK_W
OFF_ATT_Q = OFF_RET_G + RET_V_W
OFF_ATT_G = OFF_ATT_Q + ATT_Q_W
OFF_MG_RET = OFF_ATT_G + ATT_Q_W
OFF_MG_ATT = OFF_MG_RET + D_MODEL
IN_COLS = OFF_MG_ATT + D_MODEL
assert OFF_RET_Q == KV_COLS

RET_CHUNK = 256
PROJ_TM = 512
PROJ_TN = 1536
PROJ_TN_CTX = 1792
MERGE_TM = 1024
BF16_SUBLANES = 16
VT_ROWS = ATT_HEAD_DIM + BF16_SUBLANES
VMEM_LIMIT = 56 * 1024 * 1024

BF16 = jnp.bfloat16
F32 = jnp.float32


def _dot(a, b):
    return jnp.dot(a, b, preferred_element_type=F32)


def _dot_nt(a, b):
    return lax.dot_general(a, b, (((1,), (1,)), ((), ())), preferred_element_type=F32)


def _dot_tn(a, b):
    return lax.dot_general(a, b, (((0,), (0,)), ((), ())), preferred_element_type=F32)


def _silu(x):
    return x * jax.nn.sigmoid(x)


def _silu_tanh(x):
    h = 0.5 * x
    return h * jnp.tanh(h) + h


def _sigmoid_tanh(x):
    return 0.5 * jnp.tanh(0.5 * x) + 0.5


GATE_COLS = ((OFF_RET_G, OFF_RET_G + RET_V_W, _silu_tanh),
             (OFF_ATT_G, OFF_ATT_G + ATT_Q_W, _silu_tanh),
             (OFF_MG_RET, OFF_MG_ATT + D_MODEL, _sigmoid_tanh))


def _adaln_kernel(c_ref, w_ref, b_ref, o_ref):
    o_ref[...] = _dot(_silu(c_ref[...]), w_ref[...]) + b_ref[...]


def _adaln(cc, w_ada, b_ada):
    rows = cc.shape[0]
    tn = D_MODEL
    return pl.pallas_call(
        _adaln_kernel,
        grid=(3 * D_MODEL // tn,),
        in_specs=[pl.BlockSpec((rows, D_MODEL), lambda j: (0, 0)),
                  pl.BlockSpec((D_MODEL, tn), lambda j: (0, j)),
                  pl.BlockSpec((1, tn), lambda j: (0, j))],
        out_specs=pl.BlockSpec((rows, tn), lambda j: (0, j)),
        out_shape=jax.ShapeDtypeStruct((rows, 3 * D_MODEL), F32),
        compiler_params=pltpu.CompilerParams(vmem_limit_bytes=VMEM_LIMIT),
        name="adaln",
    )(cc, w_ada, b_ada.reshape(1, -1))


def _head_rms(t, w):
    ms = jnp.mean(t * t, axis=-1, keepdims=True)
    return t * lax.rsqrt(ms + NORM_EPS) * w


def _rope(t, cos, sin_signed):
    return t * cos + pltpu.roll(t, ATT_HEAD_DIM // 2, 1) * sin_signed


def _inproj_rows_kernel(x_ref, nw_ref, sc_ref, sh_ref, w_ref, qw_ref, kw_ref, cos_ref, sin_ref,
                        o_ref, vt_ref, *, tn, rope, prep_q):
    x = x_ref[...]
    ms = jnp.mean(x * x, axis=-1, keepdims=True)
    y = x * lax.rsqrt(ms + NORM_EPS) * nw_ref[...]
    h = (y * (1.0 + sc_ref[0]) + sh_ref[0]).astype(BF16)
    n_cols = o_ref.shape[1]
    heads = [(OFF_ATT_K + g * ATT_HEAD_DIM, kw_ref[...]) for g in range(ATT_KV_HEADS)]
    if prep_q:
        qw = qw_ref[...] * (ATT_HEAD_DIM ** -0.5 * LOG2E)
        heads += [(OFF_ATT_Q + r * ATT_HEAD_DIM, qw) for r in range(ATT_HEADS)]
    for c0 in range(0, n_cols, tn):
        res = _dot(h, w_ref[:, c0:c0 + tn])
        edges = sorted({c0, c0 + tn} | {e for lo, hi, _ in GATE_COLS for e in (lo, hi) if c0 < e < c0 + tn})
        for lo, hi in zip(edges[:-1], edges[1:]):
            piece = res[:, lo - c0:hi - c0]
            for g_lo, g_hi, act in GATE_COLS:
                if g_lo <= lo < g_hi:
                    piece = act(piece)
            o_ref[:, lo:hi] = piece.astype(BF16)
        for col, w in heads:
            if c0 <= col < c0 + tn:
                t = _head_rms(res[:, col - c0:col - c0 + ATT_HEAD_DIM], w)
                if rope:
                    t = _rope(t, cos_ref[...], sin_ref[...])
                o_ref[:, col:col + ATT_HEAD_DIM] = t.astype(BF16)
        for g in range(ATT_KV_HEADS):
            col = OFF_ATT_V + g * ATT_HEAD_DIM
            if c0 <= col < c0 + tn:
                n_b, _, _, n_tok = vt_ref.shape
                v = res[:, col - c0:col - c0 + ATT_HEAD_DIM]
                for bb in range(n_b):
                    vt_ref[bb, g, :ATT_HEAD_DIM, :] = v[bb * n_tok:(bb + 1) * n_tok].T.astype(BF16)
                    vt_ref[bb, g, ATT_HEAD_DIM:, :] = jnp.ones((VT_ROWS - ATT_HEAD_DIM, n_tok), BF16)


def _inproj_rows(x2d, norm_w, scale, shift, w_bf, q_norm_w, k_norm_w, cos, sin_signed,
                 n_cols, rows_per_mod, tokens, tm, tn, rope):
    rows = x2d.shape[0]
    if tokens >= tm:
        per_batch = tokens // tm
        vt_block = (1, ATT_KV_HEADS, VT_ROWS, tm)
        vt_map = lambda i: (i // per_batch, 0, 0, i % per_batch)
    else:
        vt_block = (tm // tokens, ATT_KV_HEADS, VT_ROWS, tokens)
        vt_map = lambda i: (i, 0, 0, 0)
    pos_tiles = cos.shape[0] // tm
    mod_map = lambda i: ((i * tm) // rows_per_mod, 0, 0)
    vec = lambda i: (0, 0)
    return pl.pallas_call(
        functools.partial(_inproj_rows_kernel, tn=tn, rope=rope, prep_q=n_cols > OFF_ATT_Q),
        grid=(rows // tm,),
        in_specs=[pl.BlockSpec((tm, D_MODEL), lambda i: (i, 0)),
                  pl.BlockSpec((1, D_MODEL), vec),
                  pl.BlockSpec((1, 1, D_MODEL), mod_map),
                  pl.BlockSpec((1, 1, D_MODEL), mod_map),
                  pl.BlockSpec((D_MODEL, n_cols), vec, pipeline_mode=pl.Buffered(1)),
                  pl.BlockSpec((1, ATT_HEAD_DIM), vec),
                  pl.BlockSpec((1, ATT_HEAD_DIM), vec),
                  pl.BlockSpec((tm, ATT_HEAD_DIM), lambda i: (i % pos_tiles, 0)),
                  pl.BlockSpec((tm, ATT_HEAD_DIM), lambda i: (i % pos_tiles, 0))],
        out_specs=[pl.BlockSpec((tm, n_cols), lambda i: (i, 0)), pl.BlockSpec(vt_block, vt_map)],
        out_shape=[jax.ShapeDtypeStruct((rows, n_cols), BF16),
                   jax.ShapeDtypeStruct((rows // tokens, ATT_KV_HEADS, VT_ROWS, tokens), BF16)],
        compiler_params=pltpu.CompilerParams(
            dimension_semantics=("parallel",), vmem_limit_bytes=VMEM_LIMIT),
        name="inproj_rows",
    )(x2d, norm_w.reshape(1, -1), scale, shift, w_bf, q_norm_w.reshape(1, -1),
      k_norm_w.reshape(1, -1), cos, sin_signed)


KEY_TILE = 192
ATT_UNIT = 512
ATT_TQ = 2048


def _attention_kernel(q_ref, kl_ref, kc_ref, vtl_ref, vtc_ref, g_ref, o_ref):
    k = jnp.concatenate([kl_ref[0], kc_ref[0]], axis=0)
    vt = jnp.concatenate([vtl_ref[0, 0], vtc_ref[0, 0]], axis=1)

    tq = q_ref.shape[1]
    units = [(r, c0) for r in range(ATT_REP) for c0 in range(0, tq, ATT_UNIT)]

    def logits(u):
        r, c0 = units[u]
        q = q_ref[0, c0:c0 + ATT_UNIT, r * ATT_HEAD_DIM:(r + 1) * ATT_HEAD_DIM]
        s_t = _dot_nt(k, q)
        return s_t, jnp.max(s_t, axis=0, keepdims=True)

    def finish(u, s_t, m, s_next):
        r, c0 = units[u]
        rows = slice(c0, c0 + ATT_UNIT)
        cols = slice(r * ATT_HEAD_DIM, (r + 1) * ATT_HEAD_DIM)
        if s_next is None:
            p_t = jnp.exp2(s_t - m).astype(BF16)
        else:
            tiles = []
            for j in range(0, s_t.shape[0], KEY_TILE):
                probe = pltpu.bitcast(s_next[j + KEY_TILE - 8:j + KEY_TILE, :], jnp.uint32)
                zero = ((probe >> 16) >> 16).astype(F32)[0:1, :]
                tiles.append(jnp.exp2(s_t[j:j + KEY_TILE, :] - (m + zero)).astype(BF16))
            p_t = jnp.concatenate(tiles, axis=0)
        o_t = _dot(vt, p_t)
        o = (o_t[:ATT_HEAD_DIM] * (1.0 / o_t[ATT_HEAD_DIM:ATT_HEAD_DIM + 1])).T
        o_ref[0, rows, cols] = (o * g_ref[0, rows, cols].astype(F32)).astype(BF16)

    pending = logits(0)
    for u in range(1, len(units)):
        nxt = logits(u)
        finish(u - 1, *pending, nxt[0])
        pending = nxt
    finish(len(units) - 1, *pending, None)


def _attention(px3, pc3, vt_lat, vt_ctx):
    b, s_len, _ = px3.shape
    c_len = pc3.shape[1]
    gw = ATT_REP * ATT_HEAD_DIM
    tq = ATT_TQ
    k_col = lambda i, g, t: (i, 0, OFF_ATT_K // ATT_HEAD_DIM + g)
    return pl.pallas_call(
        _attention_kernel,
        grid=(b, ATT_KV_HEADS, s_len // tq),
        in_specs=[pl.BlockSpec((1, tq, gw), lambda i, g, t: (i, t, OFF_ATT_Q // gw + g)),
                  pl.BlockSpec((1, s_len, ATT_HEAD_DIM), k_col),
                  pl.BlockSpec((1, c_len, ATT_HEAD_DIM), k_col),
                  pl.BlockSpec((1, 1, VT_ROWS, s_len), lambda i, g, t: (i, g, 0, 0)),
                  pl.BlockSpec((1, 1, VT_ROWS, c_len), lambda i, g, t: (i, g, 0, 0)),
                  pl.BlockSpec((1, tq, gw), lambda i, g, t: (i, t, OFF_ATT_G // gw + g))],
        out_specs=pl.BlockSpec((1, tq, gw), lambda i, g, t: (i, t, g)),
        out_shape=jax.ShapeDtypeStruct((b, s_len, ATT_Q_W), BF16),
        compiler_params=pltpu.CompilerParams(
            dimension_semantics=("parallel", "parallel", "arbitrary"),
            vmem_limit_bytes=VMEM_LIMIT),
        name="attention",
    )(px3, px3, pc3, vt_lat, vt_ctx, px3)


def _retention_kernel(d_ref, q_ref, k_ref, v_ref, g_ref, kc_ref, vc_ref, o_ref, sb_ref):
    s_len = q_ref.shape[1]
    c_len = kc_ref.shape[1]
    ch = RET_CHUNK
    n_chunks = s_len // ch

    d = d_ref[0]
    lg = jnp.log1p(-jnp.exp2(d))
    lg_f = lg[0:1, :]
    lg_b = lg[1:2, :]

    idx = lax.broadcasted_iota(jnp.int32, (ch, 1), 0).astype(F32)
    qd_f = (jnp.exp(lg_f * (idx + 1.0)) * (RET_DK ** -0.5)).astype(BF16)
    kd_f = jnp.exp(lg_f * (ch - 1.0 - idx)).astype(BF16)
    qd_b = (jnp.exp(lg_b * (ch - idx)) * (RET_DK ** -0.5)).astype(BF16)
    kd_b = jnp.exp(lg_b * idx).astype(BF16)
    cd_f = jnp.exp(lg_f * ch)
    cd_b = jnp.exp(lg_b * ch)

    row = lax.broadcasted_iota(jnp.int32, (ch, ch), 0)
    col = lax.broadcasted_iota(jnp.int32, (ch, ch), 1)
    rel = (row - col).astype(F32)
    mask = jnp.where(rel > 0, jnp.exp(lg_f * rel),
                     jnp.where(rel < 0, jnp.exp(-lg_b * rel), 2.0)) * (RET_DK ** -0.5)

    def chunk(n):
        return slice(n * ch, (n + 1) * ch)

    def masked_scores(n):
        return (_dot_nt(q_ref[0, chunk(n), :], k_ref[0, chunk(n), :]) * mask).astype(BF16)

    pos = lax.broadcasted_iota(jnp.int32, (c_len, 1), 0).astype(F32)
    kc = kc_ref[0].astype(F32)
    vc = vc_ref[0]
    s_f = _dot_tn((kc * jnp.exp(lg_f * (c_len - 1.0 - pos))).astype(BF16), vc)
    s_b = _dot_tn((kc * jnp.exp(lg_b * pos)).astype(BF16), vc)

    for n in reversed(range(n_chunks)):
        sb_ref[n] = s_b.astype(BF16)
        if n > 0:
            s_b = s_b * cd_b + _dot_tn(k_ref[0, chunk(n), :] * kd_b, v_ref[0, chunk(n), :])

    scores = masked_scores(0)
    for n in range(n_chunks):
        q = q_ref[0, chunk(n), :]
        v = v_ref[0, chunk(n), :]
        last = n + 1 == n_chunks
        next_scores = None if last else masked_scores(n + 1)
        kv_f = None if last else _dot_tn(k_ref[0, chunk(n), :] * kd_f, v)
        o = _dot(q * qd_b, sb_ref[n]) + _dot(q * qd_f, s_f.astype(BF16)) + _dot(scores, v)
        ms = jnp.mean(o * o, axis=-1, keepdims=True)
        if not last:
            probe = pltpu.bitcast(next_scores[0:16, 0:128], jnp.uint32)
            ms = ms + ((probe >> 16) >> 16).astype(F32)[0:1, 0:1]
        o = (o * lax.rsqrt(ms + NORM_EPS)).astype(BF16)
        o_ref[0, chunk(n), :] = o * g_ref[0, chunk(n), :]
        if not last:
            s_f = s_f * cd_f + kv_f
            scores = next_scores


def _retention(px3, pc3, decay3):
    b, s_len, _ = px3.shape
    c_len = pc3.shape[1]
    return pl.pallas_call(
        _retention_kernel,
        grid=(b, RET_HEADS),
        in_specs=[pl.BlockSpec((1, 2, 1), lambda i, h: (h, 0, 0)),
                  pl.BlockSpec((1, s_len, RET_DK), lambda i, h: (i, 0, OFF_RET_Q // RET_DK + h)),
                  pl.BlockSpec((1, s_len, RET_DK), lambda i, h: (i, 0, OFF_RET_K // RET_DK + h)),
                  pl.BlockSpec((1, s_len, RET_DV), lambda i, h: (i, 0, OFF_RET_V // RET_DV + h)),
                  pl.BlockSpec((1, s_len, RET_DV), lambda i, h: (i, 0, OFF_RET_G // RET_DV + h)),
                  pl.BlockSpec((1, c_len, RET_DK), lambda i, h: (i, 0, h)),
                  pl.BlockSpec((1, c_len, RET_DV), lambda i, h: (i, 0, RET_QK_W // RET_DV + h))],
        out_specs=pl.BlockSpec((1, s_len, RET_DV), lambda i, h: (i, 0, h)),
        out_shape=jax.ShapeDtypeStruct((b, s_len, RET_V_W), BF16),
        scratch_shapes=[pltpu.VMEM((s_len // RET_CHUNK, RET_DK, RET_DV), BF16)],
        compiler_params=pltpu.CompilerParams(
            dimension_semantics=("parallel", "arbitrary"), vmem_limit_bytes=VMEM_LIMIT),
        name="retention",
    )(decay3, px3, px3, px3, px3, pc3, pc3)


def _merge_kernel(x_ref, yr_ref, ya_ref, mr0_ref, mr1_ref, ma0_ref, ma1_ref, gate_ref,
                  wr_ref, wa_ref, wo_ref, o_ref):
    a = _dot(yr_ref[...], wr_ref[...])
    b = _dot(ya_ref[...], wa_ref[...])
    mg_ret = jnp.concatenate([mr0_ref[...], mr1_ref[...]], axis=-1).astype(F32)
    mg_att = jnp.concatenate([ma0_ref[...], ma1_ref[...]], axis=-1).astype(F32)
    y = mg_ret * a + mg_att * b
    out = _dot(y.astype(BF16), wo_ref[...])
    o_ref[...] = x_ref[...] + gate_ref[0] * out


def _merge(x2d, y_ret, y_att, px, gate, w_o_ret, w_o_att, w_out, rows_per_mod, tm):
    rows = x2d.shape[0]
    half = D_MODEL // 2
    weight = lambda shape: pl.BlockSpec(shape, lambda i: (0, 0), pipeline_mode=pl.Buffered(1))
    mg = lambda off: pl.BlockSpec((tm, half), lambda i: (i, off // half))
    return pl.pallas_call(
        _merge_kernel,
        grid=(rows // tm,),
        in_specs=[pl.BlockSpec((tm, D_MODEL), lambda i: (i, 0)),
                  pl.BlockSpec((tm, RET_V_W), lambda i: (i, 0)),
                  pl.BlockSpec((tm, ATT_Q_W), lambda i: (i, 0)),
                  mg(OFF_MG_RET), mg(OFF_MG_RET + half), mg(OFF_MG_ATT), mg(OFF_MG_ATT + half),
                  pl.BlockSpec((1, 1, D_MODEL), lambda i: ((i * tm) // rows_per_mod, 0, 0)),
                  weight((RET_V_W, D_MODEL)), weight((ATT_Q_W, D_MODEL)), weight((D_MODEL, D_MODEL))],
        out_specs=pl.BlockSpec((tm, D_MODEL), lambda i: (i, 0)),
        out_shape=jax.ShapeDtypeStruct((rows, D_MODEL), F32),
        compiler_params=pltpu.CompilerParams(
            dimension_semantics=("parallel",), vmem_limit_bytes=VMEM_LIMIT),
        name="merge",
    )(x2d, y_ret, y_att, px, px, px, px, gate, w_o_ret, w_o_att, w_out)


def _rope_tables(s_len):
    rows = s_len // GRID_W
    row = jnp.repeat(jnp.arange(rows, dtype=F32), GRID_W)
    col = jnp.tile(jnp.arange(GRID_W, dtype=F32), rows)
    half = ATT_HEAD_DIM // 2
    freqs = ROPE_THETA ** (-jnp.arange(0, half, 2, dtype=F32) / half)
    ang = jnp.concatenate([row[:, None] * freqs, col[:, None] * freqs], axis=-1)
    cos, sin = jnp.cos(ang), jnp.sin(ang)
    return jnp.concatenate([cos, cos], axis=-1), jnp.concatenate([-sin, sin], axis=-1)


def _split_pairs(t):
    lead = t.shape[:-1]
    t = t.reshape(*lead, -1, ATT_HEAD_DIM // 2, 2)
    return jnp.swapaxes(t, -1, -2).reshape(*lead, -1)


def _layer(x, ctx, c, c_ctx, norm_w, w_ada, b_ada, w_in, ret_log2_decay,
           q_norm_w, k_norm_w, w_o_ret, w_o_att, w_out):
    b, s_len, _ = x.shape
    c_len = ctx.shape[1]

    mod = _adaln(jnp.concatenate([c, c_ctx[None]], axis=0), w_ada, b_ada)
    shift, scale, gate = (mod[:, i * D_MODEL:(i + 1) * D_MODEL] for i in range(3))
    lat = lambda t: t[:b].reshape(b, 1, D_MODEL)
    cx = lambda t: t[b:].reshape(1, 1, D_MODEL)

    w_bf = w_in.astype(BF16)
    for off, width in ((OFF_ATT_K, ATT_KV_W), (OFF_ATT_Q, ATT_Q_W)):
        w_bf = w_bf.at[:, off:off + width].set(_split_pairs(w_bf[:, off:off + width]))
    q_norm_w = _split_pairs(q_norm_w)
    k_norm_w = _split_pairs(k_norm_w)

    cos, sin_signed = _rope_tables(s_len)
    x2d = x.reshape(b * s_len, D_MODEL)
    px, vt_lat = _inproj_rows(x2d, norm_w, lat(scale), lat(shift), w_bf, q_norm_w, k_norm_w,
                              cos, sin_signed, IN_COLS, s_len, s_len, PROJ_TM, PROJ_TN, rope=True)
    pc, vt_ctx = _inproj_rows(ctx.reshape(b * c_len, D_MODEL), norm_w, cx(scale), cx(shift), w_bf,
                              q_norm_w, k_norm_w, cos, sin_signed,
                              KV_COLS, b * c_len, c_len, PROJ_TM, PROJ_TN_CTX, rope=False)
    px3 = px.reshape(b, s_len, IN_COLS)
    pc3 = pc.reshape(b, c_len, KV_COLS)

    y_att = _attention(px3, pc3, vt_lat, vt_ctx)

    decay3 = ret_log2_decay.astype(F32).T.reshape(RET_HEADS, 2, 1)
    y_ret = _retention(px3, pc3, decay3)

    x_new = _merge(x2d, y_ret.reshape(b * s_len, RET_V_W), y_att.reshape(b * s_len, ATT_Q_W), px,
                   lat(gate), w_o_ret.astype(BF16), w_o_att.astype(BF16), w_out.astype(BF16),
                   s_len, MERGE_TM)
    return x_new.reshape(b, s_len, D_MODEL)


def kernel(x, c, ctx, c_ctx, norm_w, w_ada, b_ada, w_in, ret_log2_decay, q_norm_w, k_norm_w,
           w_o_ret, w_o_att, w_out):
    depth = norm_w.shape[0]
    assert depth == 1, "context update between layers is not implemented"
    return _layer(x, ctx, c, c_ctx, norm_w[0], w_ada[0], b_ada[0], w_in[0], ret_log2_decay[0],
                  q_norm_w[0], k_norm_w[0], w_o_ret[0], w_o_att[0], w_out[0])
```

```python
import functools

import jax
import jax.numpy as jnp
from jax import lax
from jax.experimental import pallas as pl
from jax.experimental.pallas import tpu as pltpu

D_MODEL = 1024
GRID_W = 64
RET_HEADS = 4
RET_DK = 256
RET_DV = 512
RET_QK_W = RET_HEADS * RET_DK
RET_V_W = RET_HEADS * RET_DV
ATT_HEADS = 8
ATT_KV_HEADS = 2
ATT_HEAD_DIM = 128
ATT_Q_W = ATT_HEADS * ATT_HEAD_DIM
ATT_KV_W = ATT_KV_HEADS * ATT_HEAD_DIM
ATT_REP = ATT_HEADS // ATT_KV_HEADS
ROPE_THETA = 10000.0
NORM_EPS = 1e-6
KV_COLS = RET_QK_W + RET_V_W + 2 * ATT_KV_W
LOG2E = 1.4426950408889634

OFF_RET_K = 0
OFF_RET_V = OFF_RET_K + RET_QK_W
OFF_ATT_K = OFF_RET_V + RET_V_W
OFF_ATT_V = OFF_ATT_K + ATT_KV_W
OFF_RET_Q = OFF_ATT_V + ATT_KV_W
OFF_RET_G = OFF_RET_Q + RET_QK_W
OFF_ATT_Q = OFF_RET_G + RET_V_W
OFF_ATT_G = OFF_ATT_Q + ATT_Q_W
OFF_MG_RET = OFF_ATT_G + ATT_Q_W
OFF_MG_ATT = OFF_MG_RET + D_MODEL
IN_COLS = OFF_MG_ATT + D_MODEL
assert OFF_RET_Q == KV_COLS

RET_CHUNK = 256
PROJ_TM = 512
PROJ_TN = 1536
PROJ_TN_CTX = 1792
MERGE_TM = 1024
BF16_SUBLANES = 16
VT_ROWS = ATT_HEAD_DIM + BF16_SUBLANES
VMEM_LIMIT = 56 * 1024 * 1024

BF16 = jnp.bfloat16
F32 = jnp.float32


def _dot(a, b):
    return jnp.dot(a, b, preferred_element_type=F32)


def _dot_nt(a, b):
    return lax.dot_general(a, b, (((1,), (1,)), ((), ())), preferred_element_type=F32)


def _dot_tn(a, b):
    return lax.dot_general(a, b, (((0,), (0,)), ((), ())), preferred_element_type=F32)


def _silu(x):
    return x * jax.nn.sigmoid(x)


def _silu_tanh(x):
    h = 0.5 * x
    return h * jnp.tanh(h) + h


def _sigmoid_tanh(x):
    return 0.5 * jnp.tanh(0.5 * x) + 0.5


GATE_COLS = ((OFF_RET_G, OFF_RET_G + RET_V_W, _silu_tanh),
             (OFF_ATT_G, OFF_ATT_G + ATT_Q_W, _silu_tanh),
             (OFF_MG_RET, OFF_MG_ATT + D_MODEL, _sigmoid_tanh))


def _adaln_kernel(c_ref, w_ref, b_ref, o_ref):
    o_ref[...] = _dot(_silu(c_ref[...]), w_ref[...]) + b_ref[...]


def _adaln(cc, w_ada, b_ada):
    rows = cc.shape[0]
    tn = D_MODEL
    return pl.pallas_call(
        _adaln_kernel,
        grid=(3 * D_MODEL // tn,),
        in_specs=[pl.BlockSpec((rows, D_MODEL), lambda j: (0, 0)),
                  pl.BlockSpec((D_MODEL, tn), lambda j: (0, j)),
                  pl.BlockSpec((1, tn), lambda j: (0, j))],
        out_specs=pl.BlockSpec((rows, tn), lambda j: (0, j)),
        out_shape=jax.ShapeDtypeStruct((rows, 3 * D_MODEL), F32),
        compiler_params=pltpu.CompilerParams(vmem_limit_bytes=VMEM_LIMIT),
        name="adaln",
    )(cc, w_ada, b_ada.reshape(1, -1))


def _head_rms(t, w):
    ms = jnp.mean(t * t, axis=-1, keepdims=True)
    return t * lax.rsqrt(ms + NORM_EPS) * w


def _rope(t, cos, sin_signed):
    return t * cos + pltpu.roll(t, ATT_HEAD_DIM // 2, 1) * sin_signed


def _inproj_rows_kernel(x_ref, nw_ref, sc_ref, sh_ref, w_ref, qw_ref, kw_ref, cos_ref, sin_ref,
                        o_ref, vt_ref, *, tn, rope, prep_q):
    x = x_ref[...]
    ms = jnp.mean(x * x, axis=-1, keepdims=True)
    y = x * lax.rsqrt(ms + NORM_EPS) * nw_ref[...]
    h = (y * (1.0 + sc_ref[0]) + sh_ref[0]).astype(BF16)
    n_cols = o_ref.shape[1]
    heads = [(OFF_ATT_K + g * ATT_HEAD_DIM, kw_ref[...]) for g in range(ATT_KV_HEADS)]
    if prep_q:
        qw = qw_ref[...] * (ATT_HEAD_DIM ** -0.5 * LOG2E)
        heads += [(OFF_ATT_Q + r * ATT_HEAD_DIM, qw) for r in range(ATT_HEADS)]
    for c0 in range(0, n_cols, tn):
        res = _dot(h, w_ref[:, c0:c0 + tn])
        edges = sorted({c0, c0 + tn} | {e for lo, hi, _ in GATE_COLS for e in (lo, hi) if c0 < e < c0 + tn})
        for lo, hi in zip(edges[:-1], edges[1:]):
            piece = res[:, lo - c0:hi - c0]
            for g_lo, g_hi, act in GATE_COLS:
                if g_lo <= lo < g_hi:
                    piece = act(piece)
            o_ref[:, lo:hi] = piece.astype(BF16)
        for col, w in heads:
            if c0 <= col < c0 + tn:
                t = _head_rms(res[:, col - c0:col - c0 + ATT_HEAD_DIM], w)
                if rope:
                    t = _rope(t, cos_ref[...], sin_ref[...])
                o_ref[:, col:col + ATT_HEAD_DIM] = t.astype(BF16)
        for g in range(ATT_KV_HEADS):
            col = OFF_ATT_V + g * ATT_HEAD_DIM
            if c0 <= col < c0 + tn:
                n_b, _, _, n_tok = vt_ref.shape
                v = res[:, col - c0:col - c0 + ATT_HEAD_DIM]
                for bb in range(n_b):
                    vt_ref[bb, g, :ATT_HEAD_DIM, :] = v[bb * n_tok:(bb + 1) * n_tok].T.astype(BF16)
                    vt_ref[bb, g, ATT_HEAD_DIM:, :] = jnp.ones((VT_ROWS - ATT_HEAD_DIM, n_tok), BF16)


def _inproj_rows(x2d, norm_w, scale, shift, w_bf, q_norm_w, k_norm_w, cos, sin_signed,
                 n_cols, rows_per_mod, tokens, tm, tn, rope):
    rows = x2d.shape[0]
    if tokens >= tm:
        per_batch = tokens // tm
        vt_block = (1, ATT_KV_HEADS, VT_ROWS, tm)
        vt_map = lambda i: (i // per_batch, 0, 0, i % per_batch)
    else:
        vt_block = (tm // tokens, ATT_KV_HEADS, VT_ROWS, tokens)
        vt_map = lambda i: (i, 0, 0, 0)
    pos_tiles = cos.shape[0] // tm
    mod_map = lambda i: ((i * tm) // rows_per_mod, 0, 0)
    vec = lambda i: (0, 0)
    return pl.pallas_call(
        functools.partial(_inproj_rows_kernel, tn=tn, rope=rope, prep_q=n_cols > OFF_ATT_Q),
        grid=(rows // tm,),
        in_specs=[pl.BlockSpec((tm, D_MODEL), lambda i: (i, 0)),
                  pl.BlockSpec((1, D_MODEL), vec),
                  pl.BlockSpec((1, 1, D_MODEL), mod_map),
                  pl.BlockSpec((1, 1, D_MODEL), mod_map),
                  pl.BlockSpec((D_MODEL, n_cols), vec, pipeline_mode=pl.Buffered(1)),
                  pl.BlockSpec((1, ATT_HEAD_DIM), vec),
                  pl.BlockSpec((1, ATT_HEAD_DIM), vec),
                  pl.BlockSpec((tm, ATT_HEAD_DIM), lambda i: (i % pos_tiles, 0)),
                  pl.BlockSpec((tm, ATT_HEAD_DIM), lambda i: (i % pos_tiles, 0))],
        out_specs=[pl.BlockSpec((tm, n_cols), lambda i: (i, 0)), pl.BlockSpec(vt_block, vt_map)],
        out_shape=[jax.ShapeDtypeStruct((rows, n_cols), BF16),
                   jax.ShapeDtypeStruct((rows // tokens, ATT_KV_HEADS, VT_ROWS, tokens), BF16)],
        compiler_params=pltpu.CompilerParams(
            dimension_semantics=("parallel",), vmem_limit_bytes=VMEM_LIMIT),
        name="inproj_rows",
    )(x2d, norm_w.reshape(1, -1), scale, shift, w_bf, q_norm_w.reshape(1, -1),
      k_norm_w.reshape(1, -1), cos, sin_signed)


KEY_TILE = 288
ATT_UNIT = 512
ATT_TQ = 2048


def _attention_kernel(q_ref, kl_ref, kc_ref, vtl_ref, vtc_ref, g_ref, o_ref):
    k = jnp.concatenate([kl_ref[0], kc_ref[0]], axis=0)
    vt = jnp.concatenate([vtl_ref[0, 0], vtc_ref[0, 0]], axis=1)

    tq = q_ref.shape[1]
    units = [(r, c0) for r in range(ATT_REP) for c0 in range(0, tq, ATT_UNIT)]

    def logits(u):
        r, c0 = units[u]
        q = q_ref[0, c0:c0 + ATT_UNIT, r * ATT_HEAD_DIM:(r + 1) * ATT_HEAD_DIM]
        s_t = _dot_nt(k, q)
        return s_t, jnp.max(s_t, axis=0, keepdims=True)

    def finish(u, s_t, m, s_next):
        r, c0 = units[u]
        rows = slice(c0, c0 + ATT_UNIT)
        cols = slice(r * ATT_HEAD_DIM, (r + 1) * ATT_HEAD_DIM)
        if s_next is None:
            p_t = jnp.exp2(s_t - m).astype(BF16)
        else:
            tiles = []
            for j in range(0, s_t.shape[0], KEY_TILE):
                probe = pltpu.bitcast(s_next[j + KEY_TILE - 8:j + KEY_TILE, :], jnp.uint32)
                zero = ((probe >> 16) >> 16).astype(F32)[0:1, :]
                tiles.append(jnp.exp2(s_t[j:j + KEY_TILE, :] - (m + zero)).astype(BF16))
            p_t = jnp.concatenate(tiles, axis=0)
        o_t = _dot(vt, p_t)
        o = (o_t[:ATT_HEAD_DIM] * (1.0 / o_t[ATT_HEAD_DIM:ATT_HEAD_DIM + 1])).T
        o_ref[0, rows, cols] = (o * g_ref[0, rows, cols].astype(F32)).astype(BF16)

    pending = logits(0)
    for u in range(1, len(units)):
        nxt = logits(u)
        finish(u - 1, *pending, nxt[0])
        pending = nxt
    finish(len(units) - 1, *pending, None)


def _attention(px3, pc3, vt_lat, vt_ctx):
    b, s_len, _ = px3.shape
    c_len = pc3.shape[1]
    gw = ATT_REP * ATT_HEAD_DIM
    tq = ATT_TQ
    k_col = lambda i, g, t: (i, 0, OFF_ATT_K // ATT_HEAD_DIM + g)
    return pl.pallas_call(
        _attention_kernel,
        grid=(b, ATT_KV_HEADS, s_len // tq),
        in_specs=[pl.BlockSpec((1, tq, gw), lambda i, g, t: (i, t, OFF_ATT_Q // gw + g)),
                  pl.BlockSpec((1, s_len, ATT_HEAD_DIM), k_col),
                  pl.BlockSpec((1, c_len, ATT_HEAD_DIM), k_col),
                  pl.BlockSpec((1, 1, VT_ROWS, s_len), lambda i, g, t: (i, g, 0, 0)),
                  pl.BlockSpec((1, 1, VT_ROWS, c_len), lambda i, g, t: (i, g, 0, 0)),
                  pl.BlockSpec((1, tq, gw), lambda i, g, t: (i, t, OFF_ATT_G // gw + g))],
        out_specs=pl.BlockSpec((1, tq, gw), lambda i, g, t: (i, t, g)),
        out_shape=jax.ShapeDtypeStruct((b, s_len, ATT_Q_W), BF16),
        compiler_params=pltpu.CompilerParams(
            dimension_semantics=("parallel", "parallel", "arbitrary"),
            vmem_limit_bytes=VMEM_LIMIT),
        name="attention",
    )(px3, px3, pc3, vt_lat, vt_ctx, px3)


def _retention_kernel(d_ref, q_ref, k_ref, v_ref, g_ref, kc_ref, vc_ref, o_ref, sb_ref):
    s_len = q_ref.shape[1]
    c_len = kc_ref.shape[1]
    ch = RET_CHUNK
    n_chunks = s_len // ch

    d = d_ref[0]
    lg = jnp.log1p(-jnp.exp2(d))
    lg_f = lg[0:1, :]
    lg_b = lg[1:2, :]

    idx = lax.broadcasted_iota(jnp.int32, (ch, 1), 0).astype(F32)
    qd_f = (jnp.exp(lg_f * (idx + 1.0)) * (RET_DK ** -0.5)).astype(BF16)
    kd_f = jnp.exp(lg_f * (ch - 1.0 - idx)).astype(BF16)
    qd_b = (jnp.exp(lg_b * (ch - idx)) * (RET_DK ** -0.5)).astype(BF16)
    kd_b = jnp.exp(lg_b * idx).astype(BF16)
    cd_f = jnp.exp(lg_f * ch)
    cd_b = jnp.exp(lg_b * ch)

    row = lax.broadcasted_iota(jnp.int32, (ch, ch), 0)
    col = lax.broadcasted_iota(jnp.int32, (ch, ch), 1)
    rel = (row - col).astype(F32)
    mask = jnp.where(rel > 0, jnp.exp(lg_f * rel),
                     jnp.where(rel < 0, jnp.exp(-lg_b * rel), 2.0)) * (RET_DK ** -0.5)

    def chunk(n):
        return slice(n * ch, (n + 1) * ch)

    def masked_scores(n):
        return (_dot_nt(q_ref[0, chunk(n), :], k_ref[0, chunk(n), :]) * mask).astype(BF16)

    pos = lax.broadcasted_iota(jnp.int32, (c_len, 1), 0).astype(F32)
    kc = kc_ref[0].astype(F32)
    vc = vc_ref[0]
    s_f = _dot_tn((kc * jnp.exp(lg_f * (c_len - 1.0 - pos))).astype(BF16), vc)
    s_b = _dot_tn((kc * jnp.exp(lg_b * pos)).astype(BF16), vc)

    for n in reversed(range(n_chunks)):
        sb_ref[n] = s_b.astype(BF16)
        if n > 0:
            s_b = s_b * cd_b + _dot_tn(k_ref[0, chunk(n), :] * kd_b, v_ref[0, chunk(n), :])

    scores = masked_scores(0)
    for n in range(n_chunks):
        q = q_ref[0, chunk(n), :]
        v = v_ref[0, chunk(n), :]
        last = n + 1 == n_chunks
        next_scores = None if last else masked_scores(n + 1)
        kv_f = None if last else _dot_tn(k_ref[0, chunk(n), :] * kd_f, v)
        o = _dot(q * qd_b, sb_ref[n]) + _dot(q * qd_f, s_f.astype(BF16)) + _dot(scores, v)
        ms = jnp.mean(o * o, axis=-1, keepdims=True)
        if not last:
            probe = pltpu.bitcast(next_scores[0:16, 0:128], jnp.uint32)
            ms = ms + ((probe >> 16) >> 16).astype(F32)[0:1, 0:1]
        o = (o * lax.rsqrt(ms + NORM_EPS)).astype(BF16)
        o_ref[0, chunk(n), :] = o * g_ref[0, chunk(n), :]
        if not last:
            s_f = s_f * cd_f + kv_f
            scores = next_scores


def _retention(px3, pc3, decay3):
    b, s_len, _ = px3.shape
    c_len = pc3.shape[1]
    return pl.pallas_call(
        _retention_kernel,
        grid=(b, RET_HEADS),
        in_specs=[pl.BlockSpec((1, 2, 1), lambda i, h: (h, 0, 0)),
                  pl.BlockSpec((1, s_len, RET_DK), lambda i, h: (i, 0, OFF_RET_Q // RET_DK + h)),
                  pl.BlockSpec((1, s_len, RET_DK), lambda i, h: (i, 0, OFF_RET_K // RET_DK + h)),
                  pl.BlockSpec((1, s_len, RET_DV), lambda i, h: (i, 0, OFF_RET_V // RET_DV + h)),
                  pl.BlockSpec((1, s_len, RET_DV), lambda i, h: (i, 0, OFF_RET_G // RET_DV + h)),
                  pl.BlockSpec((1, c_len, RET_DK), lambda i, h: (i, 0, h)),
                  pl.BlockSpec((1, c_len, RET_DV), lambda i, h: (i, 0, RET_QK_W // RET_DV + h))],
        out_specs=pl.BlockSpec((1, s_len, RET_DV), lambda i, h: (i, 0, h)),
        out_shape=jax.ShapeDtypeStruct((b, s_len, RET_V_W), BF16),
        scratch_shapes=[pltpu.VMEM((s_len // RET_CHUNK, RET_DK, RET_DV), BF16)],
        compiler_params=pltpu.CompilerParams(
            dimension_semantics=("parallel", "arbitrary"), vmem_limit_bytes=VMEM_LIMIT),
        name="retention",
    )(decay3, px3, px3, px3, px3, pc3, pc3)


def _merge_kernel(x_ref, yr_ref, ya_ref, mr0_ref, mr1_ref, ma0_ref, ma1_ref, gate_ref,
                  wr_ref, wa_ref, wo_ref, o_ref):
    a = _dot(yr_ref[...], wr_ref[...])
    b = _dot(ya_ref[...], wa_ref[...])
    mg_ret = jnp.concatenate([mr0_ref[...], mr1_ref[...]], axis=-1).astype(F32)
    mg_att = jnp.concatenate([ma0_ref[...], ma1_ref[...]], axis=-1).astype(F32)
    y = mg_ret * a + mg_att * b
    out = _dot(y.astype(BF16), wo_ref[...])
    o_ref[...] = x_ref[...] + gate_ref[0] * out


def _merge(x2d, y_ret, y_att, px, gate, w_o_ret, w_o_att, w_out, rows_per_mod, tm):
    rows = x2d.shape[0]
    half = D_MODEL // 2
    weight = lambda shape: pl.BlockSpec(shape, lambda i: (0, 0), pipeline_mode=pl.Buffered(1))
    mg = lambda off: pl.BlockSpec((tm, half), lambda i: (i, off // half))
    return pl.pallas_call(
        _merge_kernel,
        grid=(rows // tm,),
        in_specs=[pl.BlockSpec((tm, D_MODEL), lambda i: (i, 0)),
                  pl.BlockSpec((tm, RET_V_W), lambda i: (i, 0)),
                  pl.BlockSpec((tm, ATT_Q_W), lambda i: (i, 0)),
                  mg(OFF_MG_RET), mg(OFF_MG_RET + half), mg(OFF_MG_ATT), mg(OFF_MG_ATT + half),
                  pl.BlockSpec((1, 1, D_MODEL), lambda i: ((i * tm) // rows_per_mod, 0, 0)),
                  weight((RET_V_W, D_MODEL)), weight((ATT_Q_W, D_MODEL)), weight((D_MODEL, D_MODEL))],
        out_specs=pl.BlockSpec((tm, D_MODEL), lambda i: (i, 0)),
        out_shape=jax.ShapeDtypeStruct((rows, D_MODEL), F32),
        compiler_params=pltpu.CompilerParams(
            dimension_semantics=("parallel",), vmem_limit_bytes=VMEM_LIMIT),
        name="merge",
    )(x2d, y_ret, y_att, px, px, px, px, gate, w_o_ret, w_o_att, w_out)


def _rope_tables(s_len):
    rows = s_len // GRID_W
    row = jnp.repeat(jnp.arange(rows, dtype=F32), GRID_W)
    col = jnp.tile(jnp.arange(GRID_W, dtype=F32), rows)
    half = ATT_HEAD_DIM // 2
    freqs = ROPE_THETA ** (-jnp.arange(0, half, 2, dtype=F32) / half)
    ang = jnp.concatenate([row[:, None] * freqs, col[:, None] * freqs], axis=-1)
    cos, sin = jnp.cos(ang), jnp.sin(ang)
    return jnp.concatenate([cos, cos], axis=-1), jnp.concatenate([-sin, sin], axis=-1)


def _split_pairs(t):
    lead = t.shape[:-1]
    t = t.reshape(*lead, -1, ATT_HEAD_DIM // 2, 2)
    return jnp.swapaxes(t, -1, -2).reshape(*lead, -1)


def _layer(x, ctx, c, c_ctx, norm_w, w_ada, b_ada, w_in, ret_log2_decay,
           q_norm_w, k_norm_w, w_o_ret, w_o_att, w_out):
    b, s_len, _ = x.shape
    c_len = ctx.shape[1]

    mod = _adaln(jnp.concatenate([c, c_ctx[None]], axis=0), w_ada, b_ada)
    shift, scale, gate = (mod[:, i * D_MODEL:(i + 1) * D_MODEL] for i in range(3))
    lat = lambda t: t[:b].reshape(b, 1, D_MODEL)
    cx = lambda t: t[b:].reshape(1, 1, D_MODEL)

    w_bf = w_in.astype(BF16)
    for off, width in ((OFF_ATT_K, ATT_KV_W), (OFF_ATT_Q, ATT_Q_W)):
        w_bf = w_bf.at[:, off:off + width].set(_split_pairs(w_bf[:, off:off + width]))
    q_norm_w = _split_pairs(q_norm_w)
    k_norm_w = _split_pairs(k_norm_w)

    cos, sin_signed = _rope_tables(s_len)
    x2d = x.reshape(b * s_len, D_MODEL)
    px, vt_lat = _inproj_rows(x2d, norm_w, lat(scale), lat(shift), w_bf, q_norm_w, k_norm_w,
                              cos, sin_signed, IN_COLS, s_len, s_len, PROJ_TM, PROJ_TN, rope=True)
    pc, vt_ctx = _inproj_rows(ctx.reshape(b * c_len, D_MODEL), norm_w, cx(scale), cx(shift), w_bf,
                              q_norm_w, k_norm_w, cos, sin_signed,
                              KV_COLS, b * c_len, c_len, PROJ_TM, PROJ_TN_CTX, rope=False)
    px3 = px.reshape(b, s_len, IN_COLS)
    pc3 = pc.reshape(b, c_len, KV_COLS)

    y_att = _attention(px3, pc3, vt_lat, vt_ctx)

    decay3 = ret_log2_decay.astype(F32).T.reshape(RET_HEADS, 2, 1)
    y_ret = _retention(px3, pc3, decay3)

    x_new = _merge(x2d, y_ret.reshape(b * s_len, RET_V_W), y_att.reshape(b * s_len, ATT_Q_W), px,
                   lat(gate), w_o_ret.astype(BF16), w_o_att.astype(BF16), w_out.astype(BF16),
                   s_len, MERGE_TM)
    return x_new.reshape(b, s_len, D_MODEL)


def kernel(x, c, ctx, c_ctx, norm_w, w_ada, b_ada, w_in, ret_log2_decay, q_norm_w, k_norm_w,
           w_o_ret, w_o_att, w_out):
    depth = norm_w.shape[0]
    assert depth == 1, "context update between layers is not implemented"
    return _layer(x, ctx, c, c_ctx, norm_w[0], w_ada[0], b_ada[0], w_in[0], ret_log2_decay[0],
                  q_norm_w[0], k_norm_w[0], w_o_ret[0], w_o_att[0], w_out[0])
```

```python
import functools

import jax
import jax.numpy as jnp
from jax import lax
from jax.experimental import pallas as pl
from jax.experimental.pallas import tpu as pltpu

D_MODEL = 1024
GRID_W = 64
RET_HEADS = 4
RET_DK = 256
RET_DV = 512
RET_QK_W = RET_HEADS * RET_DK
RET_V_W = RET_HEADS * RET_DV
ATT_HEADS = 8
ATT_KV_HEADS = 2
ATT_HEAD_DIM = 128
ATT_Q_W = ATT_HEADS * ATT_HEAD_DIM
ATT_KV_W = ATT_KV_HEADS * ATT_HEAD_DIM
ATT_REP = ATT_HEADS // ATT_KV_HEADS
ROPE_THETA = 10000.0
NORM_EPS = 1e-6
KV_COLS = RET_QK_W + RET_V_W + 2 * ATT_KV_W
LOG2E = 1.4426950408889634

OFF_RET_K = 0
OFF_RET_V = OFF_RET_K + RET_QK_W
OFF_ATT_K = OFF_RET_V + RET_V_W
OFF_ATT_V = OFF_ATT_K + ATT_KV_W
OFF_RET_Q = OFF_ATT_V + ATT_KV_W
OFF_RET_G = OFF_RET_Q + RET_QK_W
OFF_ATT_Q = OFF_RET_G + RET_V_W
OFF_ATT_G = OFF_ATT_Q + ATT_Q_W
OFF_MG_RET = OFF_ATT_G + ATT_Q_W
OFF_MG_ATT = OFF_MG_RET + D_MODEL
IN_COLS = OFF_MG_ATT + D_MODEL
assert OFF_RET_Q == KV_COLS

RET_CHUNK = 256
PROJ_TM = 512
PROJ_TN = 1536
PROJ_TN_CTX = 1792
MERGE_TM = 1024
BF16_SUBLANES = 16
VT_ROWS = ATT_HEAD_DIM + BF16_SUBLANES
VMEM_LIMIT = 56 * 1024 * 1024

BF16 = jnp.bfloat16
F32 = jnp.float32


def _dot(a, b):
    return jnp.dot(a, b, preferred_element_type=F32)


def _dot_nt(a, b):
    return lax.dot_general(a, b, (((1,), (1,)), ((), ())), preferred_element_type=F32)


def _dot_tn(a, b):
    return lax.dot_general(a, b, (((0,), (0,)), ((), ())), preferred_element_type=F32)


def _silu(x):
    return x * jax.nn.sigmoid(x)


def _silu_tanh(x):
    h = 0.5 * x
    return h * jnp.tanh(h) + h


def _sigmoid_tanh(x):
    return 0.5 * jnp.tanh(0.5 * x) + 0.5


GATE_COLS = ((OFF_RET_G, OFF_RET_G + RET_V_W, _silu_tanh),
             (OFF_ATT_G, OFF_ATT_G + ATT_Q_W, _silu_tanh),
             (OFF_MG_RET, OFF_MG_ATT + D_MODEL, _sigmoid_tanh))


def _adaln_kernel(c_ref, w_ref, b_ref, o_ref):
    o_ref[...] = _dot(_silu(c_ref[...]), w_ref[...]) + b_ref[...]


def _adaln(cc, w_ada, b_ada):
    rows = cc.shape[0]
    tn = D_MODEL
    return pl.pallas_call(
        _adaln_kernel,
        grid=(3 * D_MODEL // tn,),
        in_specs=[pl.BlockSpec((rows, D_MODEL), lambda j: (0, 0)),
                  pl.BlockSpec((D_MODEL, tn), lambda j: (0, j)),
                  pl.BlockSpec((1, tn), lambda j: (0, j))],
        out_specs=pl.BlockSpec((rows, tn), lambda j: (0, j)),
        out_shape=jax.ShapeDtypeStruct((rows, 3 * D_MODEL), F32),
        compiler_params=pltpu.CompilerParams(vmem_limit_bytes=VMEM_LIMIT),
        name="adaln",
    )(cc, w_ada, b_ada.reshape(1, -1))


def _head_rms(t, w):
    ms = jnp.mean(t * t, axis=-1, keepdims=True)
    return t * lax.rsqrt(ms + NORM_EPS) * w


def _rope(t, cos, sin_signed):
    return t * cos + pltpu.roll(t, ATT_HEAD_DIM // 2, 1) * sin_signed


def _inproj_rows_kernel(x_ref, nw_ref, sc_ref, sh_ref, w_ref, qw_ref, kw_ref, cos_ref, sin_ref,
                        o_ref, vt_ref, *, tn, rope, prep_q):
    x = x_ref[...]
    ms = jnp.mean(x * x, axis=-1, keepdims=True)
    y = x * lax.rsqrt(ms + NORM_EPS) * nw_ref[...]
    h = (y * (1.0 + sc_ref[0]) + sh_ref[0]).astype(BF16)
    n_cols = o_ref.shape[1]
    heads = [(OFF_ATT_K + g * ATT_HEAD_DIM, kw_ref[...]) for g in range(ATT_KV_HEADS)]
    if prep_q:
        qw = qw_ref[...] * (ATT_HEAD_DIM ** -0.5 * LOG2E)
        heads += [(OFF_ATT_Q + r * ATT_HEAD_DIM, qw) for r in range(ATT_HEADS)]
    for c0 in range(0, n_cols, tn):
        res = _dot(h, w_ref[:, c0:c0 + tn])
        edges = sorted({c0, c0 + tn} | {e for lo, hi, _ in GATE_COLS for e in (lo, hi) if c0 < e < c0 + tn})
        for lo, hi in zip(edges[:-1], edges[1:]):
            piece = res[:, lo - c0:hi - c0]
            for g_lo, g_hi, act in GATE_COLS:
                if g_lo <= lo < g_hi:
                    piece = act(piece)
            o_ref[:, lo:hi] = piece.astype(BF16)
        for col, w in heads:
            if c0 <= col < c0 + tn:
                t = _head_rms(res[:, col - c0:col - c0 + ATT_HEAD_DIM], w)
                if rope:
                    t = _rope(t, cos_ref[...], sin_ref[...])
                o_ref[:, col:col + ATT_HEAD_DIM] = t.astype(BF16)
        for g in range(ATT_KV_HEADS):
            col = OFF_ATT_V + g * ATT_HEAD_DIM
            if c0 <= col < c0 + tn:
                n_b, _, _, n_tok = vt_ref.shape
                v = res[:, col - c0:col - c0 + ATT_HEAD_DIM]
                for bb in range(n_b):
                    vt_ref[bb, g, :ATT_HEAD_DIM, :] = v[bb * n_tok:(bb + 1) * n_tok].T.astype(BF16)
                    vt_ref[bb, g, ATT_HEAD_DIM:, :] = jnp.ones((VT_ROWS - ATT_HEAD_DIM, n_tok), BF16)


def _inproj_rows(x2d, norm_w, scale, shift, w_bf, q_norm_w, k_norm_w, cos, sin_signed,
                 n_cols, rows_per_mod, tokens, tm, tn, rope):
    rows = x2d.shape[0]
    if tokens >= tm:
        per_batch = tokens // tm
        vt_block = (1, ATT_KV_HEADS, VT_ROWS, tm)
        vt_map = lambda i: (i // per_batch, 0, 0, i % per_batch)
    else:
        vt_block = (tm // tokens, ATT_KV_HEADS, VT_ROWS, tokens)
        vt_map = lambda i: (i, 0, 0, 0)
    pos_tiles = cos.shape[0] // tm
    mod_map = lambda i: ((i * tm) // rows_per_mod, 0, 0)
    vec = lambda i: (0, 0)
    return pl.pallas_call(
        functools.partial(_inproj_rows_kernel, tn=tn, rope=rope, prep_q=n_cols > OFF_ATT_Q),
        grid=(rows // tm,),
        in_specs=[pl.BlockSpec((tm, D_MODEL), lambda i: (i, 0)),
                  pl.BlockSpec((1, D_MODEL), vec),
                  pl.BlockSpec((1, 1, D_MODEL), mod_map),
                  pl.BlockSpec((1, 1, D_MODEL), mod_map),
                  pl.BlockSpec((D_MODEL, n_cols), vec, pipeline_mode=pl.Buffered(1)),
                  pl.BlockSpec((1, ATT_HEAD_DIM), vec),
                  pl.BlockSpec((1, ATT_HEAD_DIM), vec),
                  pl.BlockSpec((tm, ATT_HEAD_DIM), lambda i: (i % pos_tiles, 0)),
                  pl.BlockSpec((tm, ATT_HEAD_DIM), lambda i: (i % pos_tiles, 0))],
        out_specs=[pl.BlockSpec((tm, n_cols), lambda i: (i, 0)), pl.BlockSpec(vt_block, vt_map)],
        out_shape=[jax.ShapeDtypeStruct((rows, n_cols), BF16),
                   jax.ShapeDtypeStruct((rows // tokens, ATT_KV_HEADS, VT_ROWS, tokens), BF16)],
        compiler_params=pltpu.CompilerParams(
            dimension_semantics=("parallel",), vmem_limit_bytes=VMEM_LIMIT),
        name="inproj_rows",
    )(x2d, norm_w.reshape(1, -1), scale, shift, w_bf, q_norm_w.reshape(1, -1),
      k_norm_w.reshape(1, -1), cos, sin_signed)


KEY_TILE = 192
ATT_UNIT = 512
ATT_TQ = 2048


def _attention_kernel(q_ref, kl_ref, kc_ref, vtl_ref, vtc_ref, g_ref, o_ref):
    k = jnp.concatenate([kl_ref[0], kc_ref[0]], axis=0)
    vt = jnp.concatenate([vtl_ref[0, 0], vtc_ref[0, 0]], axis=1)

    tq = q_ref.shape[1]
    units = [(r, c0) for r in range(ATT_REP) for c0 in range(0, tq, ATT_UNIT)]

    def logits(u):
        r, c0 = units[u]
        q = q_ref[0, c0:c0 + ATT_UNIT, r * ATT_HEAD_DIM:(r + 1) * ATT_HEAD_DIM]
        s_t = _dot_nt(k, q)
        return s_t, jnp.max(s_t, axis=0, keepdims=True)

    def finish(u, s_t, m, s_next):
        r, c0 = units[u]
        rows = slice(c0, c0 + ATT_UNIT)
        cols = slice(r * ATT_HEAD_DIM, (r + 1) * ATT_HEAD_DIM)
        if s_next is None:
            p_t = jnp.exp2(s_t - m).astype(BF16)
        else:
            tiles = []
            for j in range(0, s_t.shape[0], KEY_TILE):
                probe = pltpu.bitcast(s_next[j + KEY_TILE - 8:j + KEY_TILE, :], jnp.uint32)
                zero = ((probe >> 16) >> 16).astype(F32)[0:1, :]
                tiles.append(jnp.exp2(s_t[j:j + KEY_TILE, :] - (m + zero)).astype(BF16))
            p_t = jnp.concatenate(tiles, axis=0)
        o_t = _dot(vt, p_t)
        o = (o_t[:ATT_HEAD_DIM] * (1.0 / o_t[ATT_HEAD_DIM:ATT_HEAD_DIM + 1])).T
        o_ref[0, rows, cols] = (o * g_ref[0, rows, cols].astype(F32)).astype(BF16)

    pending = logits(0)
    for u in range(1, len(units)):
        nxt = logits(u)
        finish(u - 1, *pending, nxt[0])
        pending = nxt
    finish(len(units) - 1, *pending, None)


def _attention(px3, pc3, vt_lat, vt_ctx):
    b, s_len, _ = px3.shape
    c_len = pc3.shape[1]
    gw = ATT_REP * ATT_HEAD_DIM
    tq = ATT_TQ
    k_col = lambda i, g, t: (i, 0, OFF_ATT_K // ATT_HEAD_DIM + g)
    return pl.pallas_call(
        _attention_kernel,
        grid=(b, ATT_KV_HEADS, s_len // tq),
        in_specs=[pl.BlockSpec((1, tq, gw), lambda i, g, t: (i, t, OFF_ATT_Q // gw + g)),
                  pl.BlockSpec((1, s_len, ATT_HEAD_DIM), k_col),
                  pl.BlockSpec((1, c_len, ATT_HEAD_DIM), k_col),
                  pl.BlockSpec((1, 1, VT_ROWS, s_len), lambda i, g, t: (i, g, 0, 0)),
                  pl.BlockSpec((1, 1, VT_ROWS, c_len), lambda i, g, t: (i, g, 0, 0)),
                  pl.BlockSpec((1, tq, gw), lambda i, g, t: (i, t, OFF_ATT_G // gw + g))],
        out_specs=pl.BlockSpec((1, tq, gw), lambda i, g, t: (i, t, g)),
        out_shape=jax.ShapeDtypeStruct((b, s_len, ATT_Q_W), BF16),
        compiler_params=pltpu.CompilerParams(
            dimension_semantics=("parallel", "parallel", "arbitrary"),
            vmem_limit_bytes=VMEM_LIMIT),
        name="attention",
    )(px3, px3, pc3, vt_lat, vt_ctx, px3)


def _retention_kernel(d_ref, q_ref, k_ref, v_ref, g_ref, kc_ref, vc_ref, o_ref, sb_ref):
    s_len = q_ref.shape[1]
    c_len = kc_ref.shape[1]
    ch = RET_CHUNK
    n_chunks = s_len // ch

    d = d_ref[0]
    lg = jnp.log1p(-jnp.exp2(d))
    lg_f = lg[0:1, :]
    lg_b = lg[1:2, :]

    idx = lax.broadcasted_iota(jnp.int32, (ch, 1), 0).astype(F32)
    qd_f = (jnp.exp(lg_f * (idx + 1.0)) * (RET_DK ** -0.5)).astype(BF16)
    kd_f = jnp.exp(lg_f * (ch - 1.0 - idx)).astype(BF16)
    qd_b = (jnp.exp(lg_b * (ch - idx)) * (RET_DK ** -0.5)).astype(BF16)
    kd_b = jnp.exp(lg_b * idx).astype(BF16)
    cd_f = jnp.exp(lg_f * ch)
    cd_b = jnp.exp(lg_b * ch)

    row = lax.broadcasted_iota(jnp.int32, (ch, ch), 0)
    col = lax.broadcasted_iota(jnp.int32, (ch, ch), 1)
    rel = (row - col).astype(F32)
    mask = jnp.where(rel > 0, jnp.exp(lg_f * rel),
                     jnp.where(rel < 0, jnp.exp(-lg_b * rel), 2.0)) * (RET_DK ** -0.5)

    def chunk(n):
        return slice(n * ch, (n + 1) * ch)

    def masked_scores(n):
        return (_dot_nt(q_ref[0, chunk(n), :], k_ref[0, chunk(n), :]) * mask).astype(BF16)

    pos = lax.broadcasted_iota(jnp.int32, (c_len, 1), 0).astype(F32)
    kc = kc_ref[0].astype(F32)
    vc = vc_ref[0]
    s_f = _dot_tn((kc * jnp.exp(lg_f * (c_len - 1.0 - pos))).astype(BF16), vc)
    s_b = _dot_tn((kc * jnp.exp(lg_b * pos)).astype(BF16), vc)

    for n in reversed(range(n_chunks)):
        sb_ref[n] = s_b.astype(BF16)
        if n > 0:
            s_b = s_b * cd_b + _dot_tn(k_ref[0, chunk(n), :] * kd_b, v_ref[0, chunk(n), :])

    scores = masked_scores(0)
    for n in range(n_chunks):
        q = q_ref[0, chunk(n), :]
        v = v_ref[0, chunk(n), :]
        last = n + 1 == n_chunks
        next_scores = None if last else masked_scores(n + 1)
        kv_f = None if last else _dot_tn(k_ref[0, chunk(n), :] * kd_f, v)
        o = _dot(q * qd_b, sb_ref[n]) + _dot(q * qd_f, s_f.astype(BF16)) + _dot(scores, v)
        ms = jnp.mean(o * o, axis=-1, keepdims=True)
        if not last:
            probe = pltpu.bitcast(next_scores[0:16, 0:128], jnp.uint32)
            ms = ms + ((probe >> 16) >> 16).astype(F32)[0:1, 0:1]
        o = (o * lax.rsqrt(ms + NORM_EPS)).astype(BF16)
        o_ref[0, chunk(n), :] = o * g_ref[0, chunk(n), :]
        if not last:
            s_f = s_f * cd_f + kv_f
            scores = next_scores


def _retention(px3, pc3, decay3):
    b, s_len, _ = px3.shape
    c_len = pc3.shape[1]
    return pl.pallas_call(
        _retention_kernel,
        grid=(b, RET_HEADS),
        in_specs=[pl.BlockSpec((1, 2, 1), lambda i, h: (h, 0, 0)),
                  pl.BlockSpec((1, s_len, RET_DK), lambda i, h: (i, 0, OFF_RET_Q // RET_DK + h)),
                  pl.BlockSpec((1, s_len, RET_DK), lambda i, h: (i, 0, OFF_RET_K // RET_DK + h)),
                  pl.BlockSpec((1, s_len, RET_DV), lambda i, h: (i, 0, OFF_RET_V // RET_DV + h)),
                  pl.BlockSpec((1, s_len, RET_DV), lambda i, h: (i, 0, OFF_RET_G // RET_DV + h)),
                  pl.BlockSpec((1, c_len, RET_DK), lambda i, h: (i, 0, h)),
                  pl.BlockSpec((1, c_len, RET_DV), lambda i, h: (i, 0, RET_QK_W // RET_DV + h))],
        out_specs=pl.BlockSpec((1, s_len, RET_DV), lambda i, h: (i, 0, h)),
        out_shape=jax.ShapeDtypeStruct((b, s_len, RET_V_W), BF16),
        scratch_shapes=[pltpu.VMEM((s_len // RET_CHUNK, RET_DK, RET_DV), BF16)],
        compiler_params=pltpu.CompilerParams(
            dimension_semantics=("parallel", "arbitrary"), vmem_limit_bytes=VMEM_LIMIT),
        name="retention",
    )(decay3, px3, px3, px3, px3, pc3, pc3)


def _merge_kernel(x_ref, yr_ref, ya_ref, mr0_ref, mr1_ref, ma0_ref, ma1_ref, gate_ref,
                  wr_ref, wa_ref, wo_ref, o_ref):
    half = x_ref.shape[0] // 2

    def branches(r):
        rows = slice(r * half, (r + 1) * half)
        a = _dot(yr_ref[rows, :], wr_ref[...])
        b = _dot(ya_ref[rows, :], wa_ref[...])
        mg_ret = jnp.concatenate([mr0_ref[rows, :], mr1_ref[rows, :]], axis=-1).astype(F32)
        mg_att = jnp.concatenate([ma0_ref[rows, :], ma1_ref[rows, :]], axis=-1).astype(F32)
        return (mg_ret * a + mg_att * b).astype(BF16)

    def finish(r, y):
        rows = slice(r * half, (r + 1) * half)
        o_ref[rows, :] = x_ref[rows, :] + gate_ref[0] * _dot(y, wo_ref[...])

    y0 = branches(0)
    y1 = branches(1)
    finish(0, y0)
    finish(1, y1)


def _merge(x2d, y_ret, y_att, px, gate, w_o_ret, w_o_att, w_out, rows_per_mod, tm):
    rows = x2d.shape[0]
    half = D_MODEL // 2
    weight = lambda shape: pl.BlockSpec(shape, lambda i: (0, 0), pipeline_mode=pl.Buffered(1))
    mg = lambda off: pl.BlockSpec((tm, half), lambda i: (i, off // half))
    return pl.pallas_call(
        _merge_kernel,
        grid=(rows // tm,),
        in_specs=[pl.BlockSpec((tm, D_MODEL), lambda i: (i, 0)),
                  pl.BlockSpec((tm, RET_V_W), lambda i: (i, 0)),
                  pl.BlockSpec((tm, ATT_Q_W), lambda i: (i, 0)),
                  mg(OFF_MG_RET), mg(OFF_MG_RET + half), mg(OFF_MG_ATT), mg(OFF_MG_ATT + half),
                  pl.BlockSpec((1, 1, D_MODEL), lambda i: ((i * tm) // rows_per_mod, 0, 0)),
                  weight((RET_V_W, D_MODEL)), weight((ATT_Q_W, D_MODEL)), weight((D_MODEL, D_MODEL))],
        out_specs=pl.BlockSpec((tm, D_MODEL), lambda i: (i, 0)),
        out_shape=jax.ShapeDtypeStruct((rows, D_MODEL), F32),
        compiler_params=pltpu.CompilerParams(
            dimension_semantics=("parallel",), vmem_limit_bytes=VMEM_LIMIT),
        name="merge",
    )(x2d, y_ret, y_att, px, px, px, px, gate, w_o_ret, w_o_att, w_out)


def _rope_tables(s_len):
    rows = s_len // GRID_W
    row = jnp.repeat(jnp.arange(rows, dtype=F32), GRID_W)
    col = jnp.tile(jnp.arange(GRID_W, dtype=F32), rows)
    half = ATT_HEAD_DIM // 2
    freqs = ROPE_THETA ** (-jnp.arange(0, half, 2, dtype=F32) / half)
    ang = jnp.concatenate([row[:, None] * freqs, col[:, None] * freqs], axis=-1)
    cos, sin = jnp.cos(ang), jnp.sin(ang)
    return jnp.concatenate([cos, cos], axis=-1), jnp.concatenate([-sin, sin], axis=-1)


def _split_pairs(t):
    lead = t.shape[:-1]
    t = t.reshape(*lead, -1, ATT_HEAD_DIM // 2, 2)
    return jnp.swapaxes(t, -1, -2).reshape(*lead, -1)


def _layer(x, ctx, c, c_ctx, norm_w, w_ada, b_ada, w_in, ret_log2_decay,
           q_norm_w, k_norm_w, w_o_ret, w_o_att, w_out):
    b, s_len, _ = x.shape
    c_len = ctx.shape[1]

    mod = _adaln(jnp.concatenate([c, c_ctx[None]], axis=0), w_ada, b_ada)
    shift, scale, gate = (mod[:, i * D_MODEL:(i + 1) * D_MODEL] for i in range(3))
    lat = lambda t: t[:b].reshape(b, 1, D_MODEL)
    cx = lambda t: t[b:].reshape(1, 1, D_MODEL)

    w_bf = w_in.astype(BF16)
    for off, width in ((OFF_ATT_K, ATT_KV_W), (OFF_ATT_Q, ATT_Q_W)):
        w_bf = w_bf.at[:, off:off + width].set(_split_pairs(w_bf[:, off:off + width]))
    q_norm_w = _split_pairs(q_norm_w)
    k_norm_w = _split_pairs(k_norm_w)

    cos, sin_signed = _rope_tables(s_len)
    x2d = x.reshape(b * s_len, D_MODEL)
    px, vt_lat = _inproj_rows(x2d, norm_w, lat(scale), lat(shift), w_bf, q_norm_w, k_norm_w,
                              cos, sin_signed, IN_COLS, s_len, s_len, PROJ_TM, PROJ_TN, rope=True)
    pc, vt_ctx = _inproj_rows(ctx.reshape(b * c_len, D_MODEL), norm_w, cx(scale), cx(shift), w_bf,
                              q_norm_w, k_norm_w, cos, sin_signed,
                              KV_COLS, b * c_len, c_len, PROJ_TM, PROJ_TN_CTX, rope=False)
    px3 = px.reshape(b, s_len, IN_COLS)
    pc3 = pc.reshape(b, c_len, KV_COLS)

    y_att = _attention(px3, pc3, vt_lat, vt_ctx)

    decay3 = ret_log2_decay.astype(F32).T.reshape(RET_HEADS, 2, 1)
    y_ret = _retention(px3, pc3, decay3)

    x_new = _merge(x2d, y_ret.reshape(b * s_len, RET_V_W), y_att.reshape(b * s_len, ATT_Q_W), px,
                   lat(gate), w_o_ret.astype(BF16), w_o_att.astype(BF16), w_out.astype(BF16),
                   s_len, MERGE_TM)
    return x_new.reshape(b, s_len, D_MODEL)


def kernel(x, c, ctx, c_ctx, norm_w, w_ada, b_ada, w_in, ret_log2_decay, q_norm_w, k_norm_w,
           w_o_ret, w_o_att, w_out):
    depth = norm_w.shape[0]
    assert depth == 1, "context update between layers is not implemented"
    return _layer(x, ctx, c, c_ctx, norm_w[0], w_ada[0], b_ada[0], w_in[0], ret_log2_decay[0],
                  q_norm_w[0], k_norm_w[0], w_o_ret[0], w_o_att[0], w_out[0])
```
